```python
import math
import jax
import jax.numpy as jnp
from jax import lax
import numpy as np

D_MODEL = 1024
BATCH = 2
SEQ = 8192
DEPTH = 2

GRID_W = 64
CTX_LEN = 256
N_EVEN = (DEPTH + 1) // 2
N_ODD = DEPTH // 2
NORM_EPS = 1e-6

SSD_WIDTH = D_MODEL
SSD_HEAD_DIM = 64
SSD_HEADS = SSD_WIDTH // SSD_HEAD_DIM
SSD_GROUPS = 4
SSD_STATE = 128
SSD_CONV = 5
SSD_CHUNK = 128
CONV_CH = SSD_WIDTH + 2 * SSD_GROUPS * SSD_STATE

S5_WIDTH = D_MODEL // 2
S5_GROUP = 16
S5_GROUPS = S5_WIDTH // S5_GROUP
S5_STATE = 64
DT_MIN = 1e-3
DT_MAX = 1e-1

EVEN_CUTS = [SSD_WIDTH, SSD_WIDTH + CONV_CH, SSD_WIDTH + CONV_CH + 2 * SSD_HEADS,
             SSD_WIDTH + CONV_CH + 2 * SSD_HEADS + S5_WIDTH]
EVEN_IN = EVEN_CUTS[-1] + S5_WIDTH
EVEN_OUT = SSD_WIDTH + S5_WIDTH

ATTN_HEAD_DIM = 64
ATTN_Q_HEADS = D_MODEL // ATTN_HEAD_DIM
ATTN_KV_HEADS = 4
ATTN_Q_BLOCK = 128
Q_W = ATTN_Q_HEADS * ATTN_HEAD_DIM
KV_W = ATTN_KV_HEADS * ATTN_HEAD_DIM
ODD_CUTS = [Q_W, Q_W + KV_W, Q_W + 2 * KV_W]
ODD_IN = 2 * Q_W + 2 * KV_W
ROPE_PAIRS = ATTN_HEAD_DIM // 4
ROPE_THETA = 10000.0

kernel_name = 'hybrid_ssd_s5_gqa_prefix_dit'


def rms_norm(x):
    xf = x.astype(jnp.float32)
    return (xf * lax.rsqrt(jnp.mean(xf * xf, axis=-1, keepdims=True) + NORM_EPS)).astype(x.dtype)


def centred_depthwise_conv(x, w, b):
    ch = x.shape[-1]
    kern = w.T[:, None, :].astype(x.dtype)
    y = lax.conv_general_dilated(x, kern, window_strides=(1,),
                                 padding=[(SSD_CONV // 2, SSD_CONV // 2)],
                                 dimension_numbers=('NWC', 'WIO', 'NWC'),
                                 feature_group_count=ch)
    return y + b


def bidirectional(fn, ctx_parts, lat_parts):
    n_ctx = ctx_parts[0].shape[1]
    rev = lambda a: jnp.flip(a, axis=1)
    y_f = fn(0, *[jnp.concatenate([pc, pl], axis=1) for pc, pl in zip(ctx_parts, lat_parts)])
    y_b = fn(1, *[jnp.concatenate([rev(pc), rev(pl)], axis=1) for pc, pl in zip(ctx_parts, lat_parts)])
    y_ctx = y_f[:, :n_ctx] + rev(y_b[:, :n_ctx])
    y_lat = y_f[:, n_ctx:] + rev(y_b[:, n_ctx:])
    return y_ctx, y_lat


def ssd_chunked(xs, dt, a, bm, cm):
    b, t, h, p = xs.shape
    nc = t // SSD_CHUNK
    r = lambda z: z.reshape(b, nc, SSD_CHUNK, *z.shape[2:])
    loga = r(dt * a)
    xd = r(xs * dt[..., None])
    bm, cm = r(bm), r(cm)
    a_cs = jnp.cumsum(loga, axis=2)
    seg = a_cs[:, :, :, None, :] - a_cs[:, :, None, :, :]
    causal = jnp.tril(jnp.ones((SSD_CHUNK, SSD_CHUNK), dtype=bool))[None, None, :, :, None]
    lmat = jnp.exp(jnp.where(causal, seg, -jnp.inf))
    scores = jnp.einsum('bclhn,bcshn->bclsh', cm, bm) * lmat
    y_diag = jnp.einsum('bclsh,bcshp->bclhp', scores, xd)
    decay_to_end = jnp.exp(a_cs[:, :, -1:, :] - a_cs)
    chunk_states = jnp.einsum('bclhn,bclh,bclhp->bchpn', bm, decay_to_end, xd)
    chunk_decay = jnp.exp(a_cs[:, :, -1, :])

    def step(carry, inp):
        st, dec = inp
        return carry * dec[..., None, None] + st, carry

    init = jnp.zeros_like(chunk_states[:, 0])
    _, prev = lax.scan(step, init, (jnp.moveaxis(chunk_states, 1, 0), jnp.moveaxis(chunk_decay, 1, 0)))
    prev = jnp.moveaxis(prev, 0, 1)
    y_off = jnp.einsum('bclhn,bchpn,bclh->bclhp', cm, prev, jnp.exp(a_cs))
    return (y_diag + y_off).reshape(b, t, h, p)


def _complex_affine_combine(e1, e2):
    a1r, a1i, b1r, b1i = e1
    a2r, a2i, b2r, b2i = e2
    ar = a2r * a1r - a2i * a1i
    ai = a2r * a1i + a2i * a1r
    br = a2r * b1r - a2i * b1i + b2r
    bi = a2r * b1i + a2i * b1r + b2i
    return ar, ai, br, bi


def s5_scan(u, lam_re, lam_im, log_step, b_re, b_im, c_re, c_im):
    f32 = jnp.float32
    lr, li = lam_re.astype(f32), lam_im.astype(f32)
    step = jnp.exp(log_step.astype(f32))[:, None]
    mag = jnp.exp(lr * step)
    ab_re, ab_im = mag * jnp.cos(li * step), mag * jnp.sin(li * step)
    den = lr * lr + li * li
    f_re = ((ab_re - 1.0) * lr + ab_im * li) / den
    f_im = (ab_im * lr - (ab_re - 1.0) * li) / den
    br, bi = b_re.astype(f32), b_im.astype(f32)
    bb_re = f_re[..., None] * br - f_im[..., None] * bi
    bb_im = f_re[..., None] * bi + f_im[..., None] * br
    uf = u.astype(f32)
    bu_re = jnp.einsum('btgi,gpi->btgp', uf, bb_re)
    bu_im = jnp.einsum('btgi,gpi->btgp', uf, bb_im)
    a_re = jnp.broadcast_to(ab_re, bu_re.shape)
    a_im = jnp.broadcast_to(ab_im, bu_im.shape)
    _, _, h_re, h_im = lax.associative_scan(_complex_affine_combine, (a_re, a_im, bu_re, bu_im), axis=1)
    return (jnp.einsum('btgp,gip->btgi', h_re, c_re.astype(f32))
            - jnp.einsum('btgp,gip->btgi', h_im, c_im.astype(f32)))


def ssm_mixer(a_ctx, a_lat, w_in, conv_w, conv_b, dt_bias, a_log, d_ssd, ssd_norm,
              lam_re, lam_im, log_step, b_re, b_im, c_re, c_im, d_s5, glu_w, glu_b, w_out):
    def project(h):
        z, xbc, dt_raw, u, g = jnp.split(h @ w_in, EVEN_CUTS, axis=-1)
        xbc = jax.nn.silu(centred_depthwise_conv(xbc, conv_w, conv_b))
        return z, xbc, dt_raw, u, g

    zc, xbc_c, dtc, uc, gc = project(a_ctx)
    zl, xbc_l, dtl, ul, gl = project(a_lat)
    rep = SSD_HEADS // SSD_GROUPS

    def ssd_dir(d, xbc, dt_raw):
        b, t = xbc.shape[:2]
        xs, bm, cm = jnp.split(xbc, [SSD_WIDTH, SSD_WIDTH + SSD_GROUPS * SSD_STATE], axis=-1)
        xs = xs.reshape(b, t, SSD_HEADS, SSD_HEAD_DIM)
        bm = jnp.repeat(bm.reshape(b, t, SSD_GROUPS, SSD_STATE), rep, axis=2)
        cm = jnp.repeat(cm.reshape(b, t, SSD_GROUPS, SSD_STATE), rep, axis=2)
        dt = jax.nn.softplus(dt_raw.reshape(b, t, 2, SSD_HEADS)[:, :, d].astype(jnp.float32)
                             + dt_bias[d].astype(jnp.float32))
        a = -jnp.exp(a_log[d].astype(jnp.float32))
        return ssd_chunked(xs, dt, a, bm, cm).reshape(b, t, SSD_WIDTH)

    ys_c, ys_l = bidirectional(ssd_dir, (xbc_c, dtc), (xbc_l, dtl))
    d_full = jnp.repeat(d_ssd, SSD_HEAD_DIM)

    def ssd_out(y, xbc, z):
        y = y + d_full * xbc[..., :SSD_WIDTH]
        return rms_norm(y * jax.nn.silu(z)) * ssd_norm

    def s5_dir(d, u):
        b, t = u.shape[:2]
        y = s5_scan(u.reshape(b, t, S5_GROUPS, S5_GROUP), lam_re[d], lam_im[d], log_step[d],
                    b_re, b_im, c_re[d], c_im[d])
        return y.reshape(b, t, S5_WIDTH)

    y5_c, y5_l = bidirectional(s5_dir, (uc,), (ul,))

    def s5_out(y, u, g):
        y = jax.nn.gelu(y + d_s5 * u)
        y = y * jax.nn.sigmoid(y @ glu_w + glu_b)
        return y * jax.nn.silu(g)

    out_c = jnp.concatenate([ssd_out(ys_c, xbc_c, zc), s5_out(y5_c, uc, gc)], axis=-1) @ w_out
    out_l = jnp.concatenate([ssd_out(ys_l, xbc_l, zl), s5_out(y5_l, ul, gl)], axis=-1) @ w_out
    return out_c, out_l


def axial_rope_tables(n_tokens):
    rows = n_tokens // GRID_W
    row = jnp.repeat(jnp.arange(rows), GRID_W).astype(jnp.float32)
    col = jnp.tile(jnp.arange(GRID_W), rows).astype(jnp.float32)
    inv = ROPE_THETA ** (-jnp.arange(ROPE_PAIRS, dtype=jnp.float32) / ROPE_PAIRS)
    ang = jnp.concatenate([row[:, None] * inv, col[:, None] * inv], axis=-1)
    return jnp.cos(ang), jnp.sin(ang)


def apply_axial_rope(x, cos, sin):
    b, n, h, _ = x.shape
    xr = x.reshape(b, n, h, 2, 2, ROPE_PAIRS)
    x1, x2 = xr[..., 0, :], xr[..., 1, :]
    cs = cos.reshape(n, 2, ROPE_PAIRS)[None, :, None]
    sn = sin.reshape(n, 2, ROPE_PAIRS)[None, :, None]
    out = jnp.stack([x1 * cs - x2 * sn, x2 * cs + x1 * sn], axis=-2)
    return out.reshape(x.shape).astype(x.dtype)


def blocked_attention(q, k, v):
    b, t = q.shape[:2]
    nb = t // ATTN_Q_BLOCK
    grp = ATTN_Q_HEADS // ATTN_KV_HEADS
    qb = q.reshape(b, nb, ATTN_Q_BLOCK, ATTN_KV_HEADS, grp, ATTN_HEAD_DIM).swapaxes(0, 1)
    scale = ATTN_HEAD_DIM ** -0.5

    def block(qi):
        s = jnp.einsum('bqkgd,bskd->bkgqs', qi, k).astype(jnp.float32) * scale
        p = jax.nn.softmax(s, axis=-1).astype(v.dtype)
        return jnp.einsum('bkgqs,bskd->bqkgd', p, v)

    o = lax.map(block, qb)
    return o.swapaxes(0, 1).reshape(b, t, Q_W)


def attention_mixer(a_ctx, a_lat, w_in, q_gain, k_gain, w_out, cos, sin, need_ctx):
    b, n_lat = a_lat.shape[:2]
    n_ctx = a_ctx.shape[1]
    heads = lambda z, n, h: z.reshape(b, n, h, ATTN_HEAD_DIM)
    q_l, k_l, v_l, g_l = jnp.split(a_lat @ w_in, ODD_CUTS, axis=-1)
    q_l = apply_axial_rope(rms_norm(heads(q_l, n_lat, ATTN_Q_HEADS)) * q_gain, cos, sin)
    k_l = apply_axial_rope(rms_norm(heads(k_l, n_lat, ATTN_KV_HEADS)) * k_gain, cos, sin)
    v_l = heads(v_l, n_lat, ATTN_KV_HEADS)
    if need_ctx:
        q_c, k_c, v_c, g_c = jnp.split(a_ctx @ w_in, ODD_CUTS, axis=-1)
    else:
        k_c, v_c = jnp.split(a_ctx @ w_in[:, Q_W:Q_W + 2 * KV_W], [KV_W], axis=-1)
    k_c = rms_norm(heads(k_c, n_ctx, ATTN_KV_HEADS)) * k_gain
    v_c = heads(v_c, n_ctx, ATTN_KV_HEADS)
    k_all = jnp.concatenate([k_c, k_l], axis=1)
    v_all = jnp.concatenate([v_c, v_l], axis=1)
    o_l = blocked_attention(q_l, k_all, v_all)
    out_l = (o_l * jax.nn.silu(g_l)) @ w_out
    out_c = None
    if need_ctx:
        q_c = rms_norm(heads(q_c, n_ctx, ATTN_Q_HEADS)) * q_gain
        o_c = blocked_attention(q_c, k_c, v_c)
        out_c = (o_c * jax.nn.silu(g_c)) @ w_out
    return out_c, out_l


def setup_inputs(seed: int = 0) -> dict:
    key = jax.random.key(seed)
    k = jax.random.split(key, 32)
    f32 = jnp.float32
    nrm = lambda i, shape, s: jax.random.normal(k[i], shape, f32) * s
    log_u = lambda i, shape, lo, hi: jax.random.uniform(k[i], shape, f32, math.log(lo), math.log(hi))
    dt0 = jnp.exp(log_u(9, (N_EVEN, 2, SSD_HEADS), DT_MIN, DT_MAX))
    lam_im0 = jnp.pi * jnp.arange(S5_STATE, dtype=f32)
    return {
        'x': nrm(0, (BATCH, SEQ, D_MODEL), 1.0),
        'c': nrm(1, (BATCH, D_MODEL), 1.0),
        'ctx': nrm(2, (BATCH, CTX_LEN, D_MODEL), 1.0),
        'c_ctx': nrm(3, (D_MODEL,), 1.0),
        'ada_w': nrm(4, (DEPTH, D_MODEL, 3 * D_MODEL), 0.5 * D_MODEL ** -0.5),
        'ada_b': nrm(5, (DEPTH, 3 * D_MODEL), 0.01),
        'ev_w_in': nrm(6, (N_EVEN, D_MODEL, EVEN_IN), D_MODEL ** -0.5),
        'ev_conv_w': nrm(7, (N_EVEN, CONV_CH, SSD_CONV), SSD_CONV ** -0.5),
        'ev_conv_b': nrm(8, (N_EVEN, CONV_CH), 0.01),
        'ev_dt_bias': dt0 + jnp.log(-jnp.expm1(-dt0)),
        'ev_a_log': jnp.log(jax.random.uniform(k[10], (N_EVEN, 2, SSD_HEADS), f32, 1.0, 16.0)),
        'ev_d_ssd': 1.0 + nrm(11, (N_EVEN, SSD_HEADS), 0.01),
        'ev_ssd_norm': 1.0 + nrm(12, (N_EVEN, SSD_WIDTH), 0.01),
        'ev_lam_re': -0.5 + nrm(13, (N_EVEN, 2, S5_GROUPS, S5_STATE), 0.01),
        'ev_lam_im': lam_im0 + nrm(14, (N_EVEN, 2, S5_GROUPS, S5_STATE), 0.01),
        'ev_log_step': log_u(15, (N_EVEN, 2, S5_GROUPS), DT_MIN, DT_MAX),
        'ev_b_re': nrm(16, (N_EVEN, S5_GROUPS, S5_STATE, S5_GROUP), (2 * S5_GROUP) ** -0.5),
        'ev_b_im': nrm(17, (N_EVEN, S5_GROUPS, S5_STATE, S5_GROUP), (2 * S5_GROUP) ** -0.5),
        'ev_c_re': nrm(18, (N_EVEN, 2, S5_GROUPS, S5_GROUP, S5_STATE), (2 * S5_STATE) ** -0.5),
        'ev_c_im': nrm(19, (N_EVEN, 2, S5_GROUPS, S5_GROUP, S5_STATE), (2 * S5_STATE) ** -0.5),
        'ev_d_s5': nrm(20, (N_EVEN, S5_WIDTH), 1.0),
        'ev_glu_w': nrm(21, (N_EVEN, S5_WIDTH, S5_WIDTH), S5_WIDTH ** -0.5),
        'ev_glu_b': nrm(22, (N_EVEN, S5_WIDTH), 0.01),
        'ev_w_out': nrm(23, (N_EVEN, EVEN_OUT, D_MODEL), EVEN_OUT ** -0.5),
        'od_w_in': nrm(24, (N_ODD, D_MODEL, ODD_IN), D_MODEL ** -0.5),
        'od_q_gain': 1.0 + nrm(25, (N_ODD, ATTN_HEAD_DIM), 0.01),
        'od_k_gain': 1.0 + nrm(26, (N_ODD, ATTN_HEAD_DIM), 0.01),
        'od_w_out': nrm(27, (N_ODD, Q_W, D_MODEL), Q_W ** -0.5),
        'final_gain': 1.0 + nrm(28, (D_MODEL,), 0.01),
    }


def reference(x, c, ctx, c_ctx, ada_w, ada_b, ev_w_in, ev_conv_w, ev_conv_b, ev_dt_bias, ev_a_log,
              ev_d_ssd, ev_ssd_norm, ev_lam_re, ev_lam_im, ev_log_step, ev_b_re, ev_b_im, ev_c_re,
              ev_c_im, ev_d_s5, ev_glu_w, ev_glu_b, ev_w_out, od_w_in, od_q_gain, od_k_gain,
              od_w_out, final_gain):
    cos, sin = axial_rope_tables(x.shape[1])
    sc = jax.nn.silu(c)
    scc = jax.nn.silu(c_ctx)
    h_lat, h_ctx = x, ctx
    for i in range(DEPTH):
        shift, scale, gate = jnp.split(sc @ ada_w[i] + ada_b[i], 3, axis=-1)
        shift_c, scale_c, gate_c = jnp.split(scc @ ada_w[i] + ada_b[i], 3, axis=-1)
        a_lat = rms_norm(h_lat) * (1.0 + scale[:, None]) + shift[:, None]
        a_ctx = rms_norm(h_ctx) * (1.0 + scale_c) + shift_c
        last = i == DEPTH - 1
        j = i // 2
        if i % 2 == 0:
            o_ctx, o_lat = ssm_mixer(a_ctx, a_lat, ev_w_in[j], ev_conv_w[j], ev_conv_b[j], ev_dt_bias[j],
                                     ev_a_log[j], ev_d_ssd[j], ev_ssd_norm[j], ev_lam_re[j], ev_lam_im[j],
                                     ev_log_step[j], ev_b_re[j], ev_b_im[j], ev_c_re[j], ev_c_im[j],
                                     ev_d_s5[j], ev_glu_w[j], ev_glu_b[j], ev_w_out[j])
        else:
            o_ctx, o_lat = attention_mixer(a_ctx, a_lat, od_w_in[j], od_q_gain[j], od_k_gain[j],
                                           od_w_out[j], cos, sin, not last)
        h_lat = h_lat + gate[:, None] * o_lat
        if not last:
            h_ctx = h_ctx + gate_c * o_ctx
    return rms_norm(h_lat) * final_gain
```

```python
import functools
import math

import jax
import jax.numpy as jnp
from jax import lax
from jax.experimental import pallas as pl
from jax.experimental.pallas import tpu as pltpu

F32 = jnp.float32
BF16 = jnp.bfloat16
HIGHEST = lax.Precision.HIGHEST

NORM_EPS = 1e-6
GRID_W = 64
ROPE_THETA = 10000.0

HEAD_DIM = 64
SSD_HEADS = 16
SSD_GROUPS = 4
SSD_STATE = 128
SSD_CHUNK = 128
SSD_CONV = 5
S5_GROUP = 16
S5_STATE = 64
S5_BLOCK = 16
ATTN_Q_HEADS = 16
ATTN_KV_HEADS = 4

TOKEN_TILE = 256
HALO_ROWS = 8
VMEM_LIMIT = 56 << 20


def _params(semantics):
    return pltpu.CompilerParams(dimension_semantics=semantics, vmem_limit_bytes=VMEM_LIMIT)


def _sigmoid(x):
    return 1.0 / (1.0 + jnp.exp(-x))


def _silu(x):
    return x * _sigmoid(x)


def _softplus(x):
    return jnp.maximum(x, 0.0) + jnp.log(1.0 + jnp.exp(-jnp.abs(x)))


def _gelu_tanh(x):
    return 0.5 * x * (1.0 + jnp.tanh(math.sqrt(2.0 / math.pi) * (x + 0.044715 * (x * x * x))))


def _rms(x):
    return x * lax.rsqrt(jnp.mean(x * x, axis=-1, keepdims=True) + NORM_EPS)


def _norm_mod(h, mod):
    return _rms(h) * (1.0 + mod[1:2]) + mod[0:1]


def _split_bf16(x):
    hi = x.astype(BF16)
    return hi, (x - hi.astype(F32)).astype(BF16)


def _adaln_kernel(s_ref, w_ref, b_ref, o_ref):
    s = _silu(s_ref[...])
    o_ref[0] = jnp.dot(s, w_ref[0], preferred_element_type=F32, precision=HIGHEST) + b_ref[0]


def _adaln(c, c_ctx, ada_w, ada_b):
    depth, d, d3 = ada_w.shape
    b = c.shape[0]
    assert b < 8
    rows = jnp.zeros((8, d), F32).at[:b].set(c).at[b].set(c_ctx)
    out = pl.pallas_call(
        _adaln_kernel,
        grid=(depth, d3 // d),
        in_specs=[
            pl.BlockSpec((8, d), lambda i, j: (0, 0)),
            pl.BlockSpec((1, d, d), lambda i, j: (i, 0, j)),
            pl.BlockSpec((1, 1, d), lambda i, j: (i, 0, j)),
        ],
        out_specs=pl.BlockSpec((1, 8, d), lambda i, j: (i, 0, j)),
        out_shape=jax.ShapeDtypeStruct((depth, 8, d3), F32),
        compiler_params=_params(("arbitrary", "arbitrary")),
        name="adaln",
    )(rows, ada_w, ada_b.reshape(depth, 1, d3))
    m = out.reshape(depth, 8, 3, d)
    lat = m[:, :b]
    ctx = jnp.broadcast_to(m[:, b:b + 1], lat.shape)
    return jnp.stack([ctx, lat], axis=2)


def _even_in_kernel(h_ref, mod_ref, w_ref, z_ref, xbc_ref, u_ref, g_ref, dt_ref, *, cuts):
    a = _norm_mod(h_ref[0], mod_ref[0, 0])
    a_hi, a_lo = _split_bf16(a)

    def mm(x, lo, hi):
        return jnp.dot(x, w_ref[:, lo:hi], preferred_element_type=F32)

    c0, c1, c2, c3, c4, c5 = cuts
    z_ref[0] = mm(a_hi, 0, c0)
    xbc_ref[0] = mm(a_hi, c0, c1)
    u_ref[0] = mm(a_hi, c1, c2).astype(u_ref.dtype)
    g_ref[0] = mm(a_hi, c2, c3)
    dt_ref[0] = mm(a_hi, c3, c4) + mm(a_lo, c3, c4) + mm(a_hi, c4, c5)


def _even_in(h, mod, w, n_ctx_tiles, widths):
    b, t, d = h.shape
    tm = TOKEN_TILE
    wz, wx, wu, wg = widths
    cuts = (wz, wz + wx, wz + wx + wu, wz + wx + wu + wg, wz + wx + wu + wg + 128, wz + wx + wu + wg + 256)
    assert w.shape == (d, cuts[-1])
    tok = lambda n: pl.BlockSpec((1, tm, n), lambda i, j: (i, j, 0))
    return pl.pallas_call(
        functools.partial(_even_in_kernel, cuts=cuts),
        grid=(b, t // tm),
        in_specs=[
            tok(d),
            pl.BlockSpec((1, 1, 3, d), lambda i, j: (i, (j >= n_ctx_tiles).astype(jnp.int32), 0, 0)),
            pl.BlockSpec(w.shape, lambda i, j: (0, 0)),
        ],
        out_specs=[tok(wz), tok(wx), tok(wu), tok(wg), tok(128)],
        out_shape=[
            jax.ShapeDtypeStruct((b, t, wz), F32),
            jax.ShapeDtypeStruct((b, t, wx), F32),
            jax.ShapeDtypeStruct((b, t, wu), F32),
            jax.ShapeDtypeStruct((b, t, wg), F32),
            jax.ShapeDtypeStruct((b, t, 128), F32),
        ],
        compiler_params=_params(("parallel", "parallel")),
        name="even_in",
    )(h, mod, w)


def _conv_kernel(prev_ref, main_ref, next_ref, w_ref, b_ref, o_ref, xe_ref, *, n_ctx_tiles, n_tiles):
    tm = TOKEN_TILE
    ch = o_ref.shape[-1]
    t = pl.program_id(1)
    has_prev = jnp.logical_and(t != 0, t != n_ctx_tiles)
    has_next = jnp.logical_and(t != n_ctx_tiles - 1, t != n_tiles - 1)
    xe_ref[0:HALO_ROWS] = jnp.where(has_prev, prev_ref[0], 0.0)
    xe_ref[HALO_ROWS:HALO_ROWS + tm] = main_ref[0]
    xe_ref[HALO_ROWS + tm:2 * HALO_ROWS + tm] = jnp.where(has_next, next_ref[0], 0.0)
    first = HALO_ROWS - SSD_CONV // 2
    rows, lanes = 32, 512
    for r0 in range(0, tm, rows):
        for c0 in range(0, ch, lanes):
            acc = b_ref[:, c0:c0 + lanes] + w_ref[0:1, c0:c0 + lanes] * xe_ref[first + r0:first + r0 + rows, c0:c0 + lanes]
            for k in range(1, SSD_CONV):
                acc = acc + w_ref[k:k + 1, c0:c0 + lanes] * xe_ref[first + k + r0:first + k + r0 + rows, c0:c0 + lanes]
            o_ref[0, r0:r0 + rows, c0:c0 + lanes] = _silu(acc)


def _conv(xbc, conv_w, conv_b, n_ctx_tiles):
    b, t, ch = xbc.shape
    tm = TOKEN_TILE
    n_tiles = t // tm
    per = tm // HALO_ROWS
    last = t // HALO_ROWS - 1
    w = jnp.zeros((8, ch), F32).at[:SSD_CONV].set(conv_w.T)
    return pl.pallas_call(
        functools.partial(_conv_kernel, n_ctx_tiles=n_ctx_tiles, n_tiles=n_tiles),
        grid=(b, n_tiles),
        in_specs=[
            pl.BlockSpec((1, HALO_ROWS, ch), lambda i, j: (i, jnp.maximum(j * per - 1, 0), 0)),
            pl.BlockSpec((1, tm, ch), lambda i, j: (i, j, 0)),
            pl.BlockSpec((1, HALO_ROWS, ch), lambda i, j: (i, jnp.minimum((j + 1) * per, last), 0)),
            pl.BlockSpec((8, ch), lambda i, j: (0, 0)),
            pl.BlockSpec((1, ch), lambda i, j: (0, 0)),
        ],
        out_specs=pl.BlockSpec((1, tm, ch), lambda i, j: (i, j, 0)),
        out_shape=jax.ShapeDtypeStruct((b, t, ch), F32),
        scratch_shapes=[pltpu.VMEM((tm + 2 * HALO_ROWS, ch), F32)],
        compiler_params=_params(("parallel", "parallel")),
        name="conv",
    )(xbc, xbc, xbc, w, conv_b.reshape(1, ch))


def _ssd_kernel(xf_ref, xb_ref, dtf_ref, dtb_ref, bias_ref, a_ref, e_ref, yf_ref, yb_ref, st_ref):
    L = SSD_CHUNK
    width = SSD_HEADS * HEAD_DIM
    gw = width // SSD_GROUPS
    hpg = SSD_HEADS // SSD_GROUPS

    @pl.when(pl.program_id(1) == 0)
    def _():
        st_ref[...] = jnp.zeros_like(st_ref)

    row = lax.broadcasted_iota(jnp.int32, (L, L), 0)
    col = lax.broadcasted_iota(jnp.int32, (L, L), 1)
    lane_head = lax.broadcasted_iota(jnp.int32, (L, gw), 1) // HEAD_DIM
    sub16 = lax.broadcasted_iota(jnp.int32, (16, 128), 0)

    for d, (x_ref, dt_ref, y_ref) in enumerate(((xf_ref, dtf_ref, yf_ref), (xb_ref, dtb_ref, yb_ref))):
        mask = (row >= col) if d == 0 else (row <= col)
        dtv = _softplus(dt_ref[0] + bias_ref[...])
        loga = dtv * a_ref[...]
        cs = jnp.dot(mask.astype(F32), loga, preferred_element_type=F32, precision=HIGHEST)
        cs_t = cs.T
        total = jnp.sum(loga, axis=0, keepdims=True)
        t1 = total.astype(BF16).astype(F32)
        t2 = (total - t1).astype(BF16).astype(F32)
        t3 = total - t1 - t2
        tot = jnp.where(sub16 == 0, t1, jnp.where(sub16 == 1, t2, jnp.where(sub16 == 2, t3, 0.0)))
        stack = jnp.concatenate(
            [dtv.astype(BF16), jnp.exp(cs).astype(BF16), jnp.exp(total - cs).astype(BF16), tot.astype(BF16)], axis=0)
        ex = jnp.dot(stack, e_ref[d], preferred_element_type=F32)
        dt_x = ex[0:L]
        ecs_x = ex[L:2 * L]
        wend_x = ex[2 * L:3 * L]
        edec_x = jnp.exp(ex[3 * L:3 * L + 1] + ex[3 * L + 1:3 * L + 2] + ex[3 * L + 2:3 * L + 3])

        for g in range(SSD_GROUPS):
            sl = slice(g * gw, (g + 1) * gw)
            xs_g = x_ref[0, :, sl]
            b_g = x_ref[0, :, width + g * SSD_STATE:width + (g + 1) * SSD_STATE]
            c_g = x_ref[0, :, width + (SSD_GROUPS + g) * SSD_STATE:width + (SSD_GROUPS + g + 1) * SSD_STATE]
            cb = c_g.astype(BF16)
            xd = xs_g * dt_x[:, sl]
            gram = lax.dot_general(cb, b_g.astype(BF16), (((1,), (1,)), ((), ())), preferred_element_type=F32)
            scores = []
            for hh in range(hpg):
                li = SSD_HEADS * d + hpg * g + hh
                seg = cs[:, li:li + 1] - cs_t[li:li + 1, :]
                scores.append((gram * jnp.where(mask, jnp.exp(seg), 0.0)).astype(BF16))
            scores = jnp.concatenate(scores, axis=1)
            xd_blocks = jnp.concatenate(
                [jnp.where(lane_head == hh, xd, 0.0).astype(BF16) for hh in range(hpg)], axis=0)
            y = jnp.dot(scores, xd_blocks, preferred_element_type=F32)
            st = st_ref[d, :, sl]
            y = y + jnp.dot(cb, st.astype(BF16), preferred_element_type=F32) * ecs_x[:, sl]
            y_ref[0, :, sl] = y
            xdw = (xd * wend_x[:, sl]).astype(BF16)
            st_ref[d, :, sl] = st * edec_x[:, sl] + jnp.dot(b_g.T.astype(BF16), xdw, preferred_element_type=F32)


def _ssd(xbc, dt, dt_bias, a_log, n_ctx_chunks):
    b, t, ch = xbc.shape
    L = SSD_CHUNK
    n = t // L
    width = SSD_HEADS * HEAD_DIM
    ncc = n_ctx_chunks

    def fwd(i, j):
        return (i, j, 0)

    def bwd(i, j):
        return (i, jnp.where(j < ncc, ncc - 1 - j, n + ncc - 1 - j), 0)

    lanes = jnp.zeros((128,), F32)
    bias = lanes.at[:2 * SSD_HEADS].set(dt_bias.reshape(-1)).reshape(1, 128)
    a_neg = lanes.at[:2 * SSD_HEADS].set(-jnp.exp(a_log.reshape(-1))).reshape(1, 128)
    head_of_lane = jnp.arange(width) // HEAD_DIM
    expand = jnp.stack([
        (jnp.arange(128)[:, None] == SSD_HEADS * d + head_of_lane[None, :]) for d in range(2)
    ]).astype(BF16)
    return pl.pallas_call(
        _ssd_kernel,
        grid=(b, n),
        in_specs=[
            pl.BlockSpec((1, L, ch), fwd),
            pl.BlockSpec((1, L, ch), bwd),
            pl.BlockSpec((1, L, 128), fwd),
            pl.BlockSpec((1, L, 128), bwd),
            pl.BlockSpec((1, 128), lambda i, j: (0, 0)),
            pl.BlockSpec((1, 128), lambda i, j: (0, 0)),
            pl.BlockSpec((2, 128, width), lambda i, j: (0, 0, 0)),
        ],
        out_specs=[pl.BlockSpec((1, L, width), fwd), pl.BlockSpec((1, L, width), bwd)],
        out_shape=[jax.ShapeDtypeStruct((b, t, width), F32)] * 2,
        scratch_shapes=[pltpu.VMEM((2, SSD_STATE, width), F32)],
        compiler_params=_params(("parallel", "arbitrary")),
        name="ssd",
    )(xbc, xbc, dt, dt, bias, a_neg, expand)


def _s5_operators(lam_re, lam_im, log_step, b_re, b_im, c_re, c_im):
    nb = S5_BLOCK
    ng, ns = lam_re.shape[1:]
    hp = functools.partial(jnp.einsum, precision=HIGHEST)
    step = jnp.exp(log_step)[..., None]
    k = jnp.arange(nb + 1, dtype=F32)[:, None, None, None]
    mag = jnp.exp(k * (lam_re * step))
    ak_re = mag * jnp.cos(k * (lam_im * step))
    ak_im = mag * jnp.sin(k * (lam_im * step))
    ab_re, ab_im = ak_re[1], ak_im[1]
    den = lam_re * lam_re + lam_im * lam_im
    f_re = ((ab_re - 1.0) * lam_re + ab_im * lam_im) / den
    f_im = (ab_im * lam_re - (ab_re - 1.0) * lam_im) / den
    bb_re = f_re[..., None] * b_re - f_im[..., None] * b_im
    bb_im = f_re[..., None] * b_im + f_im[..., None] * b_re
    w_re = ak_re[..., None] * bb_re - ak_im[..., None] * bb_im
    w_im = ak_re[..., None] * bb_im + ak_im[..., None] * bb_re
    kern = hp('dgip,kdgpj->kdgij', c_re, w_re[:nb]) - hp('dgip,kdgpj->kdgij', c_im, w_im[:nb])
    kf, kb = kern[:, 0], kern[:, 1]
    zall = jnp.concatenate([kb[:0:-1], (kf[0] + kb[0])[None], kf[1:]], axis=0)
    idx = jnp.arange(nb)[None, :] - jnp.arange(nb)[:, None] + nb - 1
    m = zall[idx]
    m = m.transpose(2, 0, 4, 1, 3).reshape(ng, nb * S5_GROUP, nb * S5_GROUP)
    inj = lambda w, d, rev: (w[nb - 1::-1, d] if rev else w[:nb, d]).transpose(1, 0, 3, 2)
    s_all = jnp.stack([inj(w_re, 0, True), inj(w_re, 1, False), inj(w_im, 0, True), inj(w_im, 1, False)], axis=3)
    s_all = s_all.reshape(ng, nb * S5_GROUP, 4, ns)
    def readout(d, ks):
        ar, ai = ak_re[ks, d], ak_im[ks, d]
        cr, ci = c_re[d], c_im[d]
        o_re = cr[None] * ar[:, :, None, :] - ci[None] * ai[:, :, None, :]
        o_im = -(cr[None] * ai[:, :, None, :] + ci[None] * ar[:, :, None, :])
        fix = lambda o: o.transpose(1, 3, 0, 2).reshape(ng, ns, nb * S5_GROUP)
        return fix(o_re), fix(o_im)
    of_re, of_im = readout(0, jnp.arange(1, nb + 1))
    ob_re, ob_im = readout(1, jnp.arange(nb, 0, -1))
    o_all = jnp.stack([of_re, ob_re, of_im, ob_im], axis=1)
    eye = jnp.eye(2, dtype=F32)
    npair = ng // 2
    smat = jnp.einsum('qarcp,ab->qarcbp', s_all.reshape(npair, 2, nb * S5_GROUP, 4, ns), eye)
    smat = smat.reshape(npair, 2 * nb * S5_GROUP, 8 * ns)
    m_pair = jnp.einsum('qars,ab->qarbs', m.reshape(npair, 2, nb * S5_GROUP, nb * S5_GROUP), eye)
    m_pair = m_pair.reshape(npair, 2 * nb * S5_GROUP, 2 * nb * S5_GROUP)
    o_pair = jnp.einsum('qacps,ab->qcapbs', o_all.reshape(npair, 2, 4, ns, nb * S5_GROUP), eye)
    o_pair = o_pair.reshape(npair, 8 * ns, 2 * nb * S5_GROUP)
    yw = jnp.concatenate([m_pair, o_pair], axis=1)
    dec = jnp.stack([ak_re[nb, 0], ak_im[nb, 0], ak_re[nb, 1], ak_im[nb, 1]]).reshape(4, ng * ns)
    return smat.astype(BF16), yw.astype(BF16), dec


def _s5_kernel(u_ref, smat_ref, yw_ref, dec_ref, y_ref, sfr, sbr, sfi, sbi, hfr, hbr, hfi, hbi, *, n_ctx_blocks):
    pairs = u_ref.shape[1]
    n = u_ref.shape[2]
    ncb = n_ctx_blocks
    lw = 2 * S5_STATE
    for p in range(pairs):
        s = jnp.dot(u_ref[0, p], smat_ref[p], preferred_element_type=F32)
        for c, ref in enumerate((sfr, sbr, sfi, sbi)):
            ref[:, p * lw:(p + 1) * lw] = s[:, c * lw:(c + 1) * lw]
    arf, aif, arb, aib = dec_ref[0, 0:1], dec_ref[0, 1:2], dec_ref[0, 2:3], dec_ref[0, 3:4]

    def step(i, carry):
        fr, fi, br, bi = carry
        rf = i
        rb = jnp.where(i < ncb, ncb - 1 - i, n + ncb - 1 - i)
        hfr[pl.ds(rf, 1), :] = fr
        hfi[pl.ds(rf, 1), :] = fi
        hbr[pl.ds(rb, 1), :] = br
        hbi[pl.ds(rb, 1), :] = bi
        nfr = arf * fr - aif * fi + sfr[pl.ds(rf, 1), :]
        nfi = arf * fi + aif * fr + sfi[pl.ds(rf, 1), :]
        nbr = arb * br - aib * bi + sbr[pl.ds(rb, 1), :]
        nbi = arb * bi + aib * br + sbi[pl.ds(rb, 1), :]
        return nfr, nfi, nbr, nbi

    zero = jnp.zeros((1, pairs * lw), F32)
    lax.fori_loop(0, n, step, (zero, zero, zero, zero))
    for p in range(pairs):
        sl = slice(p * lw, (p + 1) * lw)
        lhs = jnp.concatenate(
            [u_ref[0, p], hfr[:, sl].astype(BF16), hbr[:, sl].astype(BF16), hfi[:, sl].astype(BF16),
             hbi[:, sl].astype(BF16)], axis=1)
        y_ref[0, p] = jnp.dot(lhs, yw_ref[p], preferred_element_type=F32)


def _s5(u, ops, n_ctx_blocks):
    smat, yw, dec = ops
    b, t, w = u.shape
    nb = S5_BLOCK
    n = t // nb
    npair = w // (2 * S5_GROUP)
    pw = 2 * nb * S5_GROUP
    ppb = 4
    lw = 2 * S5_STATE
    uf = u.astype(BF16).reshape(b, n, nb, npair, 2, S5_GROUP).transpose(0, 3, 1, 4, 2, 5).reshape(b, npair, n, pw)
    dec = dec.reshape(4, npair // ppb, ppb * lw).transpose(1, 0, 2)
    dec = jnp.concatenate([dec, jnp.zeros_like(dec)], axis=1)
    y = pl.pallas_call(
        functools.partial(_s5_kernel, n_ctx_blocks=n_ctx_blocks),
        grid=(b, npair // ppb),
        in_specs=[
            pl.BlockSpec((1, ppb, n, pw), lambda i, j: (i, j, 0, 0)),
            pl.BlockSpec((ppb,) + smat.shape[1:], lambda i, j: (j, 0, 0)),
            pl.BlockSpec((ppb,) + yw.shape[1:], lambda i, j: (j, 0, 0)),
            pl.BlockSpec((1, 8, ppb * lw), lambda i, j: (j, 0, 0)),
        ],
        out_specs=pl.BlockSpec((1, ppb, n, pw), lambda i, j: (i, j, 0, 0)),
        out_shape=jax.ShapeDtypeStruct((b, npair, n, pw), F32),
        scratch_shapes=[pltpu.VMEM((n, ppb * lw), F32)] * 8,
        compiler_params=_params(("parallel", "parallel")),
        name="s5",
    )(uf, smat, yw, dec)
    return y.reshape(b, npair, n, 2, nb, S5_GROUP).transpose(0, 2, 4, 1, 3, 5).reshape(b, t, w)


def _even_out_kernel(yf_ref, yb_ref, xs_ref, z_ref, y5_ref, u_ref, g_ref, h_ref, mod_ref, vs_ref, v5_ref,
                     glu_ref, w_ref, o_ref):
    ws = z_ref.shape[-1]
    ys = yf_ref[0] + yb_ref[0] + vs_ref[1:2] * xs_ref[0]
    s = _rms(ys * _silu(z_ref[0])) * vs_ref[0:1]
    y = _gelu_tanh(y5_ref[0] + v5_ref[0:1] * u_ref[0])
    y = y * _sigmoid(jnp.dot(y.astype(BF16), glu_ref[...], preferred_element_type=F32) + v5_ref[1:2])
    y = y * _silu(g_ref[0])
    o = (jnp.dot(s.astype(BF16), w_ref[0:ws], preferred_element_type=F32)
         + jnp.dot(y.astype(BF16), w_ref[ws:], preferred_element_type=F32))
    o_ref[0] = h_ref[0] + mod_ref[0, 0][2:3] * o


def _even_out(yf, yb, xbc, z, y5, u, g, h, mod, ssd_norm, d_ssd, d_s5, glu_w, glu_b, w_out, n_ctx_tiles):
    b, t, d = h.shape
    tm = TOKEN_TILE
    ws, w5 = z.shape[-1], u.shape[-1]
    vs = jnp.zeros((8, ws), F32).at[0].set(ssd_norm).at[1].set(jnp.repeat(d_ssd, HEAD_DIM))
    v5 = jnp.zeros((8, w5), F32).at[0].set(d_s5).at[1].set(glu_b)
    tok = lambda n: pl.BlockSpec((1, tm, n), lambda i, j: (i, j, 0))
    const = lambda a: pl.BlockSpec(a.shape, lambda i, j: (0,) * a.ndim)
    glu_w = glu_w.astype(BF16)
    w_out = w_out.astype(BF16)
    return pl.pallas_call(
        _even_out_kernel,
        grid=(b, t // tm),
        in_specs=[
            tok(ws), tok(ws), tok(ws), tok(ws), tok(w5), tok(w5), tok(w5), tok(d),
            pl.BlockSpec((1, 1, 3, d), lambda i, j: (i, (j >= n_ctx_tiles).astype(jnp.int32), 0, 0)),
            const(vs), const(v5), const(glu_w), const(w_out),
        ],
        out_specs=tok(d),
        out_shape=jax.ShapeDtypeStruct((b, t, d), F32),
        compiler_params=_params(("parallel", "parallel")),
        name="even_out",
    )(yf, yb, xbc, z, y5, u, g, h, mod, vs, v5, glu_w, w_out)


def _odd_in_kernel(h_ref, mod_ref, w_ref, qg_ref, kg_ref, cos_ref, sin_ref, ones_ref, q_ref, k_ref, v_ref, g_ref, *,
                   q_w, kv_w):
    tm = TOKEN_TILE
    a = _norm_mod(h_ref[0], mod_ref[0, 0]).astype(BF16)
    cosv = cos_ref[...]
    sinv = sin_ref[...]
    first_half = (lax.broadcasted_iota(jnp.int32, (tm, 128), 1) % (HEAD_DIM // 2)) < (HEAD_DIM // 4)

    def head_norm_rope(lo, width, gain_ref, scale, out_ref):
        for c in range(width // 256):
            x = jnp.dot(a, w_ref[:, lo + 256 * c:lo + 256 * (c + 1)], preferred_element_type=F32)
            hi2, lo2 = _split_bf16(x * x)
            ms = (jnp.dot(hi2, ones_ref[...], preferred_element_type=F32)
                  + jnp.dot(lo2, ones_ref[...], preferred_element_type=F32)) * (1.0 / HEAD_DIM)
            xn = x * lax.rsqrt(ms + NORM_EPS) * gain_ref[:, 256 * c:256 * (c + 1)]
            for s in range(2):
                xb = xn[:, 128 * s:128 * (s + 1)]
                partner = jnp.where(first_half, pltpu.roll(xb, 128 - HEAD_DIM // 4, 1), pltpu.roll(xb, HEAD_DIM // 4, 1))
                r = (xb * cosv + partner * sinv) * scale
                out_ref[0, :, 256 * c + 128 * s:256 * c + 128 * (s + 1)] = r.astype(out_ref.dtype)

    head_norm_rope(0, q_w, qg_ref, HEAD_DIM ** -0.5, q_ref)
    head_norm_rope(q_w, kv_w, kg_ref, 1.0, k_ref)
    v_ref[0] = jnp.dot(a, w_ref[:, q_w + kv_w:q_w + 2 * kv_w], preferred_element_type=F32).astype(v_ref.dtype)
    g_ref[0] = jnp.dot(a, w_ref[:, q_w + 2 * kv_w:], preferred_element_type=F32)


def _rope_tables(n_ctx, n_lat):
    pairs = HEAD_DIM // 4
    pos = jnp.arange(n_lat)
    row = (pos // GRID_W).astype(F32)
    colp = (pos % GRID_W).astype(F32)
    inv = ROPE_THETA ** (-jnp.arange(pairs, dtype=F32) / pairs)
    lane = jnp.arange(128) % HEAD_DIM
    axis_is_col = (lane // (HEAD_DIM // 2)) == 1
    ang = jnp.where(axis_is_col[None, :], colp[:, None], row[:, None]) * inv[lane % pairs][None, :]
    sign = jnp.where((lane % (HEAD_DIM // 2)) < pairs, -1.0, 1.0)
    cos = jnp.concatenate([jnp.ones((n_ctx, 128), F32), jnp.cos(ang)], axis=0)
    sin = jnp.concatenate([jnp.zeros((n_ctx, 128), F32), jnp.sin(ang) * sign[None, :]], axis=0)
    return cos, sin


def _odd_in(h, mod, w, q_gain, k_gain, cos, sin, n_ctx_tiles):
    b, t, d = h.shape
    tm = TOKEN_TILE
    q_w = ATTN_Q_HEADS * HEAD_DIM
    kv_w = ATTN_KV_HEADS * HEAD_DIM
    qg = jnp.tile(q_gain, ATTN_Q_HEADS).reshape(1, q_w)
    kg = jnp.tile(k_gain, ATTN_KV_HEADS).reshape(1, kv_w)
    blk = jnp.arange(256) // HEAD_DIM
    ones = (blk[:, None] == blk[None, :]).astype(BF16)
    tok = lambda n: pl.BlockSpec((1, tm, n), lambda i, j: (i, j, 0))
    const = lambda a: pl.BlockSpec(a.shape, lambda i, j: (0,) * a.ndim)
    return pl.pallas_call(
        functools.partial(_odd_in_kernel, q_w=q_w, kv_w=kv_w),
        grid=(b, t // tm),
        in_specs=[
            tok(d),
            pl.BlockSpec((1, 1, 3, d), lambda i, j: (i, (j >= n_ctx_tiles).astype(jnp.int32), 0, 0)),
            const(w), const(qg), const(kg),
            pl.BlockSpec((tm, 128), lambda i, j: (j, 0)),
            pl.BlockSpec((tm, 128), lambda i, j: (j, 0)),
            const(ones),
        ],
        out_specs=[tok(q_w), tok(kv_w), tok(kv_w), tok(q_w)],
        out_shape=[
            jax.ShapeDtypeStruct((b, t, q_w), BF16),
            jax.ShapeDtypeStruct((b, t, kv_w), BF16),
            jax.ShapeDtypeStruct((b, t, kv_w), BF16),
            jax.ShapeDtypeStruct((b, t, q_w), F32),
        ],
        compiler_params=_params(("parallel", "parallel")),
        name="odd_in",
    )(h, mod, w, qg, kg, cos, sin, ones)


def _attn_kernel(q_ref, k_ref, v_ref, o_ref, *, tk):
    rep, tq, hd = q_ref.shape[1:]
    nk = k_ref.shape[2] // tk
    q = q_ref[0].reshape(rep * tq, hd)

    def body(j, carry):
        m, l, acc = carry
        off = pl.multiple_of(j * tk, tk)
        k = k_ref[0, 0, pl.ds(off, tk), :]
        v = v_ref[0, 0, pl.ds(off, tk), :]
        s = lax.dot_general(q, k, (((1,), (1,)), ((), ())), preferred_element_type=F32)
        m_new = jnp.maximum(m, jnp.max(s, axis=-1, keepdims=True))
        alpha = jnp.exp(m - m_new)
        p = jnp.exp(s - m_new)
        l = alpha * l + jnp.sum(p, axis=-1, keepdims=True)
        acc = alpha * acc + jnp.dot(p.astype(BF16), v, preferred_element_type=F32)
        return m_new, l, acc

    init = (jnp.full((rep * tq, 1), -1e30, F32), jnp.zeros((rep * tq, 1), F32), jnp.zeros((rep * tq, hd), F32))
    _, l, acc = lax.fori_loop(0, nk, body, init)
    o_ref[0] = (acc / l).reshape(rep, tq, hd).astype(o_ref.dtype)


def _attention(q, k, v):
    b, hq, tq_all, hd = q.shape
    hkv, tk_all = k.shape[1:3]
    rep = hq // hkv
    tq = 256
    tk = next(c for c in (768, 512, 256, 128) if tk_all % c == 0)
    return pl.pallas_call(
        functools.partial(_attn_kernel, tk=tk),
        grid=(b, hkv, tq_all // tq),
        in_specs=[
            pl.BlockSpec((1, rep, tq, hd), lambda i, j, n: (i, j, n, 0)),
            pl.BlockSpec((1, 1, tk_all, hd), lambda i, j, n: (i, j, 0, 0)),
            pl.BlockSpec((1, 1, tk_all, hd), lambda i, j, n: (i, j, 0, 0)),
        ],
        out_specs=pl.BlockSpec((1, rep, tq, hd), lambda i, j, n: (i, j, n, 0)),
        out_shape=jax.ShapeDtypeStruct((b, hq, tq_all, hd), BF16),
        compiler_params=_params(("parallel", "parallel", "parallel")),
        name="attention",
    )(q, k, v)


def _attn_out_kernel(o_ref, g_ref, h_ref, mod_ref, w_ref, fg_ref, out_ref):
    x = o_ref[0].astype(F32) * _silu(g_ref[0])
    y = jnp.dot(x.astype(BF16), w_ref[...], preferred_element_type=F32)
    hn = h_ref[0] + mod_ref[0, 0][2:3] * y
    out_ref[0] = _rms(hn) * fg_ref[...]


def _attn_out(o, g, h, mod, w_out, final_gain, n_ctx_tiles):
    b, n_lat, d = o.shape
    tm = TOKEN_TILE
    lat = lambda n: pl.BlockSpec((1, tm, n), lambda i, j: (i, j + n_ctx_tiles, 0))
    w_out = w_out.astype(BF16)
    return pl.pallas_call(
        _attn_out_kernel,
        grid=(b, n_lat // tm),
        in_specs=[
            pl.BlockSpec((1, tm, d), lambda i, j: (i, j, 0)),
            lat(d), lat(d),
            pl.BlockSpec((1, 1, 3, d), lambda i, j: (i, 1, 0, 0)),
            pl.BlockSpec(w_out.shape, lambda i, j: (0, 0)),
            pl.BlockSpec((1, d), lambda i, j: (0, 0)),
        ],
        out_specs=pl.BlockSpec((1, tm, d), lambda i, j: (i, j, 0)),
        out_shape=jax.ShapeDtypeStruct((b, n_lat, d), F32),
        compiler_params=_params(("parallel", "parallel")),
        name="attn_out",
    )(o, g, h, mod, w_out, final_gain.reshape(1, d))


def kernel(x, c, ctx, c_ctx, ada_w, ada_b, ev_w_in, ev_conv_w, ev_conv_b, ev_dt_bias, ev_a_log, ev_d_ssd, ev_ssd_norm, ev_lam_re, ev_lam_im, ev_log_step, ev_b_re, ev_b_im, ev_c_re, ev_c_im, ev_d_s5, ev_glu_w, ev_glu_b, ev_w_out, od_w_in, od_q_gain, od_k_gain, od_w_out, final_gain):
    b, n_lat, d = x.shape
    n_ctx = ctx.shape[1]
    assert ada_w.shape[0] == 2 and n_ctx % TOKEN_TILE == 0 and n_lat % TOKEN_TILE == 0
    n_ctx_tiles = n_ctx // TOKEN_TILE
    mods = _adaln(c, c_ctx, ada_w, ada_b)
    h = jnp.concatenate([ctx, x], axis=1)

    ws = SSD_HEADS * HEAD_DIM
    wx = ws + 2 * SSD_GROUPS * SSD_STATE
    w5 = ev_d_s5.shape[-1]
    w = ev_w_in[0]
    cuts = (ws, ws + wx, ws + wx + 2 * SSD_HEADS, ws + wx + 2 * SSD_HEADS + w5)
    w_dt = jnp.zeros((d, 128), F32).at[:, :2 * SSD_HEADS].set(w[:, cuts[1]:cuts[2]])
    w_dt_hi, w_dt_lo = _split_bf16(w_dt)
    w_cat = jnp.concatenate(
        [w[:, :cuts[1]].astype(BF16), w[:, cuts[2]:].astype(BF16), w_dt_hi, w_dt_lo], axis=1)
    z, xbc, u, g, dt = _even_in(h, mods[0], w_cat, n_ctx_tiles, (ws, wx, w5, w5))
    xbc = _conv(xbc, ev_conv_w[0], ev_conv_b[0], n_ctx_tiles)
    yf, yb = _ssd(xbc, dt, ev_dt_bias[0], ev_a_log[0], n_ctx // SSD_CHUNK)
    ops = _s5_operators(ev_lam_re[0], ev_lam_im[0], ev_log_step[0], ev_b_re[0], ev_b_im[0], ev_c_re[0], ev_c_im[0])
    y5 = _s5(u, ops, n_ctx // S5_BLOCK)
    h = _even_out(yf, yb, xbc, z, y5, u, g, h, mods[0], ev_ssd_norm[0], ev_d_ssd[0], ev_d_s5[0], ev_glu_w[0],
                  ev_glu_b[0], ev_w_out[0], n_ctx_tiles)

    cos, sin = _rope_tables(n_ctx, n_lat)
    q, k, v, g = _odd_in(h, mods[1], od_w_in[0].astype(BF16), od_q_gain[0], od_k_gain[0], cos, sin, n_ctx_tiles)
    t = n_ctx + n_lat
    heads = lambda a, n: a.reshape(b, a.shape[1], n, HEAD_DIM).transpose(0, 2, 1, 3)
    o = _attention(heads(q[:, n_ctx:], ATTN_Q_HEADS), heads(k, ATTN_KV_HEADS), heads(v, ATTN_KV_HEADS))
    o = o.transpose(0, 2, 1, 3).reshape(b, n_lat, ATTN_Q_HEADS * HEAD_DIM)
    return _attn_out(o, g, h, mods[1], od_w_out[0], final_gain, n_ctx_tiles)
```

```python
import functools
import math

import jax
import jax.numpy as jnp
from jax import lax
from jax.experimental import pallas as pl
from jax.experimental.pallas import tpu as pltpu

F32 = jnp.float32
BF16 = jnp.bfloat16
HIGHEST = lax.Precision.HIGHEST

NORM_EPS = 1e-6
GRID_W = 64
ROPE_THETA = 10000.0

HEAD_DIM = 64
SSD_HEADS = 16
SSD_GROUPS = 4
SSD_STATE = 128
SSD_CHUNK = 128
SSD_CONV = 5
S5_GROUP = 16
S5_STATE = 64
S5_BLOCK = 16
ATTN_Q_HEADS = 16
ATTN_KV_HEADS = 4

TOKEN_TILE = 256
HALO_ROWS = 8
VMEM_LIMIT = 56 << 20


def _params(semantics):
    return pltpu.CompilerParams(dimension_semantics=semantics, vmem_limit_bytes=VMEM_LIMIT)


def _sigmoid(x):
    return 1.0 / (1.0 + jnp.exp(-x))


def _silu(x):
    return x * _sigmoid(x)


def _softplus(x):
    return jnp.maximum(x, 0.0) + jnp.log(1.0 + jnp.exp(-jnp.abs(x)))


def _gelu_tanh(x):
    return 0.5 * x * (1.0 + jnp.tanh(math.sqrt(2.0 / math.pi) * (x + 0.044715 * (x * x * x))))


def _rms(x):
    return x * lax.rsqrt(jnp.mean(x * x, axis=-1, keepdims=True) + NORM_EPS)


def _norm_mod(h, mod):
    return _rms(h) * (1.0 + mod[1:2]) + mod[0:1]


def _split_bf16(x):
    hi = x.astype(BF16)
    return hi, (x - hi.astype(F32)).astype(BF16)


def _adaln_kernel(s_ref, w_ref, b_ref, o_ref):
    s = _silu(s_ref[...])
    o_ref[0] = jnp.dot(s, w_ref[0], preferred_element_type=F32, precision=HIGHEST) + b_ref[0]


def _adaln(c, c_ctx, ada_w, ada_b):
    depth, d, d3 = ada_w.shape
    b = c.shape[0]
    assert b < 8
    rows = jnp.zeros((8, d), F32).at[:b].set(c).at[b].set(c_ctx)
    out = pl.pallas_call(
        _adaln_kernel,
        grid=(depth, d3 // d),
        in_specs=[
            pl.BlockSpec((8, d), lambda i, j: (0, 0)),
            pl.BlockSpec((1, d, d), lambda i, j: (i, 0, j)),
            pl.BlockSpec((1, 1, d), lambda i, j: (i, 0, j)),
        ],
        out_specs=pl.BlockSpec((1, 8, d), lambda i, j: (i, 0, j)),
        out_shape=jax.ShapeDtypeStruct((depth, 8, d3), F32),
        compiler_params=_params(("arbitrary", "arbitrary")),
        name="adaln",
    )(rows, ada_w, ada_b.reshape(depth, 1, d3))
    m = out.reshape(depth, 8, 3, d)
    lat = m[:, :b]
    ctx = jnp.broadcast_to(m[:, b:b + 1], lat.shape)
    return jnp.stack([ctx, lat], axis=2)


def _even_in_kernel(h_ref, mod_ref, w_ref, z_ref, xbc_ref, u_ref, g_ref, dt_ref, *, cuts):
    a = _norm_mod(h_ref[0], mod_ref[0, 0])
    a_hi, a_lo = _split_bf16(a)

    def mm(x, lo, hi):
        return jnp.dot(x, w_ref[:, lo:hi], preferred_element_type=F32)

    c0, c1, c2, c3, c4, c5 = cuts
    z_ref[0] = mm(a_hi, 0, c0)
    xbc_ref[0] = mm(a_hi, c0, c1)
    u_ref[0] = mm(a_hi, c1, c2).astype(u_ref.dtype)
    g_ref[0] = mm(a_hi, c2, c3)
    dt_ref[0] = mm(a_hi, c3, c4) + mm(a_lo, c3, c4) + mm(a_hi, c4, c5)


def _even_in(h, mod, w, n_ctx_tiles, widths):
    b, t, d = h.shape
    tm = TOKEN_TILE
    wz, wx, wu, wg = widths
    cuts = (wz, wz + wx, wz + wx + wu, wz + wx + wu + wg, wz + wx + wu + wg + 128, wz + wx + wu + wg + 256)
    assert w.shape == (d, cuts[-1])
    tok = lambda n: pl.BlockSpec((1, tm, n), lambda i, j: (i, j, 0))
    return pl.pallas_call(
        functools.partial(_even_in_kernel, cuts=cuts),
        grid=(b, t // tm),
        in_specs=[
            tok(d),
            pl.BlockSpec((1, 1, 3, d), lambda i, j: (i, (j >= n_ctx_tiles).astype(jnp.int32), 0, 0)),
            pl.BlockSpec(w.shape, lambda i, j: (0, 0)),
        ],
        out_specs=[tok(wz), tok(wx), tok(wu), tok(wg), tok(128)],
        out_shape=[
            jax.ShapeDtypeStruct((b, t, wz), F32),
            jax.ShapeDtypeStruct((b, t, wx), F32),
            jax.ShapeDtypeStruct((b, t, wu), F32),
            jax.ShapeDtypeStruct((b, t, wg), F32),
            jax.ShapeDtypeStruct((b, t, 128), F32),
        ],
        compiler_params=_params(("parallel", "parallel")),
        name="even_in",
    )(h, mod, w)


def _conv_kernel(prev_ref, main_ref, next_ref, w_ref, b_ref, o_ref, xe_ref, *, n_ctx_tiles, n_tiles):
    tm = TOKEN_TILE
    ch = o_ref.shape[-1]
    t = pl.program_id(1)
    has_prev = jnp.logical_and(t != 0, t != n_ctx_tiles)
    has_next = jnp.logical_and(t != n_ctx_tiles - 1, t != n_tiles - 1)
    xe_ref[0:HALO_ROWS] = jnp.where(has_prev, prev_ref[0], 0.0)
    xe_ref[HALO_ROWS:HALO_ROWS + tm] = main_ref[0]
    xe_ref[HALO_ROWS + tm:2 * HALO_ROWS + tm] = jnp.where(has_next, next_ref[0], 0.0)
    first = HALO_ROWS - SSD_CONV // 2
    rows, lanes = 32, 512
    for r0 in range(0, tm, rows):
        for c0 in range(0, ch, lanes):
            acc = b_ref[:, c0:c0 + lanes] + w_ref[0:1, c0:c0 + lanes] * xe_ref[first + r0:first + r0 + rows, c0:c0 + lanes]
            for k in range(1, SSD_CONV):
                acc = acc + w_ref[k:k + 1, c0:c0 + lanes] * xe_ref[first + k + r0:first + k + r0 + rows, c0:c0 + lanes]
            o_ref[0, r0:r0 + rows, c0:c0 + lanes] = _silu(acc)


def _conv(xbc, conv_w, conv_b, n_ctx_tiles):
    b, t, ch = xbc.shape
    tm = TOKEN_TILE
    n_tiles = t // tm
    per = tm // HALO_ROWS
    last = t // HALO_ROWS - 1
    w = jnp.zeros((8, ch), F32).at[:SSD_CONV].set(conv_w.T)
    return pl.pallas_call(
        functools.partial(_conv_kernel, n_ctx_tiles=n_ctx_tiles, n_tiles=n_tiles),
        grid=(b, n_tiles),
        in_specs=[
            pl.BlockSpec((1, HALO_ROWS, ch), lambda i, j: (i, jnp.maximum(j * per - 1, 0), 0)),
            pl.BlockSpec((1, tm, ch), lambda i, j: (i, j, 0)),
            pl.BlockSpec((1, HALO_ROWS, ch), lambda i, j: (i, jnp.minimum((j + 1) * per, last), 0)),
            pl.BlockSpec((8, ch), lambda i, j: (0, 0)),
            pl.BlockSpec((1, ch), lambda i, j: (0, 0)),
        ],
        out_specs=pl.BlockSpec((1, tm, ch), lambda i, j: (i, j, 0)),
        out_shape=jax.ShapeDtypeStruct((b, t, ch), F32),
        scratch_shapes=[pltpu.VMEM((tm + 2 * HALO_ROWS, ch), F32)],
        compiler_params=_params(("parallel", "parallel")),
        name="conv",
    )(xbc, xbc, xbc, w, conv_b.reshape(1, ch))


def _ssd_kernel(xf_ref, xb_ref, dtf_ref, dtb_ref, bias_ref, a_ref, e_ref, yf_ref, yb_ref, st_ref):
    L = SSD_CHUNK
    width = SSD_HEADS * HEAD_DIM
    gw = width // SSD_GROUPS
    hpg = SSD_HEADS // SSD_GROUPS

    @pl.when(pl.program_id(1) == 0)
    def _():
        st_ref[...] = jnp.zeros_like(st_ref)

    row = lax.broadcasted_iota(jnp.int32, (L, L), 0)
    col = lax.broadcasted_iota(jnp.int32, (L, L), 1)
    lane_head = lax.broadcasted_iota(jnp.int32, (L, gw), 1) // HEAD_DIM
    sub16 = lax.broadcasted_iota(jnp.int32, (16, 128), 0)

    for d, (x_ref, dt_ref, y_ref) in enumerate(((xf_ref, dtf_ref, yf_ref), (xb_ref, dtb_ref, yb_ref))):
        mask = (row >= col) if d == 0 else (row <= col)
        dtv = _softplus(dt_ref[0] + bias_ref[...])
        loga = dtv * a_ref[...]
        cs = jnp.dot(mask.astype(F32), loga, preferred_element_type=F32, precision=HIGHEST)
        cs_t = cs.T
        total = jnp.sum(loga, axis=0, keepdims=True)
        t1 = total.astype(BF16).astype(F32)
        t2 = (total - t1).astype(BF16).astype(F32)
        t3 = total - t1 - t2
        tot = jnp.where(sub16 == 0, t1, jnp.where(sub16 == 1, t2, jnp.where(sub16 == 2, t3, 0.0)))
        stack = jnp.concatenate(
            [dtv.astype(BF16), jnp.exp(cs).astype(BF16), jnp.exp(total - cs).astype(BF16), tot.astype(BF16)], axis=0)
        ex = jnp.dot(stack, e_ref[d], preferred_element_type=F32)
        dt_x = ex[0:L]
        ecs_x = ex[L:2 * L]
        wend_x = ex[2 * L:3 * L]
        edec_x = jnp.exp(ex[3 * L:3 * L + 1] + ex[3 * L + 1:3 * L + 2] + ex[3 * L + 2:3 * L + 3])

        for g in range(SSD_GROUPS):
            sl = slice(g * gw, (g + 1) * gw)
            xs_g = x_ref[0, :, sl]
            b_g = x_ref[0, :, width + g * SSD_STATE:width + (g + 1) * SSD_STATE]
            c_g = x_ref[0, :, width + (SSD_GROUPS + g) * SSD_STATE:width + (SSD_GROUPS + g + 1) * SSD_STATE]
            cb = c_g.astype(BF16)
            xd = xs_g * dt_x[:, sl]
            gram = lax.dot_general(cb, b_g.astype(BF16), (((1,), (1,)), ((), ())), preferred_element_type=F32)
            scores = []
            for hh in range(hpg):
                li = SSD_HEADS * d + hpg * g + hh
                seg = cs[:, li:li + 1] - cs_t[li:li + 1, :]
                scores.append((gram * jnp.where(mask, jnp.exp(seg), 0.0)).astype(BF16))
            scores = jnp.concatenate(scores, axis=1)
            xd_blocks = jnp.concatenate(
                [jnp.where(lane_head == hh, xd, 0.0).astype(BF16) for hh in range(hpg)], axis=0)
            y = jnp.dot(scores, xd_blocks, preferred_element_type=F32)
            st = st_ref[d, :, sl]
            y = y + jnp.dot(cb, st.astype(BF16), preferred_element_type=F32) * ecs_x[:, sl]
            y_ref[0, :, sl] = y
            xdw = (xd * wend_x[:, sl]).astype(BF16)
            st_ref[d, :, sl] = st * edec_x[:, sl] + jnp.dot(b_g.T.astype(BF16), xdw, preferred_element_type=F32)


def _ssd(xbc, dt, dt_bias, a_log, n_ctx_chunks):
    b, t, ch = xbc.shape
    L = SSD_CHUNK
    n = t // L
    width = SSD_HEADS * HEAD_DIM
    ncc = n_ctx_chunks

    def fwd(i, j):
        return (i, j, 0)

    def bwd(i, j):
        return (i, jnp.where(j < ncc, ncc - 1 - j, n + ncc - 1 - j), 0)

    lanes = jnp.zeros((128,), F32)
    bias = lanes.at[:2 * SSD_HEADS].set(dt_bias.reshape(-1)).reshape(1, 128)
    a_neg = lanes.at[:2 * SSD_HEADS].set(-jnp.exp(a_log.reshape(-1))).reshape(1, 128)
    head_of_lane = jnp.arange(width) // HEAD_DIM
    expand = jnp.stack([
        (jnp.arange(128)[:, None] == SSD_HEADS * d + head_of_lane[None, :]) for d in range(2)
    ]).astype(BF16)
    return pl.pallas_call(
        _ssd_kernel,
        grid=(b, n),
        in_specs=[
            pl.BlockSpec((1, L, ch), fwd),
            pl.BlockSpec((1, L, ch), bwd),
            pl.BlockSpec((1, L, 128), fwd),
            pl.BlockSpec((1, L, 128), bwd),
            pl.BlockSpec((1, 128), lambda i, j: (0, 0)),
            pl.BlockSpec((1, 128), lambda i, j: (0, 0)),
            pl.BlockSpec((2, 128, width), lambda i, j: (0, 0, 0)),
        ],
        out_specs=[pl.BlockSpec((1, L, width), fwd), pl.BlockSpec((1, L, width), bwd)],
        out_shape=[jax.ShapeDtypeStruct((b, t, width), F32)] * 2,
        scratch_shapes=[pltpu.VMEM((2, SSD_STATE, width), F32)],
        compiler_params=_params(("parallel", "arbitrary")),
        name="ssd",
    )(xbc, xbc, dt, dt, bias, a_neg, expand)


def _s5_operators(lam_re, lam_im, log_step, b_re, b_im, c_re, c_im):
    nb = S5_BLOCK
    ng, ns = lam_re.shape[1:]
    hp = functools.partial(jnp.einsum, precision=HIGHEST)
    step = jnp.exp(log_step)[..., None]
    k = jnp.arange(nb + 1, dtype=F32)[:, None, None, None]
    mag = jnp.exp(k * (lam_re * step))
    ak_re = mag * jnp.cos(k * (lam_im * step))
    ak_im = mag * jnp.sin(k * (lam_im * step))
    ab_re, ab_im = ak_re[1], ak_im[1]
    den = lam_re * lam_re + lam_im * lam_im
    f_re = ((ab_re - 1.0) * lam_re + ab_im * lam_im) / den
    f_im = (ab_im * lam_re - (ab_re - 1.0) * lam_im) / den
    bb_re = f_re[..., None] * b_re - f_im[..., None] * b_im
    bb_im = f_re[..., None] * b_im + f_im[..., None] * b_re
    w_re = ak_re[..., None] * bb_re - ak_im[..., None] * bb_im
    w_im = ak_re[..., None] * bb_im + ak_im[..., None] * bb_re
    kern = hp('dgip,kdgpj->kdgij', c_re, w_re[:nb]) - hp('dgip,kdgpj->kdgij', c_im, w_im[:nb])
    kf, kb = kern[:, 0], kern[:, 1]
    zall = jnp.concatenate([kb[:0:-1], (kf[0] + kb[0])[None], kf[1:]], axis=0)
    idx = jnp.arange(nb)[None, :] - jnp.arange(nb)[:, None] + nb - 1
    m = zall[idx]
    m = m.transpose(2, 0, 4, 1, 3)
    inj = lambda w, d, rev: (w[nb - 1::-1, d] if rev else w[:nb, d]).transpose(1, 0, 3, 2)
    s_all = jnp.stack([inj(w_re, 0, True), inj(w_re, 1, False), inj(w_im, 0, True), inj(w_im, 1, False)], axis=3)
    def readout(d, ks):
        ar, ai = ak_re[ks, d], ak_im[ks, d]
        cr, ci = c_re[d], c_im[d]
        o_re = cr[None] * ar[:, :, None, :] - ci[None] * ai[:, :, None, :]
        o_im = -(cr[None] * ai[:, :, None, :] + ci[None] * ar[:, :, None, :])
        fix = lambda o: o.transpose(1, 3, 0, 2)
        return fix(o_re), fix(o_im)
    of_re, of_im = readout(0, jnp.arange(1, nb + 1))
    ob_re, ob_im = readout(1, jnp.arange(nb, 0, -1))
    o_all = jnp.stack([of_re, ob_re, of_im, ob_im], axis=1)
    eye = jnp.eye(2, dtype=F32)
    npair = ng // 2
    pw = 2 * nb * S5_GROUP
    smat = jnp.einsum('qasjcp,ab->qsajcbp', s_all.reshape(npair, 2, nb, S5_GROUP, 4, ns), eye)
    smat = smat.reshape(npair, pw, 8 * ns)
    m_pair = jnp.einsum('qasjli,ab->qsajlbi', m.reshape(npair, 2, nb, S5_GROUP, nb, S5_GROUP), eye)
    m_pair = m_pair.reshape(npair, pw, pw)
    o_pair = jnp.einsum('qacpli,ab->qcaplbi', o_all.reshape(npair, 2, 4, ns, nb, S5_GROUP), eye)
    o_pair = o_pair.reshape(npair, 8 * ns, pw)
    yw = jnp.concatenate([m_pair, o_pair], axis=1)
    dec = jnp.stack([ak_re[nb, 0], ak_im[nb, 0], ak_re[nb, 1], ak_im[nb, 1]]).reshape(4, ng * ns)
    return smat.astype(BF16), yw.astype(BF16), dec


def _s5_kernel(u_ref, smat_ref, yw_ref, dec_ref, y_ref, uf, ys, sfr, sbr, sfi, sbi, hfr, hbr, hfi, hbi, *,
               n_ctx_blocks):
    nb = S5_BLOCK
    pairs = smat_ref.shape[0]
    n = u_ref.shape[1] // nb
    ncb = n_ctx_blocks
    lw = 2 * S5_STATE
    seg = 2 * S5_GROUP
    per_col = 128 // seg
    lane_seg = lax.broadcasted_iota(jnp.int32, (n, 128), 1) // seg

    for p in range(pairs):
        for v in range(nb // per_col):
            col = None
            for k in range(per_col):
                x = u_ref[0, pl.ds(v * per_col + k, n, stride=nb), :]
                shift = (seg * (k - p)) % 128
                if shift:
                    x = pltpu.roll(x, shift, 1)
                col = x if col is None else jnp.where(lane_seg == k, x, col)
            uf[p, :, 128 * v:128 * (v + 1)] = col.astype(BF16)
        s = jnp.dot(uf[p], smat_ref[p], preferred_element_type=F32)
        for c, ref in enumerate((sfr, sbr, sfi, sbi)):
            ref[:, p * lw:(p + 1) * lw] = s[:, c * lw:(c + 1) * lw]
    arf, aif, arb, aib = dec_ref[0, 0:1], dec_ref[0, 1:2], dec_ref[0, 2:3], dec_ref[0, 3:4]

    def step(i, carry):
        fr, fi, br, bi = carry
        rf = i
        rb = jnp.where(i < ncb, ncb - 1 - i, n + ncb - 1 - i)
        hfr[pl.ds(rf, 1), :] = fr
        hfi[pl.ds(rf, 1), :] = fi
        hbr[pl.ds(rb, 1), :] = br
        hbi[pl.ds(rb, 1), :] = bi
        nfr = arf * fr - aif * fi + sfr[pl.ds(rf, 1), :]
        nfi = arf * fi + aif * fr + sfi[pl.ds(rf, 1), :]
        nbr = arb * br - aib * bi + sbr[pl.ds(rb, 1), :]
        nbi = arb * bi + aib * br + sbi[pl.ds(rb, 1), :]
        return nfr, nfi, nbr, nbi

    zero = jnp.zeros((1, pairs * lw), F32)
    lax.fori_loop(0, n, step, (zero, zero, zero, zero))
    for p in range(pairs):
        sl = slice(p * lw, (p + 1) * lw)
        lhs = jnp.concatenate(
            [uf[p], hfr[:, sl].astype(BF16), hbr[:, sl].astype(BF16), hfi[:, sl].astype(BF16),
             hbi[:, sl].astype(BF16)], axis=1)
        ys[p] = jnp.dot(lhs, yw_ref[p], preferred_element_type=F32)
    for l in range(nb):
        v, k = divmod(l, per_col)
        out = None
        for p in range(pairs):
            x = ys[p, :, 128 * v:128 * (v + 1)]
            shift = (seg * (p - k)) % 128
            if shift:
                x = pltpu.roll(x, shift, 1)
            out = x if out is None else jnp.where(lane_seg == p, x, out)
        y_ref[0, pl.ds(l, n, stride=nb), :] = out


def _s5(u, ops, n_ctx_blocks):
    smat, yw, dec = ops
    b, t, w = u.shape
    nb = S5_BLOCK
    n = t // nb
    npair = w // (2 * S5_GROUP)
    pw = 2 * nb * S5_GROUP
    ppb = 128 // (2 * S5_GROUP)
    lw = 2 * S5_STATE
    dec = dec.reshape(4, npair // ppb, ppb * lw).transpose(1, 0, 2)
    dec = jnp.concatenate([dec, jnp.zeros_like(dec)], axis=1)
    return pl.pallas_call(
        functools.partial(_s5_kernel, n_ctx_blocks=n_ctx_blocks),
        grid=(b, npair // ppb),
        in_specs=[
            pl.BlockSpec((1, t, 128), lambda i, j: (i, 0, j)),
            pl.BlockSpec((ppb,) + smat.shape[1:], lambda i, j: (j, 0, 0)),
            pl.BlockSpec((ppb,) + yw.shape[1:], lambda i, j: (j, 0, 0)),
            pl.BlockSpec((1, 8, ppb * lw), lambda i, j: (j, 0, 0)),
        ],
        out_specs=pl.BlockSpec((1, t, 128), lambda i, j: (i, 0, j)),
        out_shape=jax.ShapeDtypeStruct((b, t, w), F32),
        scratch_shapes=[pltpu.VMEM((ppb, n, pw), BF16), pltpu.VMEM((ppb, n, pw), F32)]
        + [pltpu.VMEM((n, ppb * lw), F32)] * 8,
        compiler_params=_params(("parallel", "parallel")),
        name="s5",
    )(u, smat, yw, dec)


def _even_out_kernel(yf_ref, yb_ref, xs_ref, z_ref, y5_ref, u_ref, g_ref, h_ref, mod_ref, vs_ref, v5_ref,
                     glu_ref, w_ref, o_ref):
    ws = z_ref.shape[-1]
    ys = yf_ref[0] + yb_ref[0] + vs_ref[1:2] * xs_ref[0]
    s = _rms(ys * _silu(z_ref[0])) * vs_ref[0:1]
    y = _gelu_tanh(y5_ref[0] + v5_ref[0:1] * u_ref[0])
    y = y * _sigmoid(jnp.dot(y.astype(BF16), glu_ref[...], preferred_element_type=F32) + v5_ref[1:2])
    y = y * _silu(g_ref[0])
    o = (jnp.dot(s.astype(BF16), w_ref[0:ws], preferred_element_type=F32)
         + jnp.dot(y.astype(BF16), w_ref[ws:], preferred_element_type=F32))
    o_ref[0] = h_ref[0] + mod_ref[0, 0][2:3] * o


def _even_out(yf, yb, xbc, z, y5, u, g, h, mod, ssd_norm, d_ssd, d_s5, glu_w, glu_b, w_out, n_ctx_tiles):
    b, t, d = h.shape
    tm = TOKEN_TILE
    ws, w5 = z.shape[-1], u.shape[-1]
    vs = jnp.zeros((8, ws), F32).at[0].set(ssd_norm).at[1].set(jnp.repeat(d_ssd, HEAD_DIM))
    v5 = jnp.zeros((8, w5), F32).at[0].set(d_s5).at[1].set(glu_b)
    tok = lambda n: pl.BlockSpec((1, tm, n), lambda i, j: (i, j, 0))
    const = lambda a: pl.BlockSpec(a.shape, lambda i, j: (0,) * a.ndim)
    glu_w = glu_w.astype(BF16)
    w_out = w_out.astype(BF16)
    return pl.pallas_call(
        _even_out_kernel,
        grid=(b, t // tm),
        in_specs=[
            tok(ws), tok(ws), tok(ws), tok(ws), tok(w5), tok(w5), tok(w5), tok(d),
            pl.BlockSpec((1, 1, 3, d), lambda i, j: (i, (j >= n_ctx_tiles).astype(jnp.int32), 0, 0)),
            const(vs), const(v5), const(glu_w), const(w_out),
        ],
        out_specs=tok(d),
        out_shape=jax.ShapeDtypeStruct((b, t, d), F32),
        compiler_params=_params(("parallel", "parallel")),
        name="even_out",
    )(yf, yb, xbc, z, y5, u, g, h, mod, vs, v5, glu_w, w_out)


def _odd_in_kernel(h_ref, mod_ref, w_ref, qg_ref, kg_ref, cos_ref, sin_ref, ones_ref, q_ref, k_ref, v_ref, g_ref, *,
                   q_w, kv_w):
    tm = TOKEN_TILE
    a = _norm_mod(h_ref[0], mod_ref[0, 0]).astype(BF16)
    cosv = cos_ref[...]
    sinv = sin_ref[...]
    first_half = (lax.broadcasted_iota(jnp.int32, (tm, 128), 1) % (HEAD_DIM // 2)) < (HEAD_DIM // 4)

    def head_norm_rope(lo, width, gain_ref, scale, out_ref):
        for c in range(width // 256):
            x = jnp.dot(a, w_ref[:, lo + 256 * c:lo + 256 * (c + 1)], preferred_element_type=F32)
            hi2, lo2 = _split_bf16(x * x)
            ms = (jnp.dot(hi2, ones_ref[...], preferred_element_type=F32)
                  + jnp.dot(lo2, ones_ref[...], preferred_element_type=F32)) * (1.0 / HEAD_DIM)
            xn = x * lax.rsqrt(ms + NORM_EPS) * gain_ref[:, 256 * c:256 * (c + 1)]
            for s in range(2):
                xb = xn[:, 128 * s:128 * (s + 1)]
                partner = jnp.where(first_half, pltpu.roll(xb, 128 - HEAD_DIM // 4, 1), pltpu.roll(xb, HEAD_DIM // 4, 1))
                r = (xb * cosv + partner * sinv) * scale
                head = 4 * c + 2 * s
                out_ref[0, head] = r[:, :HEAD_DIM].astype(out_ref.dtype)
                out_ref[0, head + 1] = pltpu.roll(r, HEAD_DIM, 1)[:, :HEAD_DIM].astype(out_ref.dtype)

    head_norm_rope(0, q_w, qg_ref, HEAD_DIM ** -0.5 * math.log2(math.e), q_ref)
    head_norm_rope(q_w, kv_w, kg_ref, 1.0, k_ref)
    lane = lax.broadcasted_iota(jnp.int32, (tm, 128), 1)
    for c in range(kv_w // 128):
        x = jnp.dot(a, w_ref[:, q_w + kv_w + 128 * c:q_w + kv_w + 128 * (c + 1)], preferred_element_type=F32)
        for s in range(2):
            xs = pltpu.roll(x, HEAD_DIM, 1) if s else x
            vh = jnp.where(lane < HEAD_DIM, xs, jnp.where(lane == HEAD_DIM, 1.0, 0.0))
            v_ref[0, 2 * c + s] = vh.astype(v_ref.dtype)
    g_ref[0] = jnp.dot(a, w_ref[:, q_w + 2 * kv_w:], preferred_element_type=F32)


def _rope_tables(n_ctx, n_lat):
    pairs = HEAD_DIM // 4
    pos = jnp.arange(n_lat)
    row = (pos // GRID_W).astype(F32)
    colp = (pos % GRID_W).astype(F32)
    inv = ROPE_THETA ** (-jnp.arange(pairs, dtype=F32) / pairs)
    lane = jnp.arange(128) % HEAD_DIM
    axis_is_col = (lane // (HEAD_DIM // 2)) == 1
    ang = jnp.where(axis_is_col[None, :], colp[:, None], row[:, None]) * inv[lane % pairs][None, :]
    sign = jnp.where((lane % (HEAD_DIM // 2)) < pairs, -1.0, 1.0)
    cos = jnp.concatenate([jnp.ones((n_ctx, 128), F32), jnp.cos(ang)], axis=0)
    sin = jnp.concatenate([jnp.zeros((n_ctx, 128), F32), jnp.sin(ang) * sign[None, :]], axis=0)
    return cos, sin


def _odd_in(h, mod, w, q_gain, k_gain, cos, sin, n_ctx_tiles):
    b, t, d = h.shape
    tm = TOKEN_TILE
    q_w = ATTN_Q_HEADS * HEAD_DIM
    kv_w = ATTN_KV_HEADS * HEAD_DIM
    qg = jnp.tile(q_gain, ATTN_Q_HEADS).reshape(1, q_w)
    kg = jnp.tile(k_gain, ATTN_KV_HEADS).reshape(1, kv_w)
    blk = jnp.arange(256) // HEAD_DIM
    ones = (blk[:, None] == blk[None, :]).astype(BF16)
    tok = lambda n: pl.BlockSpec((1, tm, n), lambda i, j: (i, j, 0))
    heads = lambda nh, n: pl.BlockSpec((1, nh, tm, n), lambda i, j: (i, 0, j, 0))
    const = lambda a: pl.BlockSpec(a.shape, lambda i, j: (0,) * a.ndim)
    return pl.pallas_call(
        functools.partial(_odd_in_kernel, q_w=q_w, kv_w=kv_w),
        grid=(b, t // tm),
        in_specs=[
            tok(d),
            pl.BlockSpec((1, 1, 3, d), lambda i, j: (i, (j >= n_ctx_tiles).astype(jnp.int32), 0, 0)),
            const(w), const(qg), const(kg),
            pl.BlockSpec((tm, 128), lambda i, j: (j, 0)),
            pl.BlockSpec((tm, 128), lambda i, j: (j, 0)),
            const(ones),
        ],
        out_specs=[heads(ATTN_Q_HEADS, HEAD_DIM), heads(ATTN_KV_HEADS, HEAD_DIM), heads(ATTN_KV_HEADS, 128), tok(q_w)],
        out_shape=[
            jax.ShapeDtypeStruct((b, ATTN_Q_HEADS, t, HEAD_DIM), BF16),
            jax.ShapeDtypeStruct((b, ATTN_KV_HEADS, t, HEAD_DIM), BF16),
            jax.ShapeDtypeStruct((b, ATTN_KV_HEADS, t, 128), BF16),
            jax.ShapeDtypeStruct((b, t, q_w), F32),
        ],
        compiler_params=_params(("parallel", "parallel")),
        name="odd_in",
    )(h, mod, w, qg, kg, cos, sin, ones)


def _attn_kernel(q_ref, k_ref, v_ref, o_ref, *, tk):
    rep, tq, hd = q_ref.shape[1:]
    nk = k_ref.shape[2] // tk
    q = q_ref[0].reshape(rep * tq, hd)

    def body(j, carry):
        m, acc = carry
        off = pl.multiple_of(j * tk, tk)
        k = k_ref[0, 0, pl.ds(off, tk), :]
        v = v_ref[0, 0, pl.ds(off, tk), :]
        s = lax.dot_general(q, k, (((1,), (1,)), ((), ())), preferred_element_type=F32)
        m_new = jnp.maximum(m, jnp.max(s, axis=-1, keepdims=True))
        p = jnp.exp2(s - m_new)
        acc = jnp.exp2(m - m_new) * acc + jnp.dot(p.astype(BF16), v, preferred_element_type=F32)
        return m_new, acc

    init = (jnp.full((rep * tq, 1), -1e30, F32), jnp.zeros((rep * tq, 128), F32))
    _, acc = lax.fori_loop(0, nk, body, init)
    o = acc * (1.0 / acc[:, hd:hd + 1])
    lane = lax.broadcasted_iota(jnp.int32, (tq, 128), 1)
    for c in range(rep // 2):
        o_ref[0, :, 128 * c:128 * (c + 1)] = jnp.where(
            lane < hd, o[2 * c * tq:(2 * c + 1) * tq], pltpu.roll(o[(2 * c + 1) * tq:(2 * c + 2) * tq], hd, 1)
        ).astype(o_ref.dtype)


def _attention(q, k, v, n_ctx):
    b, hq, t, hd = q.shape
    hkv = k.shape[1]
    rep = hq // hkv
    tq = 256
    tk = next(c for c in (768, 512, 256, 128) if t % c == 0)
    assert n_ctx % tq == 0
    return pl.pallas_call(
        functools.partial(_attn_kernel, tk=tk),
        grid=(b, hkv, (t - n_ctx) // tq),
        in_specs=[
            pl.BlockSpec((1, rep, tq, hd), lambda i, j, n: (i, j, n + n_ctx // tq, 0)),
            pl.BlockSpec((1, 1, t, hd), lambda i, j, n: (i, j, 0, 0)),
            pl.BlockSpec((1, 1, t, 128), lambda i, j, n: (i, j, 0, 0)),
        ],
        out_specs=pl.BlockSpec((1, tq, rep * hd), lambda i, j, n: (i, n, j)),
        out_shape=jax.ShapeDtypeStruct((b, t - n_ctx, hq * hd), BF16),
        compiler_params=_params(("parallel", "parallel", "parallel")),
        name="attention",
    )(q, k, v)


def _attn_out_kernel(o_ref, g_ref, h_ref, mod_ref, w_ref, fg_ref, out_ref):
    x = o_ref[0].astype(F32) * _silu(g_ref[0])
    y = jnp.dot(x.astype(BF16), w_ref[...], preferred_element_type=F32)
    hn = h_ref[0] + mod_ref[0, 0][2:3] * y
    out_ref[0] = _rms(hn) * fg_ref[...]


def _attn_out(o, g, h, mod, w_out, final_gain, n_ctx_tiles):
    b, n_lat, d = o.shape
    tm = TOKEN_TILE
    lat = lambda n: pl.BlockSpec((1, tm, n), lambda i, j: (i, j + n_ctx_tiles, 0))
    w_out = w_out.astype(BF16)
    return pl.pallas_call(
        _attn_out_kernel,
        grid=(b, n_lat // tm),
        in_specs=[
            pl.BlockSpec((1, tm, d), lambda i, j: (i, j, 0)),
            lat(d), lat(d),
            pl.BlockSpec((1, 1, 3, d), lambda i, j: (i, 1, 0, 0)),
            pl.BlockSpec(w_out.shape, lambda i, j: (0, 0)),
            pl.BlockSpec((1, d), lambda i, j: (0, 0)),
        ],
        out_specs=pl.BlockSpec((1, tm, d), lambda i, j: (i, j, 0)),
        out_shape=jax.ShapeDtypeStruct((b, n_lat, d), F32),
        compiler_params=_params(("parallel", "parallel")),
        name="attn_out",
    )(o, g, h, mod, w_out, final_gain.reshape(1, d))


def kernel(x, c, ctx, c_ctx, ada_w, ada_b, ev_w_in, ev_conv_w, ev_conv_b, ev_dt_bias, ev_a_log, ev_d_ssd, ev_ssd_norm, ev_lam_re, ev_lam_im, ev_log_step, ev_b_re, ev_b_im, ev_c_re, ev_c_im, ev_d_s5, ev_glu_w, ev_glu_b, ev_w_out, od_w_in, od_q_gain, od_k_gain, od_w_out, final_gain):
    b, n_lat, d = x.shape
    n_ctx = ctx.shape[1]
    assert ada_w.shape[0] == 2 and n_ctx % TOKEN_TILE == 0 and n_lat % TOKEN_TILE == 0
    n_ctx_tiles = n_ctx // TOKEN_TILE
    mods = _adaln(c, c_ctx, ada_w, ada_b)
    h = jnp.concatenate([ctx, x], axis=1)

    ws = SSD_HEADS * HEAD_DIM
    wx = ws + 2 * SSD_GROUPS * SSD_STATE
    w5 = ev_d_s5.shape[-1]
    w = ev_w_in[0]
    cuts = (ws, ws + wx, ws + wx + 2 * SSD_HEADS, ws + wx + 2 * SSD_HEADS + w5)
    w_dt = jnp.zeros((d, 128), F32).at[:, :2 * SSD_HEADS].set(w[:, cuts[1]:cuts[2]])
    w_dt_hi, w_dt_lo = _split_bf16(w_dt)
    w_cat = jnp.concatenate(
        [w[:, :cuts[1]].astype(BF16), w[:, cuts[2]:].astype(BF16), w_dt_hi, w_dt_lo], axis=1)
    z, xbc, u, g, dt = _even_in(h, mods[0], w_cat, n_ctx_tiles, (ws, wx, w5, w5))
    xbc = _conv(xbc, ev_conv_w[0], ev_conv_b[0], n_ctx_tiles)
    yf, yb = _ssd(xbc, dt, ev_dt_bias[0], ev_a_log[0], n_ctx // SSD_CHUNK)
    ops = _s5_operators(ev_lam_re[0], ev_lam_im[0], ev_log_step[0], ev_b_re[0], ev_b_im[0], ev_c_re[0], ev_c_im[0])
    y5 = _s5(u, ops, n_ctx // S5_BLOCK)
    h = _even_out(yf, yb, xbc, z, y5, u, g, h, mods[0], ev_ssd_norm[0], ev_d_ssd[0], ev_d_s5[0], ev_glu_w[0],
                  ev_glu_b[0], ev_w_out[0], n_ctx_tiles)

    cos, sin = _rope_tables(n_ctx, n_lat)
    q, k, v, g = _odd_in(h, mods[1], od_w_in[0].astype(BF16), od_q_gain[0], od_k_gain[0], cos, sin, n_ctx_tiles)
    o = _attention(q, k, v, n_ctx)
    return _attn_out(o, g, h, mods[1], od_w_out[0], final_gain, n_ctx_tiles)
```

```python
import functools
import math

import jax
import jax.numpy as jnp
from jax import lax
from jax.experimental import pallas as pl
from jax.experimental.pallas import tpu as pltpu

F32 = jnp.float32
BF16 = jnp.bfloat16
HIGHEST = lax.Precision.HIGHEST

NORM_EPS = 1e-6
GRID_W = 64
ROPE_THETA = 10000.0

HEAD_DIM = 64
SSD_HEADS = 16
SSD_GROUPS = 4
SSD_STATE = 128
SSD_CHUNK = 128
SSD_CONV = 5
S5_GROUP = 16
S5_STATE = 64
S5_BLOCK = 16
ATTN_Q_HEADS = 16
ATTN_KV_HEADS = 4

TOKEN_TILE = 256
HALO_ROWS = 8
VMEM_LIMIT = 56 << 20


def _params(semantics):
    return pltpu.CompilerParams(dimension_semantics=semantics, vmem_limit_bytes=VMEM_LIMIT)


def _sigmoid(x):
    return 1.0 / (1.0 + jnp.exp(-x))


def _silu(x):
    return x * _sigmoid(x)


def _softplus(x):
    return jnp.maximum(x, 0.0) + jnp.log(1.0 + jnp.exp(-jnp.abs(x)))


def _gelu_tanh(x):
    return 0.5 * x * (1.0 + jnp.tanh(math.sqrt(2.0 / math.pi) * (x + 0.044715 * (x * x * x))))


def _rms(x):
    return x * lax.rsqrt(jnp.mean(x * x, axis=-1, keepdims=True) + NORM_EPS)


def _norm_mod(h, mod):
    return _rms(h) * (1.0 + mod[1:2]) + mod[0:1]


def _split_bf16(x):
    hi = x.astype(BF16)
    return hi, (x - hi.astype(F32)).astype(BF16)


def _adaln_kernel(s_ref, w_ref, b_ref, o_ref):
    s = _silu(s_ref[...])
    o_ref[0] = jnp.dot(s, w_ref[0], preferred_element_type=F32, precision=HIGHEST) + b_ref[0]


def _adaln(c, c_ctx, ada_w, ada_b):
    depth, d, d3 = ada_w.shape
    b = c.shape[0]
    assert b < 8
    rows = jnp.zeros((8, d), F32).at[:b].set(c).at[b].set(c_ctx)
    out = pl.pallas_call(
        _adaln_kernel,
        grid=(depth, d3 // d),
        in_specs=[
            pl.BlockSpec((8, d), lambda i, j: (0, 0)),
            pl.BlockSpec((1, d, d), lambda i, j: (i, 0, j)),
            pl.BlockSpec((1, 1, d), lambda i, j: (i, 0, j)),
        ],
        out_specs=pl.BlockSpec((1, 8, d), lambda i, j: (i, 0, j)),
        out_shape=jax.ShapeDtypeStruct((depth, 8, d3), F32),
        compiler_params=_params(("arbitrary", "arbitrary")),
        name="adaln",
    )(rows, ada_w, ada_b.reshape(depth, 1, d3))
    m = out.reshape(depth, 8, 3, d)
    lat = m[:, :b]
    ctx = jnp.broadcast_to(m[:, b:b + 1], lat.shape)
    return jnp.stack([ctx, lat], axis=2)


def _even_in_kernel(h_ref, mod_ref, w_ref, z_ref, xbc_ref, u_ref, g_ref, dt_ref, *, cuts):
    a = _norm_mod(h_ref[0], mod_ref[0, 0])
    a_hi, a_lo = _split_bf16(a)

    def mm(x, lo, hi):
        return jnp.dot(x, w_ref[:, lo:hi], preferred_element_type=F32)

    c0, c1, c2, c3, c4, c5 = cuts
    z_ref[0] = mm(a_hi, 0, c0)
    xbc_ref[0] = mm(a_hi, c0, c1)
    u_ref[0] = mm(a_hi, c1, c2).astype(u_ref.dtype)
    g_ref[0] = mm(a_hi, c2, c3)
    dt_ref[0] = mm(a_hi, c3, c4) + mm(a_lo, c3, c4) + mm(a_hi, c4, c5)


def _even_in(h, mod, w, n_ctx_tiles, widths):
    b, t, d = h.shape
    tm = TOKEN_TILE
    wz, wx, wu, wg = widths
    cuts = (wz, wz + wx, wz + wx + wu, wz + wx + wu + wg, wz + wx + wu + wg + 128, wz + wx + wu + wg + 256)
    assert w.shape == (d, cuts[-1])
    tok = lambda n: pl.BlockSpec((1, tm, n), lambda i, j: (i, j, 0))
    return pl.pallas_call(
        functools.partial(_even_in_kernel, cuts=cuts),
        grid=(b, t // tm),
        in_specs=[
            tok(d),
            pl.BlockSpec((1, 1, 3, d), lambda i, j: (i, (j >= n_ctx_tiles).astype(jnp.int32), 0, 0)),
            pl.BlockSpec(w.shape, lambda i, j: (0, 0)),
        ],
        out_specs=[tok(wz), tok(wx), tok(wu), tok(wg), tok(128)],
        out_shape=[
            jax.ShapeDtypeStruct((b, t, wz), F32),
            jax.ShapeDtypeStruct((b, t, wx), F32),
            jax.ShapeDtypeStruct((b, t, wu), F32),
            jax.ShapeDtypeStruct((b, t, wg), F32),
            jax.ShapeDtypeStruct((b, t, 128), F32),
        ],
        compiler_params=_params(("parallel", "parallel")),
        name="even_in",
    )(h, mod, w)


def _conv_kernel(prev_ref, main_ref, next_ref, w_ref, b_ref, o_ref, xe_ref, *, n_ctx_tiles, n_tiles):
    tm = TOKEN_TILE
    ch = o_ref.shape[-1]
    t = pl.program_id(1)
    has_prev = jnp.logical_and(t != 0, t != n_ctx_tiles)
    has_next = jnp.logical_and(t != n_ctx_tiles - 1, t != n_tiles - 1)
    xe_ref[0:HALO_ROWS] = jnp.where(has_prev, prev_ref[0], 0.0)
    xe_ref[HALO_ROWS:HALO_ROWS + tm] = main_ref[0]
    xe_ref[HALO_ROWS + tm:2 * HALO_ROWS + tm] = jnp.where(has_next, next_ref[0], 0.0)
    first = HALO_ROWS - SSD_CONV // 2
    rows, lanes = 32, 512
    for r0 in range(0, tm, rows):
        for c0 in range(0, ch, lanes):
            acc = b_ref[:, c0:c0 + lanes] + w_ref[0:1, c0:c0 + lanes] * xe_ref[first + r0:first + r0 + rows, c0:c0 + lanes]
            for k in range(1, SSD_CONV):
                acc = acc + w_ref[k:k + 1, c0:c0 + lanes] * xe_ref[first + k + r0:first + k + r0 + rows, c0:c0 + lanes]
            o_ref[0, r0:r0 + rows, c0:c0 + lanes] = _silu(acc)


def _conv(xbc, conv_w, conv_b, n_ctx_tiles):
    b, t, ch = xbc.shape
    tm = TOKEN_TILE
    n_tiles = t // tm
    per = tm // HALO_ROWS
    last = t // HALO_ROWS - 1
    w = jnp.zeros((8, ch), F32).at[:SSD_CONV].set(conv_w.T)
    return pl.pallas_call(
        functools.partial(_conv_kernel, n_ctx_tiles=n_ctx_tiles, n_tiles=n_tiles),
        grid=(b, n_tiles),
        in_specs=[
            pl.BlockSpec((1, HALO_ROWS, ch), lambda i, j: (i, jnp.maximum(j * per - 1, 0), 0)),
            pl.BlockSpec((1, tm, ch), lambda i, j: (i, j, 0)),
            pl.BlockSpec((1, HALO_ROWS, ch), lambda i, j: (i, jnp.minimum((j + 1) * per, last), 0)),
            pl.BlockSpec((8, ch), lambda i, j: (0, 0)),
            pl.BlockSpec((1, ch), lambda i, j: (0, 0)),
        ],
        out_specs=pl.BlockSpec((1, tm, ch), lambda i, j: (i, j, 0)),
        out_shape=jax.ShapeDtypeStruct((b, t, ch), F32),
        scratch_shapes=[pltpu.VMEM((tm + 2 * HALO_ROWS, ch), F32)],
        compiler_params=_params(("parallel", "parallel")),
        name="conv",
    )(xbc, xbc, xbc, w, conv_b.reshape(1, ch))


def _ssd_kernel(xf_ref, xb_ref, dtf_ref, dtb_ref, bias_ref, a_ref, e_ref, yf_ref, yb_ref, st_ref):
    L = SSD_CHUNK
    width = SSD_HEADS * HEAD_DIM
    gw = width // SSD_GROUPS
    hpg = SSD_HEADS // SSD_GROUPS

    @pl.when(pl.program_id(1) == 0)
    def _():
        st_ref[...] = jnp.zeros_like(st_ref)

    row = lax.broadcasted_iota(jnp.int32, (L, L), 0)
    col = lax.broadcasted_iota(jnp.int32, (L, L), 1)
    lane_head = lax.broadcasted_iota(jnp.int32, (L, gw), 1) // HEAD_DIM
    sub16 = lax.broadcasted_iota(jnp.int32, (16, 128), 0)

    for d, (x_ref, dt_ref, y_ref) in enumerate(((xf_ref, dtf_ref, yf_ref), (xb_ref, dtb_ref, yb_ref))):
        mask = (row >= col) if d == 0 else (row <= col)
        dtv = _softplus(dt_ref[0] + bias_ref[...])
        loga = dtv * a_ref[...]
        cs = jnp.dot(mask.astype(F32), loga, preferred_element_type=F32, precision=HIGHEST)
        cs_t = cs.T
        total = jnp.sum(loga, axis=0, keepdims=True)
        t1 = total.astype(BF16).astype(F32)
        t2 = (total - t1).astype(BF16).astype(F32)
        t3 = total - t1 - t2
        tot = jnp.where(sub16 == 0, t1, jnp.where(sub16 == 1, t2, jnp.where(sub16 == 2, t3, 0.0)))
        stack = jnp.concatenate(
            [dtv.astype(BF16), jnp.exp(cs).astype(BF16), jnp.exp(total - cs).astype(BF16), tot.astype(BF16)], axis=0)
        ex = jnp.dot(stack, e_ref[d], preferred_element_type=F32)
        dt_x = ex[0:L]
        ecs_x = ex[L:2 * L]
        wend_x = ex[2 * L:3 * L]
        edec_x = jnp.exp(ex[3 * L:3 * L + 1] + ex[3 * L + 1:3 * L + 2] + ex[3 * L + 2:3 * L + 3])

        for g in range(SSD_GROUPS):
            sl = slice(g * gw, (g + 1) * gw)
            xs_g = x_ref[0, :, sl]
            b_g = x_ref[0, :, width + g * SSD_STATE:width + (g + 1) * SSD_STATE]
            c_g = x_ref[0, :, width + (SSD_GROUPS + g) * SSD_STATE:width + (SSD_GROUPS + g + 1) * SSD_STATE]
            cb = c_g.astype(BF16)
            xd = xs_g * dt_x[:, sl]
            gram = lax.dot_general(cb, b_g.astype(BF16), (((1,), (1,)), ((), ())), preferred_element_type=F32)
            scores = []
            for hh in range(hpg):
                li = SSD_HEADS * d + hpg * g + hh
                seg = cs[:, li:li + 1] - cs_t[li:li + 1, :]
                scores.append((gram * jnp.where(mask, jnp.exp(seg), 0.0)).astype(BF16))
            scores = jnp.concatenate(scores, axis=1)
            xd_blocks = jnp.concatenate(
                [jnp.where(lane_head == hh, xd, 0.0).astype(BF16) for hh in range(hpg)], axis=0)
            y = jnp.dot(scores, xd_blocks, preferred_element_type=F32)
            st = st_ref[d, :, sl]
            y = y + jnp.dot(cb, st.astype(BF16), preferred_element_type=F32) * ecs_x[:, sl]
            y_ref[0, :, sl] = y
            xdw = (xd * wend_x[:, sl]).astype(BF16)
            st_ref[d, :, sl] = st * edec_x[:, sl] + jnp.dot(b_g.T.astype(BF16), xdw, preferred_element_type=F32)


def _ssd(xbc, dt, dt_bias, a_log, n_ctx_chunks):
    b, t, ch = xbc.shape
    L = SSD_CHUNK
    n = t // L
    width = SSD_HEADS * HEAD_DIM
    ncc = n_ctx_chunks

    def fwd(i, j):
        return (i, j, 0)

    def bwd(i, j):
        return (i, jnp.where(j < ncc, ncc - 1 - j, n + ncc - 1 - j), 0)

    lanes = jnp.zeros((128,), F32)
    bias = lanes.at[:2 * SSD_HEADS].set(dt_bias.reshape(-1)).reshape(1, 128)
    a_neg = lanes.at[:2 * SSD_HEADS].set(-jnp.exp(a_log.reshape(-1))).reshape(1, 128)
    head_of_lane = jnp.arange(width) // HEAD_DIM
    expand = jnp.stack([
        (jnp.arange(128)[:, None] == SSD_HEADS * d + head_of_lane[None, :]) for d in range(2)
    ]).astype(BF16)
    return pl.pallas_call(
        _ssd_kernel,
        grid=(b, n),
        in_specs=[
            pl.BlockSpec((1, L, ch), fwd),
            pl.BlockSpec((1, L, ch), bwd),
            pl.BlockSpec((1, L, 128), fwd),
            pl.BlockSpec((1, L, 128), bwd),
            pl.BlockSpec((1, 128), lambda i, j: (0, 0)),
            pl.BlockSpec((1, 128), lambda i, j: (0, 0)),
            pl.BlockSpec((2, 128, width), lambda i, j: (0, 0, 0)),
        ],
        out_specs=[pl.BlockSpec((1, L, width), fwd), pl.BlockSpec((1, L, width), bwd)],
        out_shape=[jax.ShapeDtypeStruct((b, t, width), F32)] * 2,
        scratch_shapes=[pltpu.VMEM((2, SSD_STATE, width), F32)],
        compiler_params=_params(("parallel", "arbitrary")),
        name="ssd",
    )(xbc, xbc, dt, dt, bias, a_neg, expand)


def _s5_operators(lam_re, lam_im, log_step, b_re, b_im, c_re, c_im):
    nb = S5_BLOCK
    ng, ns = lam_re.shape[1:]
    hp = functools.partial(jnp.einsum, precision=HIGHEST)
    step = jnp.exp(log_step)[..., None]
    k = jnp.arange(nb + 1, dtype=F32)[:, None, None, None]
    mag = jnp.exp(k * (lam_re * step))
    ak_re = mag * jnp.cos(k * (lam_im * step))
    ak_im = mag * jnp.sin(k * (lam_im * step))
    ab_re, ab_im = ak_re[1], ak_im[1]
    den = lam_re * lam_re + lam_im * lam_im
    f_re = ((ab_re - 1.0) * lam_re + ab_im * lam_im) / den
    f_im = (ab_im * lam_re - (ab_re - 1.0) * lam_im) / den
    tr = lambda a: jnp.swapaxes(a, -1, -2)
    bb_re = tr(f_re[..., None] * b_re - f_im[..., None] * b_im)
    bb_im = tr(f_re[..., None] * b_im + f_im[..., None] * b_re)
    akr, aki = ak_re[:, :, :, None, :], ak_im[:, :, :, None, :]
    w_re = akr * bb_re - aki * bb_im
    w_im = akr * bb_im + aki * bb_re
    kern = hp('kdgjp,dgip->kdgji', w_re[:nb], c_re) - hp('kdgjp,dgip->kdgji', w_im[:nb], c_im)
    kf, kb = kern[:, 0], kern[:, 1]
    zall = jnp.concatenate([kb[:0:-1], (kf[0] + kb[0])[None], kf[1:]], axis=0).transpose(1, 2, 0, 3)
    m = jnp.stack([zall[:, :, nb - 1 - s:2 * nb - 1 - s].reshape(ng, S5_GROUP, nb * S5_GROUP) for s in range(nb)],
                  axis=1).reshape(ng, nb * S5_GROUP, nb * S5_GROUP)
    inj = lambda w, d, rev: jnp.swapaxes(w[nb - 1::-1, d] if rev else w[:nb, d], 0, 1)
    def readout(d, ks):
        ar, ai = (a[ks, d].transpose(1, 2, 0)[..., None] for a in (ak_re, ak_im))
        cr, ci = (tr(c[d])[:, :, None, :] for c in (c_re, c_im))
        return cr * ar - ci * ai, -(cr * ai + ci * ar)
    of_re, of_im = readout(0, jnp.arange(1, nb + 1))
    ob_re, ob_im = readout(1, jnp.arange(nb, 0, -1))
    odd = (jnp.arange(ng) % 2 == 1)
    pick = lambda a, b: jnp.where(odd.reshape((ng,) + (1,) * (a.ndim - 1)), b, a)
    sf_re, sb_re, sf_im, sb_im = inj(w_re, 0, True), inj(w_re, 1, False), inj(w_im, 0, True), inj(w_im, 1, False)
    smat = jnp.stack([pick(sf_re, sb_re), pick(sb_re, sf_re), pick(sf_im, sb_im), pick(sb_im, sf_im)], axis=3)
    smat = smat.reshape(ng, nb * S5_GROUP, 4 * ns)
    o_all = jnp.stack([pick(of_re, ob_re), pick(ob_re, of_re), pick(of_im, ob_im), pick(ob_im, of_im)], axis=1)
    o_all = o_all.reshape(ng, 4 * ns, nb * S5_GROUP)
    yw = jnp.concatenate([m, o_all], axis=1)
    swap = lambda a: a.reshape(ng // 2, 2, ns)[:, ::-1].reshape(ng * ns)
    dec = jnp.stack([ak_re[nb, 0].reshape(-1), ak_im[nb, 0].reshape(-1), swap(ak_re[nb, 1]), swap(ak_im[nb, 1])])
    return smat.astype(BF16), yw.astype(BF16), dec


def _s5_kernel(u_ref, smat_ref, yw_ref, dec_ref, y_ref, us, uf, ys, sfr, sbr, sfi, sbi, hfr, hbr, hfi, hbi, *,
               n_ctx_blocks):
    nb = S5_BLOCK
    groups = smat_ref.shape[0]
    n = u_ref.shape[1] // nb
    ncb = n_ctx_blocks
    seg = S5_GROUP
    per_col = 128 // seg
    lane_seg = lax.broadcasted_iota(jnp.int32, (n, 128), 1) // seg
    low = lax.broadcasted_iota(jnp.int32, (n, 128), 1) < S5_STATE

    for s in range(nb):
        us[s] = u_ref[0, pl.ds(s, n, stride=nb), :]

    def fold(g):
        for v in range(nb // per_col):
            col = None
            for k in range(per_col):
                x = us[v * per_col + k]
                shift = (seg * (k - g)) % 128
                if shift:
                    x = pltpu.roll(x, shift, 1)
                col = x if col is None else jnp.where(lane_seg == k, x, col)
            uf[g, :, 128 * v:128 * (v + 1)] = col.astype(BF16)
        return jnp.dot(uf[g], smat_ref[g], preferred_element_type=F32)

    for p in range(groups // 2):
        se, so = fold(2 * p), fold(2 * p + 1)
        sl = slice(128 * p, 128 * (p + 1))
        sfr[:, sl] = jnp.where(low, se[:, :128], so[:, :128])
        sbr[:, sl] = jnp.where(low, so[:, :128], se[:, :128])
        sfi[:, sl] = jnp.where(low, se[:, 128:], so[:, 128:])
        sbi[:, sl] = jnp.where(low, so[:, 128:], se[:, 128:])
    arf, aif, arb, aib = dec_ref[0, 0:1], dec_ref[0, 1:2], dec_ref[0, 2:3], dec_ref[0, 3:4]

    def step(i, carry):
        fr, fi, br, bi = carry
        rf = i
        rb = jnp.where(i < ncb, ncb - 1 - i, n + ncb - 1 - i)
        hfr[pl.ds(rf, 1), :] = fr
        hfi[pl.ds(rf, 1), :] = fi
        hbr[pl.ds(rb, 1), :] = br
        hbi[pl.ds(rb, 1), :] = bi
        nfr = arf * fr - aif * fi + sfr[pl.ds(rf, 1), :]
        nfi = arf * fi + aif * fr + sfi[pl.ds(rf, 1), :]
        nbr = arb * br - aib * bi + sbr[pl.ds(rb, 1), :]
        nbi = arb * bi + aib * br + sbi[pl.ds(rb, 1), :]
        return nfr, nfi, nbr, nbi

    zero = jnp.zeros((1, 128 * (groups // 2)), F32)
    lax.fori_loop(0, n, step, (zero, zero, zero, zero))
    for g in range(groups):
        sl = slice(128 * (g // 2), 128 * (g // 2 + 1))
        first, second = ((hfr, hfi), (hbr, hbi)) if g % 2 == 0 else ((hbr, hbi), (hfr, hfi))
        lhs = jnp.concatenate(
            [uf[g]] + [jnp.where(low, a[:, sl], b[:, sl]).astype(BF16) for a, b in zip(first, second)], axis=1)
        ys[g] = jnp.dot(lhs, yw_ref[g], preferred_element_type=F32)
    for l in range(nb):
        v, k = divmod(l, per_col)
        out = None
        for g in range(groups):
            x = ys[g, :, 128 * v:128 * (v + 1)]
            shift = (seg * (g - k)) % 128
            if shift:
                x = pltpu.roll(x, shift, 1)
            out = x if out is None else jnp.where(lane_seg == g, x, out)
        y_ref[0, pl.ds(l, n, stride=nb), :] = out


def _s5(u, ops, n_ctx_blocks):
    smat, yw, dec = ops
    b, t, w = u.shape
    nb = S5_BLOCK
    n = t // nb
    gw = nb * S5_GROUP
    gpb = 128 // S5_GROUP
    steps = w // 128
    lanes = gpb * S5_STATE
    dec = dec.reshape(4, steps, lanes).transpose(1, 0, 2)
    dec = jnp.concatenate([dec, jnp.zeros_like(dec)], axis=1)
    return pl.pallas_call(
        functools.partial(_s5_kernel, n_ctx_blocks=n_ctx_blocks),
        grid=(b, steps),
        in_specs=[
            pl.BlockSpec((1, t, 128), lambda i, j: (i, 0, j)),
            pl.BlockSpec((gpb,) + smat.shape[1:], lambda i, j: (j, 0, 0)),
            pl.BlockSpec((gpb,) + yw.shape[1:], lambda i, j: (j, 0, 0)),
            pl.BlockSpec((1, 8, lanes), lambda i, j: (j, 0, 0)),
        ],
        out_specs=pl.BlockSpec((1, t, 128), lambda i, j: (i, 0, j)),
        out_shape=jax.ShapeDtypeStruct((b, t, w), F32),
        scratch_shapes=[pltpu.VMEM((nb, n, 128), F32), pltpu.VMEM((gpb, n, gw), BF16), pltpu.VMEM((gpb, n, gw), F32)]
        + [pltpu.VMEM((n, lanes), F32)] * 8,
        compiler_params=_params(("parallel", "parallel")),
        name="s5",
    )(u, smat, yw, dec)


def _even_out_kernel(yf_ref, yb_ref, xs_ref, z_ref, y5_ref, u_ref, g_ref, h_ref, mod_ref, vs_ref, v5_ref,
                     glu_ref, w_ref, o_ref):
    ws = z_ref.shape[-1]
    ys = yf_ref[0] + yb_ref[0] + vs_ref[1:2] * xs_ref[0]
    s = _rms(ys * _silu(z_ref[0])) * vs_ref[0:1]
    y = _gelu_tanh(y5_ref[0] + v5_ref[0:1] * u_ref[0])
    y = y * _sigmoid(jnp.dot(y.astype(BF16), glu_ref[...], preferred_element_type=F32) + v5_ref[1:2])
    y = y * _silu(g_ref[0])
    o = (jnp.dot(s.astype(BF16), w_ref[0:ws], preferred_element_type=F32)
         + jnp.dot(y.astype(BF16), w_ref[ws:], preferred_element_type=F32))
    o_ref[0] = h_ref[0] + mod_ref[0, 0][2:3] * o


def _even_out(yf, yb, xbc, z, y5, u, g, h, mod, ssd_norm, d_ssd, d_s5, glu_w, glu_b, w_out, n_ctx_tiles):
    b, t, d = h.shape
    tm = TOKEN_TILE
    ws, w5 = z.shape[-1], u.shape[-1]
    vs = jnp.zeros((8, ws), F32).at[0].set(ssd_norm).at[1].set(jnp.repeat(d_ssd, HEAD_DIM))
    v5 = jnp.zeros((8, w5), F32).at[0].set(d_s5).at[1].set(glu_b)
    tok = lambda n: pl.BlockSpec((1, tm, n), lambda i, j: (i, j, 0))
    const = lambda a: pl.BlockSpec(a.shape, lambda i, j: (0,) * a.ndim)
    glu_w = glu_w.astype(BF16)
    w_out = w_out.astype(BF16)
    return pl.pallas_call(
        _even_out_kernel,
        grid=(b, t // tm),
        in_specs=[
            tok(ws), tok(ws), tok(ws), tok(ws), tok(w5), tok(w5), tok(w5), tok(d),
            pl.BlockSpec((1, 1, 3, d), lambda i, j: (i, (j >= n_ctx_tiles).astype(jnp.int32), 0, 0)),
            const(vs), const(v5), const(glu_w), const(w_out),
        ],
        out_specs=tok(d),
        out_shape=jax.ShapeDtypeStruct((b, t, d), F32),
        compiler_params=_params(("parallel", "parallel")),
        name="even_out",
    )(yf, yb, xbc, z, y5, u, g, h, mod, vs, v5, glu_w, w_out)


def _odd_in_kernel(h_ref, mod_ref, w_ref, qg_ref, kg_ref, cos_ref, sin_ref, ones_ref, q_ref, k_ref, v_ref, g_ref, *,
                   q_w, kv_w):
    tm = TOKEN_TILE
    a = _norm_mod(h_ref[0], mod_ref[0, 0]).astype(BF16)
    cosv = cos_ref[...]
    sinv = sin_ref[...]
    first_half = (lax.broadcasted_iota(jnp.int32, (tm, 128), 1) % (HEAD_DIM // 2)) < (HEAD_DIM // 4)

    def head_norm_rope(lo, width, gain_ref, scale, out_ref):
        for c in range(width // 256):
            x = jnp.dot(a, w_ref[:, lo + 256 * c:lo + 256 * (c + 1)], preferred_element_type=F32)
            hi2, lo2 = _split_bf16(x * x)
            ms = (jnp.dot(hi2, ones_ref[...], preferred_element_type=F32)
                  + jnp.dot(lo2, ones_ref[...], preferred_element_type=F32)) * (1.0 / HEAD_DIM)
            xn = x * lax.rsqrt(ms + NORM_EPS) * gain_ref[:, 256 * c:256 * (c + 1)]
            for s in range(2):
                xb = xn[:, 128 * s:128 * (s + 1)]
                partner = jnp.where(first_half, pltpu.roll(xb, 128 - HEAD_DIM // 4, 1), pltpu.roll(xb, HEAD_DIM // 4, 1))
                r = (xb * cosv + partner * sinv) * scale
                head = 4 * c + 2 * s
                out_ref[0, head] = r[:, :HEAD_DIM].astype(out_ref.dtype)
                out_ref[0, head + 1] = pltpu.roll(r, HEAD_DIM, 1)[:, :HEAD_DIM].astype(out_ref.dtype)

    head_norm_rope(0, q_w, qg_ref, HEAD_DIM ** -0.5 * math.log2(math.e), q_ref)
    head_norm_rope(q_w, kv_w, kg_ref, 1.0, k_ref)
    lane = lax.broadcasted_iota(jnp.int32, (tm, 128), 1)
    for c in range(kv_w // 128):
        x = jnp.dot(a, w_ref[:, q_w + kv_w + 128 * c:q_w + kv_w + 128 * (c + 1)], preferred_element_type=F32)
        for s in range(2):
            xs = pltpu.roll(x, HEAD_DIM, 1) if s else x
            vh = jnp.where(lane < HEAD_DIM, xs, jnp.where(lane == HEAD_DIM, 1.0, 0.0))
            v_ref[0, 2 * c + s] = vh.astype(v_ref.dtype)
    g_ref[0] = jnp.dot(a, w_ref[:, q_w + 2 * kv_w:], preferred_element_type=F32)


def _rope_tables(n_ctx, n_lat):
    pairs = HEAD_DIM // 4
    pos = jnp.arange(n_lat)
    row = (pos // GRID_W).astype(F32)
    colp = (pos % GRID_W).astype(F32)
    inv = ROPE_THETA ** (-jnp.arange(pairs, dtype=F32) / pairs)
    lane = jnp.arange(128) % HEAD_DIM
    axis_is_col = (lane // (HEAD_DIM // 2)) == 1
    ang = jnp.where(axis_is_col[None, :], colp[:, None], row[:, None]) * inv[lane % pairs][None, :]
    sign = jnp.where((lane % (HEAD_DIM // 2)) < pairs, -1.0, 1.0)
    cos = jnp.concatenate([jnp.ones((n_ctx, 128), F32), jnp.cos(ang)], axis=0)
    sin = jnp.concatenate([jnp.zeros((n_ctx, 128), F32), jnp.sin(ang) * sign[None, :]], axis=0)
    return cos, sin


def _odd_in(h, mod, w, q_gain, k_gain, cos, sin, n_ctx_tiles):
    b, t, d = h.shape
    tm = TOKEN_TILE
    q_w = ATTN_Q_HEADS * HEAD_DIM
    kv_w = ATTN_KV_HEADS * HEAD_DIM
    qg = jnp.tile(q_gain, ATTN_Q_HEADS).reshape(1, q_w)
    kg = jnp.tile(k_gain, ATTN_KV_HEADS).reshape(1, kv_w)
    blk = jnp.arange(256) // HEAD_DIM
    ones = (blk[:, None] == blk[None, :]).astype(BF16)
    tok = lambda n: pl.BlockSpec((1, tm, n), lambda i, j: (i, j, 0))
    heads = lambda nh, n: pl.BlockSpec((1, nh, tm, n), lambda i, j: (i, 0, j, 0))
    const = lambda a: pl.BlockSpec(a.shape, lambda i, j: (0,) * a.ndim)
    return pl.pallas_call(
        functools.partial(_odd_in_kernel, q_w=q_w, kv_w=kv_w),
        grid=(b, t // tm),
        in_specs=[
            tok(d),
            pl.BlockSpec((1, 1, 3, d), lambda i, j: (i, (j >= n_ctx_tiles).astype(jnp.int32), 0, 0)),
            const(w), const(qg), const(kg),
            pl.BlockSpec((tm, 128), lambda i, j: (j, 0)),
            pl.BlockSpec((tm, 128), lambda i, j: (j, 0)),
            const(ones),
        ],
        out_specs=[heads(ATTN_Q_HEADS, HEAD_DIM), heads(ATTN_KV_HEADS, HEAD_DIM), heads(ATTN_KV_HEADS, 128), tok(q_w)],
        out_shape=[
            jax.ShapeDtypeStruct((b, ATTN_Q_HEADS, t, HEAD_DIM), BF16),
            jax.ShapeDtypeStruct((b, ATTN_KV_HEADS, t, HEAD_DIM), BF16),
            jax.ShapeDtypeStruct((b, ATTN_KV_HEADS, t, 128), BF16),
            jax.ShapeDtypeStruct((b, t, q_w), F32),
        ],
        compiler_params=_params(("parallel", "parallel")),
        name="odd_in",
    )(h, mod, w, qg, kg, cos, sin, ones)


def _attn_kernel(q_ref, k_ref, v_ref, o_ref, *, tk):
    rep, tq, hd = q_ref.shape[1:]
    nk = k_ref.shape[2] // tk
    q = q_ref[0].reshape(rep * tq, hd)

    def body(j, carry):
        m, acc = carry
        off = pl.multiple_of(j * tk, tk)
        k = k_ref[0, 0, pl.ds(off, tk), :]
        v = v_ref[0, 0, pl.ds(off, tk), :]
        s = lax.dot_general(q, k, (((1,), (1,)), ((), ())), preferred_element_type=F32)
        m_new = jnp.maximum(m, jnp.max(s, axis=-1, keepdims=True))
        p = jnp.exp2(s - m_new)
        acc = jnp.exp2(m - m_new) * acc + jnp.dot(p.astype(BF16), v, preferred_element_type=F32)
        return m_new, acc

    init = (jnp.full((rep * tq, 1), -1e30, F32), jnp.zeros((rep * tq, 128), F32))
    _, acc = lax.fori_loop(0, nk, body, init, unroll=True)
    o = acc * (1.0 / acc[:, hd:hd + 1])
    lane = lax.broadcasted_iota(jnp.int32, (tq, 128), 1)
    for c in range(rep // 2):
        o_ref[0, :, 128 * c:128 * (c + 1)] = jnp.where(
            lane < hd, o[2 * c * tq:(2 * c + 1) * tq], pltpu.roll(o[(2 * c + 1) * tq:(2 * c + 2) * tq], hd, 1)
        ).astype(o_ref.dtype)


def _attention(q, k, v, n_ctx):
    b, hq, t, hd = q.shape
    hkv = k.shape[1]
    rep = hq // hkv
    tq = 256
    tk = next(c for c in (768, 512, 256, 128) if t % c == 0)
    assert n_ctx % tq == 0
    return pl.pallas_call(
        functools.partial(_attn_kernel, tk=tk),
        grid=(b, hkv, (t - n_ctx) // tq),
        in_specs=[
            pl.BlockSpec((1, rep, tq, hd), lambda i, j, n: (i, j, n + n_ctx // tq, 0)),
            pl.BlockSpec((1, 1, t, hd), lambda i, j, n: (i, j, 0, 0)),
            pl.BlockSpec((1, 1, t, 128), lambda i, j, n: (i, j, 0, 0)),
        ],
        out_specs=pl.BlockSpec((1, tq, rep * hd), lambda i, j, n: (i, n, j)),
        out_shape=jax.ShapeDtypeStruct((b, t - n_ctx, hq * hd), BF16),
        compiler_params=_params(("parallel", "parallel", "parallel")),
        name="attention",
    )(q, k, v)


def _attn_out_kernel(o_ref, g_ref, h_ref, mod_ref, w_ref, fg_ref, out_ref):
    x = o_ref[0].astype(F32) * _silu(g_ref[0])
    y = jnp.dot(x.astype(BF16), w_ref[...], preferred_element_type=F32)
    hn = h_ref[0] + mod_ref[0, 0][2:3] * y
    out_ref[0] = _rms(hn) * fg_ref[...]


def _attn_out(o, g, h, mod, w_out, final_gain, n_ctx_tiles):
    b, n_lat, d = o.shape
    tm = TOKEN_TILE
    lat = lambda n: pl.BlockSpec((1, tm, n), lambda i, j: (i, j + n_ctx_tiles, 0))
    w_out = w_out.astype(BF16)
    return pl.pallas_call(
        _attn_out_kernel,
        grid=(b, n_lat // tm),
        in_specs=[
            pl.BlockSpec((1, tm, d), lambda i, j: (i, j, 0)),
            lat(d), lat(d),
            pl.BlockSpec((1, 1, 3, d), lambda i, j: (i, 1, 0, 0)),
            pl.BlockSpec(w_out.shape, lambda i, j: (0, 0)),
            pl.BlockSpec((1, d), lambda i, j: (0, 0)),
        ],
        out_specs=pl.BlockSpec((1, tm, d), lambda i, j: (i, j, 0)),
        out_shape=jax.ShapeDtypeStruct((b, n_lat, d), F32),
        compiler_params=_params(("parallel", "parallel")),
        name="attn_out",
    )(o, g, h, mod, w_out, final_gain.reshape(1, d))


def kernel(x, c, ctx, c_ctx, ada_w, ada_b, ev_w_in, ev_conv_w, ev_conv_b, ev_dt_bias, ev_a_log, ev_d_ssd, ev_ssd_norm, ev_lam_re, ev_lam_im, ev_log_step, ev_b_re, ev_b_im, ev_c_re, ev_c_im, ev_d_s5, ev_glu_w, ev_glu_b, ev_w_out, od_w_in, od_q_gain, od_k_gain, od_w_out, final_gain):
    b, n_lat, d = x.shape
    n_ctx = ctx.shape[1]
    assert ada_w.shape[0] == 2 and n_ctx % TOKEN_TILE == 0 and n_lat % TOKEN_TILE == 0
    n_ctx_tiles = n_ctx // TOKEN_TILE
    mods = _adaln(c, c_ctx, ada_w, ada_b)
    h = jnp.concatenate([ctx, x], axis=1)

    ws = SSD_HEADS * HEAD_DIM
    wx = ws + 2 * SSD_GROUPS * SSD_STATE
    w5 = ev_d_s5.shape[-1]
    w = ev_w_in[0]
    cuts = (ws, ws + wx, ws + wx + 2 * SSD_HEADS, ws + wx + 2 * SSD_HEADS + w5)
    w_dt = jnp.zeros((d, 128), F32).at[:, :2 * SSD_HEADS].set(w[:, cuts[1]:cuts[2]])
    w_dt_hi, w_dt_lo = _split_bf16(w_dt)
    w_cat = jnp.concatenate(
        [w[:, :cuts[1]].astype(BF16), w[:, cuts[2]:].astype(BF16), w_dt_hi, w_dt_lo], axis=1)
    z, xbc, u, g, dt = _even_in(h, mods[0], w_cat, n_ctx_tiles, (ws, wx, w5, w5))
    xbc = _conv(xbc, ev_conv_w[0], ev_conv_b[0], n_ctx_tiles)
    yf, yb = _ssd(xbc, dt, ev_dt_bias[0], ev_a_log[0], n_ctx // SSD_CHUNK)
    ops = _s5_operators(ev_lam_re[0], ev_lam_im[0], ev_log_step[0], ev_b_re[0], ev_b_im[0], ev_c_re[0], ev_c_im[0])
    y5 = _s5(u, ops, n_ctx // S5_BLOCK)
    h = _even_out(yf, yb, xbc, z, y5, u, g, h, mods[0], ev_ssd_norm[0], ev_d_ssd[0], ev_d_s5[0], ev_glu_w[0],
                  ev_glu_b[0], ev_w_out[0], n_ctx_tiles)

    cos, sin = _rope_tables(n_ctx, n_lat)
    q, k, v, g = _odd_in(h, mods[1], od_w_in[0].astype(BF16), od_q_gain[0], od_k_gain[0], cos, sin, n_ctx_tiles)
    o = _attention(q, k, v, n_ctx)
    return _attn_out(o, g, h, mods[1], od_w_out[0], final_gain, n_ctx_tiles)
```

```python
import functools
import math

import jax
import jax.numpy as jnp
from jax import lax
from jax.experimental import pallas as pl
from jax.experimental.pallas import tpu as pltpu

F32 = jnp.float32
BF16 = jnp.bfloat16
HIGHEST = lax.Precision.HIGHEST

NORM_EPS = 1e-6
GRID_W = 64
ROPE_THETA = 10000.0

HEAD_DIM = 64
SSD_HEADS = 16
SSD_GROUPS = 4
SSD_STATE = 128
SSD_CHUNK = 128
SSD_CONV = 5
S5_GROUP = 16
S5_STATE = 64
S5_BLOCK = 16
ATTN_Q_HEADS = 16
ATTN_KV_HEADS = 4

TOKEN_TILE = 256
HALO_ROWS = 8
VMEM_LIMIT = 56 << 20


def _params(semantics):
    return pltpu.CompilerParams(dimension_semantics=semantics, vmem_limit_bytes=VMEM_LIMIT)


def _sigmoid(x):
    return 1.0 / (1.0 + jnp.exp(-x))


def _silu(x):
    return x * _sigmoid(x)


def _softplus(x):
    return jnp.maximum(x, 0.0) + jnp.log(1.0 + jnp.exp(-jnp.abs(x)))


def _gelu_tanh(x):
    return 0.5 * x * (1.0 + jnp.tanh(math.sqrt(2.0 / math.pi) * (x + 0.044715 * (x * x * x))))


def _rms(x):
    return x * lax.rsqrt(jnp.mean(x * x, axis=-1, keepdims=True) + NORM_EPS)


def _norm_mod(h, mod):
    return _rms(h) * (1.0 + mod[1:2]) + mod[0:1]


def _split_bf16(x):
    hi = x.astype(BF16)
    return hi, (x - hi.astype(F32)).astype(BF16)


def _adaln_kernel(s_ref, w_ref, b_ref, o_ref):
    s = _silu(s_ref[...])
    o_ref[0] = jnp.dot(s, w_ref[0], preferred_element_type=F32, precision=HIGHEST) + b_ref[0]


def _adaln(c, c_ctx, ada_w, ada_b):
    depth, d, d3 = ada_w.shape
    b = c.shape[0]
    assert b < 8
    rows = jnp.concatenate([c, c_ctx[None], jnp.zeros((7 - b, d), F32)], axis=0)
    out = pl.pallas_call(
        _adaln_kernel,
        grid=(depth, d3 // d),
        in_specs=[
            pl.BlockSpec((8, d), lambda i, j: (0, 0)),
            pl.BlockSpec((1, d, d), lambda i, j: (i, 0, j)),
            pl.BlockSpec((1, 1, d), lambda i, j: (i, 0, j)),
        ],
        out_specs=pl.BlockSpec((1, 8, d), lambda i, j: (i, 0, j)),
        out_shape=jax.ShapeDtypeStruct((depth, 8, d3), F32),
        compiler_params=_params(("arbitrary", "arbitrary")),
        name="adaln",
    )(rows, ada_w, ada_b.reshape(depth, 1, d3))
    m = out.reshape(depth, 8, 3, d)
    lat = m[:, :b]
    ctx = jnp.broadcast_to(m[:, b:b + 1], lat.shape)
    return jnp.stack([ctx, lat], axis=2)


def _even_in_kernel(h_ref, mod_ref, w_ref, z_ref, xbc_ref, u_ref, g_ref, dt_ref, *, cuts):
    a = _norm_mod(h_ref[0], mod_ref[0, 0])
    a_hi, a_lo = _split_bf16(a)

    def mm(x, lo, hi):
        return jnp.dot(x, w_ref[:, lo:hi], preferred_element_type=F32)

    c0, c1, c2, c3, c4, c5 = cuts
    z_ref[0] = mm(a_hi, 0, c0)
    xbc_ref[0] = mm(a_hi, c0, c1)
    u_ref[0] = mm(a_hi, c1, c2).astype(u_ref.dtype)
    g_ref[0] = mm(a_hi, c2, c3)
    dt_ref[0] = mm(a_hi, c3, c4) + mm(a_lo, c3, c4) + mm(a_hi, c4, c5)


def _even_in(h, mod, w, n_ctx_tiles, widths):
    b, t, d = h.shape
    tm = TOKEN_TILE
    wz, wx, wu, wg = widths
    cuts = (wz, wz + wx, wz + wx + wu, wz + wx + wu + wg, wz + wx + wu + wg + 128, wz + wx + wu + wg + 256)
    assert w.shape == (d, cuts[-1])
    tok = lambda n: pl.BlockSpec((1, tm, n), lambda i, j: (i, j, 0))
    return pl.pallas_call(
        functools.partial(_even_in_kernel, cuts=cuts),
        grid=(b, t // tm),
        in_specs=[
            tok(d),
            pl.BlockSpec((1, 1, 3, d), lambda i, j: (i, (j >= n_ctx_tiles).astype(jnp.int32), 0, 0)),
            pl.BlockSpec(w.shape, lambda i, j: (0, 0)),
        ],
        out_specs=[tok(wz), tok(wx), tok(wu), tok(wg), tok(128)],
        out_shape=[
            jax.ShapeDtypeStruct((b, t, wz), F32),
            jax.ShapeDtypeStruct((b, t, wx), F32),
            jax.ShapeDtypeStruct((b, t, wu), F32),
            jax.ShapeDtypeStruct((b, t, wg), F32),
            jax.ShapeDtypeStruct((b, t, 128), F32),
        ],
        compiler_params=_params(("parallel", "parallel")),
        name="even_in",
    )(h, mod, w)


def _conv_kernel(prev_ref, main_ref, next_ref, w_ref, b_ref, o_ref, xe_ref, *, n_ctx_tiles, n_tiles):
    tm = TOKEN_TILE
    ch = o_ref.shape[-1]
    t = pl.program_id(1)
    has_prev = jnp.logical_and(t != 0, t != n_ctx_tiles)
    has_next = jnp.logical_and(t != n_ctx_tiles - 1, t != n_tiles - 1)
    xe_ref[0:HALO_ROWS] = jnp.where(has_prev, prev_ref[0], 0.0)
    xe_ref[HALO_ROWS:HALO_ROWS + tm] = main_ref[0]
    xe_ref[HALO_ROWS + tm:2 * HALO_ROWS + tm] = jnp.where(has_next, next_ref[0], 0.0)
    rows = tm + 2 * HALO_ROWS
    half = SSD_CONV // 2
    lanes = 128
    for c0 in range(0, ch, lanes):
        cs = slice(c0, c0 + lanes)
        xe = xe_ref[:, cs]
        acc = b_ref[:, cs] + w_ref[half:half + 1, cs] * xe[HALO_ROWS:HALO_ROWS + tm]
        for k in range(SSD_CONV):
            if k != half:
                shifted = pltpu.roll(xe, (half - k) % rows, 0)
                acc = acc + w_ref[k:k + 1, cs] * shifted[HALO_ROWS:HALO_ROWS + tm]
        o_ref[0, :, cs] = _silu(acc)


def _conv(xbc, conv_w, conv_b, n_ctx_tiles):
    b, t, ch = xbc.shape
    tm = TOKEN_TILE
    n_tiles = t // tm
    per = tm // HALO_ROWS
    last = t // HALO_ROWS - 1
    w = jnp.pad(conv_w.T, ((0, 8 - SSD_CONV), (0, 0)))
    return pl.pallas_call(
        functools.partial(_conv_kernel, n_ctx_tiles=n_ctx_tiles, n_tiles=n_tiles),
        grid=(b, n_tiles),
        in_specs=[
            pl.BlockSpec((1, HALO_ROWS, ch), lambda i, j: (i, jnp.maximum(j * per - 1, 0), 0)),
            pl.BlockSpec((1, tm, ch), lambda i, j: (i, j, 0)),
            pl.BlockSpec((1, HALO_ROWS, ch), lambda i, j: (i, jnp.minimum((j + 1) * per, last), 0)),
            pl.BlockSpec((8, ch), lambda i, j: (0, 0)),
            pl.BlockSpec((1, ch), lambda i, j: (0, 0)),
        ],
        out_specs=pl.BlockSpec((1, tm, ch), lambda i, j: (i, j, 0)),
        out_shape=jax.ShapeDtypeStruct((b, t, ch), F32),
        scratch_shapes=[pltpu.VMEM((tm + 2 * HALO_ROWS, ch), F32)],
        compiler_params=_params(("parallel", "parallel")),
        name="conv",
    )(xbc, xbc, xbc, w, conv_b.reshape(1, ch))


def _ssd_kernel(xf_ref, xb_ref, dtf_ref, dtb_ref, bias_ref, a_ref, e_ref, yf_ref, yb_ref, st_ref):
    L = SSD_CHUNK
    width = SSD_HEADS * HEAD_DIM
    gw = width // SSD_GROUPS
    hpg = SSD_HEADS // SSD_GROUPS

    @pl.when(pl.program_id(1) == 0)
    def _():
        st_ref[...] = jnp.zeros_like(st_ref)

    row = lax.broadcasted_iota(jnp.int32, (L, L), 0)
    col = lax.broadcasted_iota(jnp.int32, (L, L), 1)
    lane_head = lax.broadcasted_iota(jnp.int32, (L, gw), 1) // HEAD_DIM
    sub16 = lax.broadcasted_iota(jnp.int32, (16, 128), 0)

    def split3(x):
        x1 = x.astype(BF16)
        r = x - x1.astype(F32)
        x2 = r.astype(BF16)
        return x1, x2, (r - x2.astype(F32)).astype(BF16)

    def chunk(d, x_ref, dt_ref, y_ref, r0):
        rows = pl.ds(r0, L)
        mask = (row >= col) if d == 0 else (row <= col)
        dtv = _softplus(dt_ref[0, rows, :] + bias_ref[...])
        loga = dtv * a_ref[...]
        tri = jnp.where(mask, 1.0, 0.0).astype(BF16)
        cs = sum(jnp.dot(tri, part, preferred_element_type=F32) for part in split3(loga))
        cs_t = cs.T
        total = jnp.sum(loga, axis=0, keepdims=True)
        t1, t2, t3 = (part.astype(F32) for part in split3(total))
        tot = jnp.where(sub16 == 0, t1, jnp.where(sub16 == 1, t2, jnp.where(sub16 == 2, t3, 0.0)))
        stack = jnp.concatenate(
            [dtv.astype(BF16), jnp.exp(cs).astype(BF16), jnp.exp(total - cs).astype(BF16), tot.astype(BF16)], axis=0)
        ex = jnp.dot(stack, e_ref[d], preferred_element_type=F32)
        dt_x = ex[0:L]
        ecs_x = ex[L:2 * L]
        wend_x = ex[2 * L:3 * L]
        edec_x = jnp.exp(ex[3 * L:3 * L + 1] + ex[3 * L + 1:3 * L + 2] + ex[3 * L + 2:3 * L + 3])

        for g in range(SSD_GROUPS):
            sl = slice(g * gw, (g + 1) * gw)
            xs_g = x_ref[0, rows, sl]
            b_g = x_ref[0, rows, width + g * SSD_STATE:width + (g + 1) * SSD_STATE]
            c_g = x_ref[0, rows, width + (SSD_GROUPS + g) * SSD_STATE:width + (SSD_GROUPS + g + 1) * SSD_STATE]
            cb = c_g.astype(BF16)
            xd = xs_g * dt_x[:, sl]
            gram = lax.dot_general(cb, b_g.astype(BF16), (((1,), (1,)), ((), ())), preferred_element_type=F32)
            scores = []
            for hh in range(hpg):
                li = SSD_HEADS * d + hpg * g + hh
                seg = cs[:, li:li + 1] - cs_t[li:li + 1, :]
                scores.append((gram * jnp.where(mask, jnp.exp(seg), 0.0)).astype(BF16))
            scores = jnp.concatenate(scores, axis=1)
            xd_blocks = jnp.concatenate(
                [jnp.where(lane_head == hh, xd, 0.0).astype(BF16) for hh in range(hpg)], axis=0)
            y = jnp.dot(scores, xd_blocks, preferred_element_type=F32)
            st = st_ref[d, :, sl]
            y = y + jnp.dot(cb, st.astype(BF16), preferred_element_type=F32) * ecs_x[:, sl]
            y_ref[0, rows, sl] = y
            xdw = (xd * wend_x[:, sl]).astype(BF16)
            st_ref[d, :, sl] = st * edec_x[:, sl] + jnp.dot(b_g.T.astype(BF16), xdw, preferred_element_type=F32)

    chunk(0, xf_ref, dtf_ref, yf_ref, 0)
    chunk(1, xb_ref, dtb_ref, yb_ref, L)
    chunk(0, xf_ref, dtf_ref, yf_ref, L)
    chunk(1, xb_ref, dtb_ref, yb_ref, 0)


def _ssd(xbc, dt, dt_bias, a_log, n_ctx):
    b, t, ch = xbc.shape
    L = 2 * SSD_CHUNK
    assert n_ctx % L == 0 and t % L == 0
    n = t // L
    width = SSD_HEADS * HEAD_DIM
    ncc = n_ctx // L

    def fwd(i, j):
        return (i, j, 0)

    def bwd(i, j):
        return (i, jnp.where(j < ncc, ncc - 1 - j, n + ncc - 1 - j), 0)

    pad = lambda v: jnp.pad(v.reshape(1, -1), ((0, 0), (0, 128 - v.size)))
    bias = pad(dt_bias)
    a_neg = pad(-jnp.exp(a_log))
    head_of_lane = jnp.arange(width) // HEAD_DIM
    expand = jnp.stack([
        (jnp.arange(128)[:, None] == SSD_HEADS * d + head_of_lane[None, :]) for d in range(2)
    ]).astype(BF16)
    return pl.pallas_call(
        _ssd_kernel,
        grid=(b, n),
        in_specs=[
            pl.BlockSpec((1, L, ch), fwd),
            pl.BlockSpec((1, L, ch), bwd),
            pl.BlockSpec((1, L, 128), fwd),
            pl.BlockSpec((1, L, 128), bwd),
            pl.BlockSpec((1, 128), lambda i, j: (0, 0)),
            pl.BlockSpec((1, 128), lambda i, j: (0, 0)),
            pl.BlockSpec((2, 128, width), lambda i, j: (0, 0, 0)),
        ],
        out_specs=[pl.BlockSpec((1, L, width), fwd), pl.BlockSpec((1, L, width), bwd)],
        out_shape=[jax.ShapeDtypeStruct((b, t, width), F32)] * 2,
        scratch_shapes=[pltpu.VMEM((2, SSD_STATE, width), F32)],
        compiler_params=_params(("parallel", "arbitrary")),
        name="ssd",
    )(xbc, xbc, dt, dt, bias, a_neg, expand)


def _s5_operators(lam_re, lam_im, log_step, b_re, b_im, c_re, c_im):
    nb = S5_BLOCK
    ng, ns = lam_re.shape[1:]
    hp = functools.partial(jnp.einsum, precision=HIGHEST)
    step = jnp.exp(log_step)[..., None]
    k = jnp.arange(nb + 1, dtype=F32)[:, None, None, None]
    mag = jnp.exp(k * (lam_re * step))
    ak_re = mag * jnp.cos(k * (lam_im * step))
    ak_im = mag * jnp.sin(k * (lam_im * step))
    ab_re, ab_im = ak_re[1], ak_im[1]
    den = lam_re * lam_re + lam_im * lam_im
    f_re = ((ab_re - 1.0) * lam_re + ab_im * lam_im) / den
    f_im = (ab_im * lam_re - (ab_re - 1.0) * lam_im) / den
    tr = lambda a: jnp.swapaxes(a, -1, -2)
    bb_re = tr(f_re[..., None] * b_re - f_im[..., None] * b_im)
    bb_im = tr(f_re[..., None] * b_im + f_im[..., None] * b_re)
    akr, aki = ak_re[:, :, :, None, :], ak_im[:, :, :, None, :]
    w_re = akr * bb_re - aki * bb_im
    w_im = akr * bb_im + aki * bb_re
    kern = hp('kdgjp,dgip->kdgji', w_re[:nb], c_re) - hp('kdgjp,dgip->kdgji', w_im[:nb], c_im)
    kf, kb = kern[:, 0], kern[:, 1]
    zall = jnp.concatenate([kb[:0:-1], (kf[0] + kb[0])[None], kf[1:]], axis=0).transpose(1, 2, 0, 3)
    lag = (nb - 1 - jnp.arange(nb)[:, None] + jnp.arange(nb)[None, :]).reshape(-1)
    m = jnp.take(zall, lag, axis=2).reshape(ng, S5_GROUP, nb, nb * S5_GROUP)
    m = m.transpose(0, 2, 1, 3).reshape(ng, nb * S5_GROUP, nb * S5_GROUP)
    inj = lambda w, d, rev: jnp.swapaxes(w[nb - 1::-1, d] if rev else w[:nb, d], 0, 1)
    def readout(d, ks):
        ar, ai = (a[ks, d].transpose(1, 2, 0)[..., None] for a in (ak_re, ak_im))
        cr, ci = (tr(c[d])[:, :, None, :] for c in (c_re, c_im))
        return cr * ar - ci * ai, -(cr * ai + ci * ar)
    of_re, of_im = readout(0, jnp.arange(1, nb + 1))
    ob_re, ob_im = readout(1, jnp.arange(nb, 0, -1))
    odd = (jnp.arange(ng) % 2 == 1)
    pick = lambda a, b: jnp.where(odd.reshape((ng,) + (1,) * (a.ndim - 1)), b, a)
    sf_re, sb_re, sf_im, sb_im = inj(w_re, 0, True), inj(w_re, 1, False), inj(w_im, 0, True), inj(w_im, 1, False)
    smat = jnp.stack([pick(sf_re, sb_re), pick(sb_re, sf_re), pick(sf_im, sb_im), pick(sb_im, sf_im)], axis=3)
    smat = smat.reshape(ng, nb * S5_GROUP, 4 * ns)
    o_all = jnp.stack([pick(of_re, ob_re), pick(ob_re, of_re), pick(of_im, ob_im), pick(ob_im, of_im)], axis=1)
    o_all = o_all.reshape(ng, 4 * ns, nb * S5_GROUP)
    yw = jnp.concatenate([m, o_all], axis=1)
    swap = lambda a: a.reshape(ng // 2, 2, ns)[:, ::-1].reshape(ng * ns)
    dec = jnp.stack([ak_re[nb, 0].reshape(-1), ak_im[nb, 0].reshape(-1), swap(ak_re[nb, 1]), swap(ak_im[nb, 1])])
    return smat.astype(BF16), yw.astype(BF16), dec


def _s5_kernel(u_ref, smat_ref, yw_ref, dec_ref, y_ref, us, uf, ys, sfr, sbr, sfi, sbi, hfr, hbr, hfi, hbi, *,
               n_ctx_blocks):
    nb = S5_BLOCK
    groups = smat_ref.shape[0]
    n = u_ref.shape[1] // nb
    ncb = n_ctx_blocks
    seg = S5_GROUP
    per_col = 128 // seg
    lane_seg = lax.broadcasted_iota(jnp.int32, (n, 128), 1) // seg
    low = lax.broadcasted_iota(jnp.int32, (n, 128), 1) < S5_STATE

    for s in range(nb):
        us[s] = u_ref[0, pl.ds(s, n, stride=nb), :]

    def fold(g):
        for v in range(nb // per_col):
            col = None
            for k in range(per_col):
                x = us[v * per_col + k]
                shift = (seg * (k - g)) % 128
                if shift:
                    x = pltpu.roll(x, shift, 1)
                col = x if col is None else jnp.where(lane_seg == k, x, col)
            uf[g, :, 128 * v:128 * (v + 1)] = col.astype(BF16)
        return jnp.dot(uf[g], smat_ref[g], preferred_element_type=F32)

    for p in range(groups // 2):
        se, so = fold(2 * p), fold(2 * p + 1)
        sl = slice(128 * p, 128 * (p + 1))
        sfr[:, sl] = jnp.where(low, se[:, :128], so[:, :128])
        sbr[:, sl] = jnp.where(low, so[:, :128], se[:, :128])
        sfi[:, sl] = jnp.where(low, se[:, 128:], so[:, 128:])
        sbi[:, sl] = jnp.where(low, so[:, 128:], se[:, 128:])
    arf, aif, arb, aib = dec_ref[0, 0:1], dec_ref[0, 1:2], dec_ref[0, 2:3], dec_ref[0, 3:4]

    def step(i, carry):
        fr, fi, br, bi = carry
        rf = i
        rb = jnp.where(i < ncb, ncb - 1 - i, n + ncb - 1 - i)
        hfr[pl.ds(rf, 1), :] = fr
        hfi[pl.ds(rf, 1), :] = fi
        hbr[pl.ds(rb, 1), :] = br
        hbi[pl.ds(rb, 1), :] = bi
        nfr = arf * fr - aif * fi + sfr[pl.ds(rf, 1), :]
        nfi = arf * fi + aif * fr + sfi[pl.ds(rf, 1), :]
        nbr = arb * br - aib * bi + sbr[pl.ds(rb, 1), :]
        nbi = arb * bi + aib * br + sbi[pl.ds(rb, 1), :]
        return nfr, nfi, nbr, nbi

    zero = jnp.zeros((1, 128 * (groups // 2)), F32)
    lax.fori_loop(0, n, step, (zero, zero, zero, zero))
    for g in range(groups):
        sl = slice(128 * (g // 2), 128 * (g // 2 + 1))
        first, second = ((hfr, hfi), (hbr, hbi)) if g % 2 == 0 else ((hbr, hbi), (hfr, hfi))
        lhs = jnp.concatenate(
            [uf[g]] + [jnp.where(low, a[:, sl], b[:, sl]).astype(BF16) for a, b in zip(first, second)], axis=1)
        ys[g] = jnp.dot(lhs, yw_ref[g], preferred_element_type=F32)
    for l in range(nb):
        v, k = divmod(l, per_col)
        out = None
        for g in range(groups):
            x = ys[g, :, 128 * v:128 * (v + 1)]
            shift = (seg * (g - k)) % 128
            if shift:
                x = pltpu.roll(x, shift, 1)
            out = x if out is None else jnp.where(lane_seg == g, x, out)
        y_ref[0, pl.ds(l, n, stride=nb), :] = out


def _s5(u, ops, n_ctx_blocks):
    smat, yw, dec = ops
    b, t, w = u.shape
    nb = S5_BLOCK
    n = t // nb
    gw = nb * S5_GROUP
    gpb = 128 // S5_GROUP
    steps = w // 128
    lanes = gpb * S5_STATE
    dec = dec.reshape(4, steps, lanes).transpose(1, 0, 2)
    dec = jnp.concatenate([dec, jnp.zeros_like(dec)], axis=1)
    return pl.pallas_call(
        functools.partial(_s5_kernel, n_ctx_blocks=n_ctx_blocks),
        grid=(b, steps),
        in_specs=[
            pl.BlockSpec((1, t, 128), lambda i, j: (i, 0, j)),
            pl.BlockSpec((gpb,) + smat.shape[1:], lambda i, j: (j, 0, 0)),
            pl.BlockSpec((gpb,) + yw.shape[1:], lambda i, j: (j, 0, 0)),
            pl.BlockSpec((1, 8, lanes), lambda i, j: (j, 0, 0)),
        ],
        out_specs=pl.BlockSpec((1, t, 128), lambda i, j: (i, 0, j)),
        out_shape=jax.ShapeDtypeStruct((b, t, w), F32),
        scratch_shapes=[pltpu.VMEM((nb, n, 128), F32), pltpu.VMEM((gpb, n, gw), BF16), pltpu.VMEM((gpb, n, gw), F32)]
        + [pltpu.VMEM((n, lanes), F32)] * 8,
        compiler_params=_params(("parallel", "parallel")),
        name="s5",
    )(u, smat, yw, dec)


def _even_out_kernel(yf_ref, yb_ref, xs_ref, z_ref, y5_ref, u_ref, g_ref, h_ref, mod_ref, vs_ref, v5_ref,
                     glu_ref, w_ref, o_ref):
    ws = z_ref.shape[-1]
    y = _gelu_tanh(y5_ref[0] + v5_ref[0:1] * u_ref[0])
    glu = jnp.dot(y.astype(BF16), glu_ref[...], preferred_element_type=F32)
    ys = yf_ref[0] + yb_ref[0] + vs_ref[1:2] * xs_ref[0]
    s = _rms(ys * _silu(z_ref[0])) * vs_ref[0:1]
    o = jnp.dot(s.astype(BF16), w_ref[0:ws], preferred_element_type=F32)
    y = y * _sigmoid(glu + v5_ref[1:2]) * _silu(g_ref[0])
    o = o + jnp.dot(y.astype(BF16), w_ref[ws:], preferred_element_type=F32)
    o_ref[0] = h_ref[0] + mod_ref[0, 0][2:3] * o


def _even_out(yf, yb, xbc, z, y5, u, g, h, mod, ssd_norm, d_ssd, d_s5, glu_w, glu_b, w_out, n_ctx_tiles):
    b, t, d = h.shape
    tm = TOKEN_TILE
    ws, w5 = z.shape[-1], u.shape[-1]
    vs = jnp.pad(jnp.stack([ssd_norm, jnp.repeat(d_ssd, HEAD_DIM)]), ((0, 6), (0, 0)))
    v5 = jnp.pad(jnp.stack([d_s5, glu_b]), ((0, 6), (0, 0)))
    tok = lambda n: pl.BlockSpec((1, tm, n), lambda i, j: (i, j, 0))
    const = lambda a: pl.BlockSpec(a.shape, lambda i, j: (0,) * a.ndim)
    glu_w = glu_w.astype(BF16)
    w_out = w_out.astype(BF16)
    return pl.pallas_call(
        _even_out_kernel,
        grid=(b, t // tm),
        in_specs=[
            tok(ws), tok(ws), tok(ws), tok(ws), tok(w5), tok(w5), tok(w5), tok(d),
            pl.BlockSpec((1, 1, 3, d), lambda i, j: (i, (j >= n_ctx_tiles).astype(jnp.int32), 0, 0)),
            const(vs), const(v5), const(glu_w), const(w_out),
        ],
        out_specs=tok(d),
        out_shape=jax.ShapeDtypeStruct((b, t, d), F32),
        compiler_params=_params(("parallel", "parallel")),
        name="even_out",
    )(yf, yb, xbc, z, y5, u, g, h, mod, vs, v5, glu_w, w_out)


def _odd_in_kernel(h_ref, mod_ref, w_ref, qg_ref, kg_ref, cos_ref, sin_ref, ones_ref, q_ref, k_ref, v_ref, g_ref, *,
                   q_w, kv_w):
    tm = TOKEN_TILE
    a = _norm_mod(h_ref[0], mod_ref[0, 0]).astype(BF16)
    cosv = cos_ref[...]
    sinv = sin_ref[...]
    first_half = (lax.broadcasted_iota(jnp.int32, (tm, 128), 1) % (HEAD_DIM // 2)) < (HEAD_DIM // 4)

    def project(lo, width=256):
        return jnp.dot(a, w_ref[:, lo:lo + width], preferred_element_type=F32)

    def head_norm_rope(x, gain, out_ref, c):
        ms = jnp.dot((x * x).astype(BF16), ones_ref[...], preferred_element_type=F32) * (1.0 / HEAD_DIM)
        xn = x * lax.rsqrt(ms + NORM_EPS) * gain
        for s in range(2):
            xb = xn[:, 128 * s:128 * (s + 1)]
            partner = jnp.where(first_half, pltpu.roll(xb, 128 - HEAD_DIM // 4, 1), pltpu.roll(xb, HEAD_DIM // 4, 1))
            r = xb * cosv + partner * sinv
            head = 4 * c + 2 * s
            out_ref[0, head] = r[:, :HEAD_DIM].astype(out_ref.dtype)
            out_ref[0, head + 1] = pltpu.roll(r, HEAD_DIM, 1)[:, :HEAD_DIM].astype(out_ref.dtype)

    chunks = [(256 * c, qg_ref, q_ref, c) for c in range(q_w // 256)] + \
             [(q_w + 256 * c, kg_ref, k_ref, c) for c in range(kv_w // 256)]
    g_lo = q_w + 2 * kv_w
    g_w = w_ref.shape[1] - g_lo
    x_next = project(chunks[0][0])
    for i, (lo, gain_ref, out_ref, c) in enumerate(chunks):
        x = x_next
        if i + 1 < len(chunks):
            x_next = project(chunks[i + 1][0])
        if 256 * i < g_w:
            g_ref[0, :, 256 * i:256 * (i + 1)] = project(g_lo + 256 * i)
        head_norm_rope(x, gain_ref[:, 256 * c:256 * (c + 1)], out_ref, c)
    assert 256 * len(chunks) >= g_w
    lane = lax.broadcasted_iota(jnp.int32, (tm, 128), 1)
    for c in range(kv_w // 128):
        x = project(q_w + kv_w + 128 * c, 128)
        for s in range(2):
            xs = pltpu.roll(x, HEAD_DIM, 1) if s else x
            vh = jnp.where(lane < HEAD_DIM, xs, jnp.where(lane == HEAD_DIM, 1.0, 0.0))
            v_ref[0, 2 * c + s] = vh.astype(v_ref.dtype)


def _rope_tables(n_ctx, n_lat):
    pairs = HEAD_DIM // 4
    pos = jnp.arange(n_lat)
    row = (pos // GRID_W).astype(F32)
    colp = (pos % GRID_W).astype(F32)
    inv = ROPE_THETA ** (-jnp.arange(pairs, dtype=F32) / pairs)
    lane = jnp.arange(128) % HEAD_DIM
    axis_is_col = (lane // (HEAD_DIM // 2)) == 1
    ang = jnp.where(axis_is_col[None, :], colp[:, None], row[:, None]) * inv[lane % pairs][None, :]
    sign = jnp.where((lane % (HEAD_DIM // 2)) < pairs, -1.0, 1.0)
    cos = jnp.concatenate([jnp.ones((n_ctx, 128), F32), jnp.cos(ang)], axis=0)
    sin = jnp.concatenate([jnp.zeros((n_ctx, 128), F32), jnp.sin(ang) * sign[None, :]], axis=0)
    return cos, sin


def _odd_in(h, mod, w, q_gain, k_gain, cos, sin, n_ctx_tiles):
    b, t, d = h.shape
    tm = TOKEN_TILE
    q_w = ATTN_Q_HEADS * HEAD_DIM
    kv_w = ATTN_KV_HEADS * HEAD_DIM
    qg = (jnp.tile(q_gain, ATTN_Q_HEADS) * (HEAD_DIM ** -0.5 * math.log2(math.e))).reshape(1, q_w)
    kg = jnp.tile(k_gain, ATTN_KV_HEADS).reshape(1, kv_w)
    blk = jnp.arange(256) // HEAD_DIM
    ones = (blk[:, None] == blk[None, :]).astype(BF16)
    tok = lambda n: pl.BlockSpec((1, tm, n), lambda i, j: (i, j, 0))
    heads = lambda nh, n: pl.BlockSpec((1, nh, tm, n), lambda i, j: (i, 0, j, 0))
    const = lambda a: pl.BlockSpec(a.shape, lambda i, j: (0,) * a.ndim)
    return pl.pallas_call(
        functools.partial(_odd_in_kernel, q_w=q_w, kv_w=kv_w),
        grid=(b, t // tm),
        in_specs=[
            tok(d),
            pl.BlockSpec((1, 1, 3, d), lambda i, j: (i, (j >= n_ctx_tiles).astype(jnp.int32), 0, 0)),
            const(w), const(qg), const(kg),
            pl.BlockSpec((tm, 128), lambda i, j: (j, 0)),
            pl.BlockSpec((tm, 128), lambda i, j: (j, 0)),
            const(ones),
        ],
        out_specs=[heads(ATTN_Q_HEADS, HEAD_DIM), heads(ATTN_KV_HEADS, HEAD_DIM), heads(ATTN_KV_HEADS, 128), tok(q_w)],
        out_shape=[
            jax.ShapeDtypeStruct((b, ATTN_Q_HEADS, t, HEAD_DIM), BF16),
            jax.ShapeDtypeStruct((b, ATTN_KV_HEADS, t, HEAD_DIM), BF16),
            jax.ShapeDtypeStruct((b, ATTN_KV_HEADS, t, 128), BF16),
            jax.ShapeDtypeStruct((b, t, q_w), F32),
        ],
        compiler_params=_params(("parallel", "parallel")),
        name="odd_in",
    )(h, mod, w, qg, kg, cos, sin, ones)


def _attn_kernel(q_ref, k_ref, v_ref, o_ref, *, tk):
    rep, tq, hd = q_ref.shape[1:]
    nk = k_ref.shape[2] // tk
    q = q_ref[0].reshape(rep * tq, hd)

    def body(j, carry):
        m, acc = carry
        off = pl.multiple_of(j * tk, tk)
        k = k_ref[0, 0, pl.ds(off, tk), :]
        v = v_ref[0, 0, pl.ds(off, tk), :]
        s = lax.dot_general(q, k, (((1,), (1,)), ((), ())), preferred_element_type=F32)
        m_new = jnp.maximum(m, jnp.max(s, axis=-1, keepdims=True))
        p = jnp.exp2(s - m_new)
        acc = jnp.exp2(m - m_new) * acc + jnp.dot(p.astype(BF16), v, preferred_element_type=F32)
        return m_new, acc

    init = (jnp.full((rep * tq, 1), -1e30, F32), jnp.zeros((rep * tq, 128), F32))
    _, acc = lax.fori_loop(0, nk, body, init, unroll=True)
    o = acc * (1.0 / acc[:, hd:hd + 1])
    lane = lax.broadcasted_iota(jnp.int32, (tq, 128), 1)
    for c in range(rep // 2):
        o_ref[0, :, 128 * c:128 * (c + 1)] = jnp.where(
            lane < hd, o[2 * c * tq:(2 * c + 1) * tq], pltpu.roll(o[(2 * c + 1) * tq:(2 * c + 2) * tq], hd, 1)
        ).astype(o_ref.dtype)


def _attention(q, k, v, n_ctx):
    b, hq, t, hd = q.shape
    hkv = k.shape[1]
    rep = hq // hkv
    tq = 256
    tk = next(c for c in (768, 512, 256, 128) if t % c == 0)
    assert n_ctx % tq == 0
    return pl.pallas_call(
        functools.partial(_attn_kernel, tk=tk),
        grid=(b, hkv, (t - n_ctx) // tq),
        in_specs=[
            pl.BlockSpec((1, rep, tq, hd), lambda i, j, n: (i, j, n + n_ctx // tq, 0)),
            pl.BlockSpec((1, 1, t, hd), lambda i, j, n: (i, j, 0, 0)),
            pl.BlockSpec((1, 1, t, 128), lambda i, j, n: (i, j, 0, 0)),
        ],
        out_specs=pl.BlockSpec((1, tq, rep * hd), lambda i, j, n: (i, n, j)),
        out_shape=jax.ShapeDtypeStruct((b, t - n_ctx, hq * hd), BF16),
        compiler_params=_params(("parallel", "parallel", "parallel")),
        name="attention",
    )(q, k, v)


def _attn_out_kernel(o_ref, g_ref, h_ref, mod_ref, w_ref, fg_ref, out_ref):
    x = o_ref[0].astype(F32) * _silu(g_ref[0])
    y = jnp.dot(x.astype(BF16), w_ref[...], preferred_element_type=F32)
    hn = h_ref[0] + mod_ref[0, 0][2:3] * y
    out_ref[0] = _rms(hn) * fg_ref[...]


def _attn_out(o, g, h, mod, w_out, final_gain, n_ctx_tiles):
    b, n_lat, d = o.shape
    tm = TOKEN_TILE
    lat = lambda n: pl.BlockSpec((1, tm, n), lambda i, j: (i, j + n_ctx_tiles, 0))
    w_out = w_out.astype(BF16)
    return pl.pallas_call(
        _attn_out_kernel,
        grid=(b, n_lat // tm),
        in_specs=[
            pl.BlockSpec((1, tm, d), lambda i, j: (i, j, 0)),
            lat(d), lat(d),
            pl.BlockSpec((1, 1, 3, d), lambda i, j: (i, 1, 0, 0)),
            pl.BlockSpec(w_out.shape, lambda i, j: (0, 0)),
            pl.BlockSpec((1, d), lambda i, j: (0, 0)),
        ],
        out_specs=pl.BlockSpec((1, tm, d), lambda i, j: (i, j, 0)),
        out_shape=jax.ShapeDtypeStruct((b, n_lat, d), F32),
        compiler_params=_params(("parallel", "parallel")),
        name="attn_out",
    )(o, g, h, mod, w_out, final_gain.reshape(1, d))


def kernel(x, c, ctx, c_ctx, ada_w, ada_b, ev_w_in, ev_conv_w, ev_conv_b, ev_dt_bias, ev_a_log, ev_d_ssd, ev_ssd_norm, ev_lam_re, ev_lam_im, ev_log_step, ev_b_re, ev_b_im, ev_c_re, ev_c_im, ev_d_s5, ev_glu_w, ev_glu_b, ev_w_out, od_w_in, od_q_gain, od_k_gain, od_w_out, final_gain):
    b, n_lat, d = x.shape
    n_ctx = ctx.shape[1]
    assert ada_w.shape[0] == 2 and n_ctx % TOKEN_TILE == 0 and n_lat % TOKEN_TILE == 0
    n_ctx_tiles = n_ctx // TOKEN_TILE
    mods = _adaln(c, c_ctx, ada_w, ada_b)
    h = jnp.concatenate([ctx, x], axis=1)

    ws = SSD_HEADS * HEAD_DIM
    wx = ws + 2 * SSD_GROUPS * SSD_STATE
    w5 = ev_d_s5.shape[-1]
    w = ev_w_in[0]
    cuts = (ws, ws + wx, ws + wx + 2 * SSD_HEADS, ws + wx + 2 * SSD_HEADS + w5)
    w_dt = jnp.pad(w[:, cuts[1]:cuts[2]], ((0, 0), (0, 128 - 2 * SSD_HEADS)))
    w_dt_hi, w_dt_lo = _split_bf16(w_dt)
    w_cat = jnp.concatenate(
        [w[:, :cuts[1]].astype(BF16), w[:, cuts[2]:].astype(BF16), w_dt_hi, w_dt_lo], axis=1)
    z, xbc, u, g, dt = _even_in(h, mods[0], w_cat, n_ctx_tiles, (ws, wx, w5, w5))
    xbc = _conv(xbc, ev_conv_w[0], ev_conv_b[0], n_ctx_tiles)
    yf, yb = _ssd(xbc, dt, ev_dt_bias[0], ev_a_log[0], n_ctx)
    ops = _s5_operators(ev_lam_re[0], ev_lam_im[0], ev_log_step[0], ev_b_re[0], ev_b_im[0], ev_c_re[0], ev_c_im[0])
    y5 = _s5(u, ops, n_ctx // S5_BLOCK)
    h = _even_out(yf, yb, xbc, z, y5, u, g, h, mods[0], ev_ssd_norm[0], ev_d_ssd[0], ev_d_s5[0], ev_glu_w[0],
                  ev_glu_b[0], ev_w_out[0], n_ctx_tiles)

    cos, sin = _rope_tables(n_ctx, n_lat)
    q, k, v, g = _odd_in(h, mods[1], od_w_in[0].astype(BF16), od_q_gain[0], od_k_gain[0], cos, sin, n_ctx_tiles)
    o = _attention(q, k, v, n_ctx)
    return _attn_out(o, g, h, mods[1], od_w_out[0], final_gain, n_ctx_tiles)
```

```python
import functools
import math

import jax
import jax.numpy as jnp
from jax import lax
from jax.experimental import pallas as pl
from jax.experimental.pallas import tpu as pltpu

F32 = jnp.float32
BF16 = jnp.bfloat16
HIGHEST = lax.Precision.HIGHEST

NORM_EPS = 1e-6
GRID_W = 64
ROPE_THETA = 10000.0

HEAD_DIM = 64
SSD_HEADS = 16
SSD_GROUPS = 4
SSD_STATE = 128
SSD_CHUNK = 128
SSD_CONV = 5
S5_GROUP = 16
S5_STATE = 64
S5_BLOCK = 16
ATTN_Q_HEADS = 16
ATTN_KV_HEADS = 4
VT_ROWS = 80
ATTN_LOOKAHEAD = 8

TOKEN_TILE = 256
HALO_ROWS = 8
VMEM_LIMIT = 56 << 20


def _params(semantics):
    return pltpu.CompilerParams(dimension_semantics=semantics, vmem_limit_bytes=VMEM_LIMIT)


def _sigmoid(x):
    return 1.0 / (1.0 + jnp.exp(-x))


def _silu(x):
    return x * _sigmoid(x)


def _softplus(x):
    return jnp.maximum(x, 0.0) + jnp.log(1.0 + jnp.exp(-jnp.abs(x)))


def _gelu_tanh(x):
    return 0.5 * x * (1.0 + jnp.tanh(math.sqrt(2.0 / math.pi) * (x + 0.044715 * (x * x * x))))


def _rms(x):
    return x * lax.rsqrt(jnp.mean(x * x, axis=-1, keepdims=True) + NORM_EPS)


def _norm_mod(h, mod):
    return _rms(h) * (1.0 + mod[1:2]) + mod[0:1]


def _split_bf16(x):
    hi = x.astype(BF16)
    return hi, (x - hi.astype(F32)).astype(BF16)


def _adaln_kernel(s_ref, w_ref, b_ref, o_ref):
    s = _silu(s_ref[...])
    o_ref[0] = jnp.dot(s, w_ref[0], preferred_element_type=F32, precision=HIGHEST) + b_ref[0]


def _adaln(c, c_ctx, ada_w, ada_b):
    depth, d, d3 = ada_w.shape
    b = c.shape[0]
    assert b < 8
    rows = jnp.concatenate([c, c_ctx[None], jnp.zeros((7 - b, d), F32)], axis=0)
    out = pl.pallas_call(
        _adaln_kernel,
        grid=(depth, d3 // d),
        in_specs=[
            pl.BlockSpec((8, d), lambda i, j: (0, 0)),
            pl.BlockSpec((1, d, d), lambda i, j: (i, 0, j)),
            pl.BlockSpec((1, 1, d), lambda i, j: (i, 0, j)),
        ],
        out_specs=pl.BlockSpec((1, 8, d), lambda i, j: (i, 0, j)),
        out_shape=jax.ShapeDtypeStruct((depth, 8, d3), F32),
        compiler_params=_params(("arbitrary", "arbitrary")),
        name="adaln",
    )(rows, ada_w, ada_b.reshape(depth, 1, d3))
    m = out.reshape(depth, 8, 3, d)
    lat = m[:, :b]
    ctx = jnp.broadcast_to(m[:, b:b + 1], lat.shape)
    return jnp.stack([ctx, lat], axis=2)


def _even_in_kernel(h_ref, mod_ref, w_ref, z_ref, xbc_ref, u_ref, g_ref, dt_ref, *, cuts):
    a = _norm_mod(h_ref[0], mod_ref[0, 0])
    a_hi, a_lo = _split_bf16(a)

    def mm(x, lo, hi):
        return jnp.dot(x, w_ref[:, lo:hi], preferred_element_type=F32)

    c0, c1, c2, c3, c4, c5 = cuts
    z_ref[0] = mm(a_hi, 0, c0)
    xbc_ref[0] = mm(a_hi, c0, c1)
    u_ref[0] = mm(a_hi, c1, c2).astype(u_ref.dtype)
    g_ref[0] = mm(a_hi, c2, c3)
    dt_ref[0] = mm(a_hi, c3, c4) + mm(a_lo, c3, c4) + mm(a_hi, c4, c5)


def _even_in(h, mod, w, n_ctx_tiles, widths):
    b, t, d = h.shape
    tm = TOKEN_TILE
    wz, wx, wu, wg = widths
    cuts = (wz, wz + wx, wz + wx + wu, wz + wx + wu + wg, wz + wx + wu + wg + 128, wz + wx + wu + wg + 256)
    assert w.shape == (d, cuts[-1])
    tok = lambda n: pl.BlockSpec((1, tm, n), lambda i, j: (i, j, 0))
    return pl.pallas_call(
        functools.partial(_even_in_kernel, cuts=cuts),
        grid=(b, t // tm),
        in_specs=[
            tok(d),
            pl.BlockSpec((1, 1, 3, d), lambda i, j: (i, (j >= n_ctx_tiles).astype(jnp.int32), 0, 0)),
            pl.BlockSpec(w.shape, lambda i, j: (0, 0)),
        ],
        out_specs=[tok(wz), tok(wx), tok(wu), tok(wg), tok(128)],
        out_shape=[
            jax.ShapeDtypeStruct((b, t, wz), F32),
            jax.ShapeDtypeStruct((b, t, wx), F32),
            jax.ShapeDtypeStruct((b, t, wu), F32),
            jax.ShapeDtypeStruct((b, t, wg), F32),
            jax.ShapeDtypeStruct((b, t, 128), F32),
        ],
        compiler_params=_params(("parallel", "parallel")),
        name="even_in",
    )(h, mod, w)


def _conv_kernel(prev_ref, main_ref, next_ref, w_ref, b_ref, o_ref, xe_ref, *, n_ctx_tiles, n_tiles):
    tm = TOKEN_TILE
    ch = o_ref.shape[-1]
    t = pl.program_id(1)
    has_prev = jnp.logical_and(t != 0, t != n_ctx_tiles)
    has_next = jnp.logical_and(t != n_ctx_tiles - 1, t != n_tiles - 1)
    xe_ref[0:HALO_ROWS] = jnp.where(has_prev, prev_ref[0], 0.0)
    xe_ref[HALO_ROWS:HALO_ROWS + tm] = main_ref[0]
    xe_ref[HALO_ROWS + tm:2 * HALO_ROWS + tm] = jnp.where(has_next, next_ref[0], 0.0)
    rows = tm + 2 * HALO_ROWS
    half = SSD_CONV // 2
    lanes = 128
    for c0 in range(0, ch, lanes):
        cs = slice(c0, c0 + lanes)
        xe = xe_ref[:, cs]
        acc = b_ref[:, cs] + w_ref[half:half + 1, cs] * xe[HALO_ROWS:HALO_ROWS + tm]
        for k in range(SSD_CONV):
            if k != half:
                shifted = pltpu.roll(xe, (half - k) % rows, 0)
                acc = acc + w_ref[k:k + 1, cs] * shifted[HALO_ROWS:HALO_ROWS + tm]
        o_ref[0, :, cs] = _silu(acc)


def _conv(xbc, conv_w, conv_b, n_ctx_tiles):
    b, t, ch = xbc.shape
    tm = TOKEN_TILE
    n_tiles = t // tm
    per = tm // HALO_ROWS
    last = t // HALO_ROWS - 1
    w = jnp.pad(conv_w.T, ((0, 8 - SSD_CONV), (0, 0)))
    return pl.pallas_call(
        functools.partial(_conv_kernel, n_ctx_tiles=n_ctx_tiles, n_tiles=n_tiles),
        grid=(b, n_tiles),
        in_specs=[
            pl.BlockSpec((1, HALO_ROWS, ch), lambda i, j: (i, jnp.maximum(j * per - 1, 0), 0)),
            pl.BlockSpec((1, tm, ch), lambda i, j: (i, j, 0)),
            pl.BlockSpec((1, HALO_ROWS, ch), lambda i, j: (i, jnp.minimum((j + 1) * per, last), 0)),
            pl.BlockSpec((8, ch), lambda i, j: (0, 0)),
            pl.BlockSpec((1, ch), lambda i, j: (0, 0)),
        ],
        out_specs=pl.BlockSpec((1, tm, ch), lambda i, j: (i, j, 0)),
        out_shape=jax.ShapeDtypeStruct((b, t, ch), F32),
        scratch_shapes=[pltpu.VMEM((tm + 2 * HALO_ROWS, ch), F32)],
        compiler_params=_params(("parallel", "parallel")),
        name="conv",
    )(xbc, xbc, xbc, w, conv_b.reshape(1, ch))


def _ssd_kernel(xf_ref, xb_ref, dtf_ref, dtb_ref, bias_ref, a_ref, e_ref, yf_ref, yb_ref, st_ref):
    L = SSD_CHUNK
    width = SSD_HEADS * HEAD_DIM
    gw = width // SSD_GROUPS
    hpg = SSD_HEADS // SSD_GROUPS

    @pl.when(pl.program_id(1) == 0)
    def _():
        st_ref[...] = jnp.zeros_like(st_ref)

    row = lax.broadcasted_iota(jnp.int32, (L, L), 0)
    col = lax.broadcasted_iota(jnp.int32, (L, L), 1)
    lane_head = lax.broadcasted_iota(jnp.int32, (L, gw), 1) // HEAD_DIM
    sub16 = lax.broadcasted_iota(jnp.int32, (16, 128), 0)

    def split3(x):
        x1 = x.astype(BF16)
        r = x - x1.astype(F32)
        x2 = r.astype(BF16)
        return x1, x2, (r - x2.astype(F32)).astype(BF16)

    def chunk(d, x_ref, dt_ref, y_ref, r0):
        rows = pl.ds(r0, L)
        mask = (row >= col) if d == 0 else (row <= col)
        dtv = _softplus(dt_ref[0, rows, :] + bias_ref[...])
        loga = dtv * a_ref[...]
        tri = jnp.where(mask, 1.0, 0.0).astype(BF16)
        cs = sum(jnp.dot(tri, part, preferred_element_type=F32) for part in split3(loga))
        cs_t = cs.T
        total = jnp.sum(loga, axis=0, keepdims=True)
        t1, t2, t3 = (part.astype(F32) for part in split3(total))
        tot = jnp.where(sub16 == 0, t1, jnp.where(sub16 == 1, t2, jnp.where(sub16 == 2, t3, 0.0)))
        stack = jnp.concatenate(
            [dtv.astype(BF16), jnp.exp(cs).astype(BF16), jnp.exp(total - cs).astype(BF16), tot.astype(BF16)], axis=0)
        ex = jnp.dot(stack, e_ref[d], preferred_element_type=F32)
        dt_x = ex[0:L]
        ecs_x = ex[L:2 * L]
        wend_x = ex[2 * L:3 * L]
        edec_x = jnp.exp(ex[3 * L:3 * L + 1] + ex[3 * L + 1:3 * L + 2] + ex[3 * L + 2:3 * L + 3])

        for g in range(SSD_GROUPS):
            sl = slice(g * gw, (g + 1) * gw)
            xs_g = x_ref[0, rows, sl]
            b_g = x_ref[0, rows, width + g * SSD_STATE:width + (g + 1) * SSD_STATE]
            c_g = x_ref[0, rows, width + (SSD_GROUPS + g) * SSD_STATE:width + (SSD_GROUPS + g + 1) * SSD_STATE]
            cb = c_g.astype(BF16)
            xd = xs_g * dt_x[:, sl]
            gram = lax.dot_general(cb, b_g.astype(BF16), (((1,), (1,)), ((), ())), preferred_element_type=F32)
            scores = []
            for hh in range(hpg):
                li = SSD_HEADS * d + hpg * g + hh
                seg = cs[:, li:li + 1] - cs_t[li:li + 1, :]
                scores.append((gram * jnp.where(mask, jnp.exp(seg), 0.0)).astype(BF16))
            scores = jnp.concatenate(scores, axis=1)
            xd_blocks = jnp.concatenate(
                [jnp.where(lane_head == hh, xd, 0.0).astype(BF16) for hh in range(hpg)], axis=0)
            y = jnp.dot(scores, xd_blocks, preferred_element_type=F32)
            st = st_ref[d, :, sl]
            y = y + jnp.dot(cb, st.astype(BF16), preferred_element_type=F32) * ecs_x[:, sl]
            y_ref[0, rows, sl] = y
            xdw = (xd * wend_x[:, sl]).astype(BF16)
            st_ref[d, :, sl] = st * edec_x[:, sl] + jnp.dot(b_g.T.astype(BF16), xdw, preferred_element_type=F32)

    chunk(0, xf_ref, dtf_ref, yf_ref, 0)
    chunk(1, xb_ref, dtb_ref, yb_ref, L)
    chunk(0, xf_ref, dtf_ref, yf_ref, L)
    chunk(1, xb_ref, dtb_ref, yb_ref, 0)


def _ssd(xbc, dt, dt_bias, a_log, n_ctx):
    b, t, ch = xbc.shape
    L = 2 * SSD_CHUNK
    assert n_ctx % L == 0 and t % L == 0
    n = t // L
    width = SSD_HEADS * HEAD_DIM
    ncc = n_ctx // L

    def fwd(i, j):
        return (i, j, 0)

    def bwd(i, j):
        return (i, jnp.where(j < ncc, ncc - 1 - j, n + ncc - 1 - j), 0)

    pad = lambda v: jnp.pad(v.reshape(1, -1), ((0, 0), (0, 128 - v.size)))
    bias = pad(dt_bias)
    a_neg = pad(-jnp.exp(a_log))
    head_of_lane = jnp.arange(width) // HEAD_DIM
    expand = jnp.stack([
        (jnp.arange(128)[:, None] == SSD_HEADS * d + head_of_lane[None, :]) for d in range(2)
    ]).astype(BF16)
    return pl.pallas_call(
        _ssd_kernel,
        grid=(b, n),
        in_specs=[
            pl.BlockSpec((1, L, ch), fwd),
            pl.BlockSpec((1, L, ch), bwd),
            pl.BlockSpec((1, L, 128), fwd),
            pl.BlockSpec((1, L, 128), bwd),
            pl.BlockSpec((1, 128), lambda i, j: (0, 0)),
            pl.BlockSpec((1, 128), lambda i, j: (0, 0)),
            pl.BlockSpec((2, 128, width), lambda i, j: (0, 0, 0)),
        ],
        out_specs=[pl.BlockSpec((1, L, width), fwd), pl.BlockSpec((1, L, width), bwd)],
        out_shape=[jax.ShapeDtypeStruct((b, t, width), F32)] * 2,
        scratch_shapes=[pltpu.VMEM((2, SSD_STATE, width), F32)],
        compiler_params=_params(("parallel", "arbitrary")),
        name="ssd",
    )(xbc, xbc, dt, dt, bias, a_neg, expand)


def _s5_operators(lam_re, lam_im, log_step, b_re, b_im, c_re, c_im):
    nb = S5_BLOCK
    ng, ns = lam_re.shape[1:]
    hp = functools.partial(jnp.einsum, precision=HIGHEST)
    step = jnp.exp(log_step)[..., None]
    k = jnp.arange(nb + 1, dtype=F32)[:, None, None, None]
    mag = jnp.exp(k * (lam_re * step))
    ak_re = mag * jnp.cos(k * (lam_im * step))
    ak_im = mag * jnp.sin(k * (lam_im * step))
    ab_re, ab_im = ak_re[1], ak_im[1]
    den = lam_re * lam_re + lam_im * lam_im
    f_re = ((ab_re - 1.0) * lam_re + ab_im * lam_im) / den
    f_im = (ab_im * lam_re - (ab_re - 1.0) * lam_im) / den
    tr = lambda a: jnp.swapaxes(a, -1, -2)
    bb_re = tr(f_re[..., None] * b_re - f_im[..., None] * b_im)
    bb_im = tr(f_re[..., None] * b_im + f_im[..., None] * b_re)
    akr, aki = ak_re[:, :, :, None, :], ak_im[:, :, :, None, :]
    w_re = akr * bb_re - aki * bb_im
    w_im = akr * bb_im + aki * bb_re
    kern = hp('kdgjp,dgip->kdgji', w_re[:nb], c_re) - hp('kdgjp,dgip->kdgji', w_im[:nb], c_im)
    kf, kb = kern[:, 0], kern[:, 1]
    zall = jnp.concatenate([kb[:0:-1], (kf[0] + kb[0])[None], kf[1:]], axis=0).transpose(1, 2, 0, 3)
    lag = (nb - 1 - jnp.arange(nb)[:, None] + jnp.arange(nb)[None, :]).reshape(-1)
    m = jnp.take(zall, lag, axis=2).reshape(ng, S5_GROUP, nb, nb * S5_GROUP)
    m = m.transpose(0, 2, 1, 3).reshape(ng, nb * S5_GROUP, nb * S5_GROUP)
    inj = lambda w, d, rev: jnp.swapaxes(w[nb - 1::-1, d] if rev else w[:nb, d], 0, 1)
    def readout(d, ks):
        ar, ai = (a[ks, d].transpose(1, 2, 0)[..., None] for a in (ak_re, ak_im))
        cr, ci = (tr(c[d])[:, :, None, :] for c in (c_re, c_im))
        return cr * ar - ci * ai, -(cr * ai + ci * ar)
    of_re, of_im = readout(0, jnp.arange(1, nb + 1))
    ob_re, ob_im = readout(1, jnp.arange(nb, 0, -1))
    odd = (jnp.arange(ng) % 2 == 1)
    pick = lambda a, b: jnp.where(odd.reshape((ng,) + (1,) * (a.ndim - 1)), b, a)
    sf_re, sb_re, sf_im, sb_im = inj(w_re, 0, True), inj(w_re, 1, False), inj(w_im, 0, True), inj(w_im, 1, False)
    smat = jnp.stack([pick(sf_re, sb_re), pick(sb_re, sf_re), pick(sf_im, sb_im), pick(sb_im, sf_im)], axis=3)
    smat = smat.reshape(ng, nb * S5_GROUP, 4 * ns)
    o_all = jnp.stack([pick(of_re, ob_re), pick(ob_re, of_re), pick(of_im, ob_im), pick(ob_im, of_im)], axis=1)
    o_all = o_all.reshape(ng, 4 * ns, nb * S5_GROUP)
    yw = jnp.concatenate([m, o_all], axis=1)
    swap = lambda a: a.reshape(ng // 2, 2, ns)[:, ::-1].reshape(ng * ns)
    dec = jnp.stack([ak_re[nb, 0].reshape(-1), ak_im[nb, 0].reshape(-1), swap(ak_re[nb, 1]), swap(ak_im[nb, 1])])
    return smat.astype(BF16), yw.astype(BF16), dec


def _s5_kernel(u_ref, smat_ref, yw_ref, dec_ref, y_ref, us, uf, ys, sfr, sbr, sfi, sbi, hfr, hbr, hfi, hbi, *,
               n_ctx_blocks):
    nb = S5_BLOCK
    groups = smat_ref.shape[0]
    n = u_ref.shape[1] // nb
    ncb = n_ctx_blocks
    seg = S5_GROUP
    per_col = 128 // seg
    lane_seg = lax.broadcasted_iota(jnp.int32, (n, 128), 1) // seg
    low = lax.broadcasted_iota(jnp.int32, (n, 128), 1) < S5_STATE

    for s in range(nb):
        us[s] = u_ref[0, pl.ds(s, n, stride=nb), :]

    def fold(g):
        for v in range(nb // per_col):
            col = None
            for k in range(per_col):
                x = us[v * per_col + k]
                shift = (seg * (k - g)) % 128
                if shift:
                    x = pltpu.roll(x, shift, 1)
                col = x if col is None else jnp.where(lane_seg == k, x, col)
            uf[g, :, 128 * v:128 * (v + 1)] = col.astype(BF16)
        return jnp.dot(uf[g], smat_ref[g], preferred_element_type=F32)

    for p in range(groups // 2):
        se, so = fold(2 * p), fold(2 * p + 1)
        sl = slice(128 * p, 128 * (p + 1))
        sfr[:, sl] = jnp.where(low, se[:, :128], so[:, :128])
        sbr[:, sl] = jnp.where(low, so[:, :128], se[:, :128])
        sfi[:, sl] = jnp.where(low, se[:, 128:], so[:, 128:])
        sbi[:, sl] = jnp.where(low, so[:, 128:], se[:, 128:])
    arf, aif, arb, aib = dec_ref[0, 0:1], dec_ref[0, 1:2], dec_ref[0, 2:3], dec_ref[0, 3:4]

    def step(i, carry):
        fr, fi, br, bi = carry
        rf = i
        rb = jnp.where(i < ncb, ncb - 1 - i, n + ncb - 1 - i)
        hfr[pl.ds(rf, 1), :] = fr
        hfi[pl.ds(rf, 1), :] = fi
        hbr[pl.ds(rb, 1), :] = br
        hbi[pl.ds(rb, 1), :] = bi
        nfr = arf * fr - aif * fi + sfr[pl.ds(rf, 1), :]
        nfi = arf * fi + aif * fr + sfi[pl.ds(rf, 1), :]
        nbr = arb * br - aib * bi + sbr[pl.ds(rb, 1), :]
        nbi = arb * bi + aib * br + sbi[pl.ds(rb, 1), :]
        return nfr, nfi, nbr, nbi

    zero = jnp.zeros((1, 128 * (groups // 2)), F32)
    lax.fori_loop(0, n, step, (zero, zero, zero, zero))
    for g in range(groups):
        sl = slice(128 * (g // 2), 128 * (g // 2 + 1))
        first, second = ((hfr, hfi), (hbr, hbi)) if g % 2 == 0 else ((hbr, hbi), (hfr, hfi))
        lhs = jnp.concatenate(
            [uf[g]] + [jnp.where(low, a[:, sl], b[:, sl]).astype(BF16) for a, b in zip(first, second)], axis=1)
        ys[g] = jnp.dot(lhs, yw_ref[g], preferred_element_type=F32)
    for l in range(nb):
        v, k = divmod(l, per_col)
        out = None
        for g in range(groups):
            x = ys[g, :, 128 * v:128 * (v + 1)]
            shift = (seg * (g - k)) % 128
            if shift:
                x = pltpu.roll(x, shift, 1)
            out = x if out is None else jnp.where(lane_seg == g, x, out)
        y_ref[0, pl.ds(l, n, stride=nb), :] = out


def _s5(u, ops, n_ctx_blocks):
    smat, yw, dec = ops
    b, t, w = u.shape
    nb = S5_BLOCK
    n = t // nb
    gw = nb * S5_GROUP
    gpb = 128 // S5_GROUP
    steps = w // 128
    lanes = gpb * S5_STATE
    dec = dec.reshape(4, steps, lanes).transpose(1, 0, 2)
    dec = jnp.concatenate([dec, jnp.zeros_like(dec)], axis=1)
    return pl.pallas_call(
        functools.partial(_s5_kernel, n_ctx_blocks=n_ctx_blocks),
        grid=(b, steps),
        in_specs=[
            pl.BlockSpec((1, t, 128), lambda i, j: (i, 0, j)),
            pl.BlockSpec((gpb,) + smat.shape[1:], lambda i, j: (j, 0, 0)),
            pl.BlockSpec((gpb,) + yw.shape[1:], lambda i, j: (j, 0, 0)),
            pl.BlockSpec((1, 8, lanes), lambda i, j: (j, 0, 0)),
        ],
        out_specs=pl.BlockSpec((1, t, 128), lambda i, j: (i, 0, j)),
        out_shape=jax.ShapeDtypeStruct((b, t, w), F32),
        scratch_shapes=[pltpu.VMEM((nb, n, 128), F32), pltpu.VMEM((gpb, n, gw), BF16), pltpu.VMEM((gpb, n, gw), F32)]
        + [pltpu.VMEM((n, lanes), F32)] * 8,
        compiler_params=_params(("parallel", "parallel")),
        name="s5",
    )(u, smat, yw, dec)


def _even_out_kernel(yf_ref, yb_ref, xs_ref, z_ref, y5_ref, u_ref, g_ref, h_ref, mod_ref, vs_ref, v5_ref,
                     glu_ref, w_ref, o_ref):
    ws = z_ref.shape[-1]
    y = _gelu_tanh(y5_ref[0] + v5_ref[0:1] * u_ref[0])
    glu = jnp.dot(y.astype(BF16), glu_ref[...], preferred_element_type=F32)
    ys = yf_ref[0] + yb_ref[0] + vs_ref[1:2] * xs_ref[0]
    s = _rms(ys * _silu(z_ref[0])) * vs_ref[0:1]
    o = jnp.dot(s.astype(BF16), w_ref[0:ws], preferred_element_type=F32)
    y = y * _sigmoid(glu + v5_ref[1:2]) * _silu(g_ref[0])
    o = o + jnp.dot(y.astype(BF16), w_ref[ws:], preferred_element_type=F32)
    o_ref[0] = h_ref[0] + mod_ref[0, 0][2:3] * o


def _even_out(yf, yb, xbc, z, y5, u, g, h, mod, ssd_norm, d_ssd, d_s5, glu_w, glu_b, w_out, n_ctx_tiles):
    b, t, d = h.shape
    tm = TOKEN_TILE
    ws, w5 = z.shape[-1], u.shape[-1]
    vs = jnp.pad(jnp.stack([ssd_norm, jnp.repeat(d_ssd, HEAD_DIM)]), ((0, 6), (0, 0)))
    v5 = jnp.pad(jnp.stack([d_s5, glu_b]), ((0, 6), (0, 0)))
    tok = lambda n: pl.BlockSpec((1, tm, n), lambda i, j: (i, j, 0))
    const = lambda a: pl.BlockSpec(a.shape, lambda i, j: (0,) * a.ndim)
    glu_w = glu_w.astype(BF16)
    w_out = w_out.astype(BF16)
    return pl.pallas_call(
        _even_out_kernel,
        grid=(b, t // tm),
        in_specs=[
            tok(ws), tok(ws), tok(ws), tok(ws), tok(w5), tok(w5), tok(w5), tok(d),
            pl.BlockSpec((1, 1, 3, d), lambda i, j: (i, (j >= n_ctx_tiles).astype(jnp.int32), 0, 0)),
            const(vs), const(v5), const(glu_w), const(w_out),
        ],
        out_specs=tok(d),
        out_shape=jax.ShapeDtypeStruct((b, t, d), F32),
        compiler_params=_params(("parallel", "parallel")),
        name="even_out",
    )(yf, yb, xbc, z, y5, u, g, h, mod, vs, v5, glu_w, w_out)


def _odd_in_kernel(h_ref, mod_ref, w_ref, qg_ref, kg_ref, cos_ref, sin_ref, ones_ref, q_ref, k_ref, v_ref, g_ref, *,
                   q_w, kv_w):
    tm = TOKEN_TILE
    a = _norm_mod(h_ref[0], mod_ref[0, 0]).astype(BF16)
    cosv = cos_ref[...]
    sinv = sin_ref[...]
    first_half = (lax.broadcasted_iota(jnp.int32, (tm, 128), 1) % (HEAD_DIM // 2)) < (HEAD_DIM // 4)

    def project(lo, width=256):
        return jnp.dot(a, w_ref[:, lo:lo + width], preferred_element_type=F32)

    def head_norm_rope(x, gain, out_ref, c, transposed):
        ms = jnp.dot((x * x).astype(BF16), ones_ref[...], preferred_element_type=F32) * (1.0 / HEAD_DIM)
        xn = x * lax.rsqrt(ms + NORM_EPS) * gain
        for s in range(2):
            xb = xn[:, 128 * s:128 * (s + 1)]
            partner = jnp.where(first_half, pltpu.roll(xb, 128 - HEAD_DIM // 4, 1), pltpu.roll(xb, HEAD_DIM // 4, 1))
            r = xb * cosv + partner * sinv
            head = 4 * c + 2 * s
            if transposed:
                rt = r.T
                out_ref[0, head] = rt[:HEAD_DIM].astype(out_ref.dtype)
                out_ref[0, head + 1] = rt[HEAD_DIM:].astype(out_ref.dtype)
            else:
                out_ref[0, head] = r[:, :HEAD_DIM].astype(out_ref.dtype)
                out_ref[0, head + 1] = pltpu.roll(r, HEAD_DIM, 1)[:, :HEAD_DIM].astype(out_ref.dtype)

    chunks = [(256 * c, qg_ref, q_ref, c) for c in range(q_w // 256)] + \
             [(q_w + 256 * c, kg_ref, k_ref, c) for c in range(kv_w // 256)]
    g_lo = q_w + 2 * kv_w
    g_w = w_ref.shape[1] - g_lo
    x_next = project(chunks[0][0])
    for i, (lo, gain_ref, out_ref, c) in enumerate(chunks):
        x = x_next
        if i + 1 < len(chunks):
            x_next = project(chunks[i + 1][0])
        if 256 * i < g_w:
            g_ref[0, :, 256 * i:256 * (i + 1)] = project(g_lo + 256 * i)
        head_norm_rope(x, gain_ref[:, 256 * c:256 * (c + 1)], out_ref, c, out_ref is q_ref)
    assert 256 * len(chunks) >= g_w
    lane = lax.broadcasted_iota(jnp.int32, (tm, 128), 1)
    vt_rows = v_ref.shape[2]
    for c in range(kv_w // 128):
        x = project(q_w + kv_w + 128 * c, 128)
        for s in range(2):
            xs = pltpu.roll(x, HEAD_DIM, 1) if s else x
            vh = jnp.where(lane < HEAD_DIM, xs, jnp.where(lane == HEAD_DIM, 1.0, 0.0))
            v_ref[0, 2 * c + s] = vh.T[:vt_rows].astype(v_ref.dtype)


def _rope_tables(n_ctx, n_lat):
    pairs = HEAD_DIM // 4
    pos = jnp.arange(n_lat)
    row = (pos // GRID_W).astype(F32)
    colp = (pos % GRID_W).astype(F32)
    inv = ROPE_THETA ** (-jnp.arange(pairs, dtype=F32) / pairs)
    lane = jnp.arange(128) % HEAD_DIM
    axis_is_col = (lane // (HEAD_DIM // 2)) == 1
    ang = jnp.where(axis_is_col[None, :], colp[:, None], row[:, None]) * inv[lane % pairs][None, :]
    sign = jnp.where((lane % (HEAD_DIM // 2)) < pairs, -1.0, 1.0)
    cos = jnp.concatenate([jnp.ones((n_ctx, 128), F32), jnp.cos(ang)], axis=0)
    sin = jnp.concatenate([jnp.zeros((n_ctx, 128), F32), jnp.sin(ang) * sign[None, :]], axis=0)
    return cos, sin


def _odd_in(h, mod, w, q_gain, k_gain, cos, sin, n_ctx_tiles):
    b, t, d = h.shape
    tm = TOKEN_TILE
    q_w = ATTN_Q_HEADS * HEAD_DIM
    kv_w = ATTN_KV_HEADS * HEAD_DIM
    qg = (jnp.tile(q_gain, ATTN_Q_HEADS) * (HEAD_DIM ** -0.5 * math.log2(math.e))).reshape(1, q_w)
    kg = jnp.tile(k_gain, ATTN_KV_HEADS).reshape(1, kv_w)
    blk = jnp.arange(256) // HEAD_DIM
    ones = (blk[:, None] == blk[None, :]).astype(BF16)
    tok = lambda n: pl.BlockSpec((1, tm, n), lambda i, j: (i, j, 0))
    heads = lambda nh, n: pl.BlockSpec((1, nh, tm, n), lambda i, j: (i, 0, j, 0))
    heads_t = lambda nh, n: pl.BlockSpec((1, nh, n, tm), lambda i, j: (i, 0, 0, j))
    const = lambda a: pl.BlockSpec(a.shape, lambda i, j: (0,) * a.ndim)
    return pl.pallas_call(
        functools.partial(_odd_in_kernel, q_w=q_w, kv_w=kv_w),
        grid=(b, t // tm),
        in_specs=[
            tok(d),
            pl.BlockSpec((1, 1, 3, d), lambda i, j: (i, (j >= n_ctx_tiles).astype(jnp.int32), 0, 0)),
            const(w), const(qg), const(kg),
            pl.BlockSpec((tm, 128), lambda i, j: (j, 0)),
            pl.BlockSpec((tm, 128), lambda i, j: (j, 0)),
            const(ones),
        ],
        out_specs=[heads_t(ATTN_Q_HEADS, HEAD_DIM), heads(ATTN_KV_HEADS, HEAD_DIM), heads_t(ATTN_KV_HEADS, VT_ROWS),
                   tok(q_w)],
        out_shape=[
            jax.ShapeDtypeStruct((b, ATTN_Q_HEADS, HEAD_DIM, t), BF16),
            jax.ShapeDtypeStruct((b, ATTN_KV_HEADS, t, HEAD_DIM), BF16),
            jax.ShapeDtypeStruct((b, ATTN_KV_HEADS, VT_ROWS, t), BF16),
            jax.ShapeDtypeStruct((b, t, q_w), F32),
        ],
        compiler_params=_params(("parallel", "parallel")),
        name="odd_in",
    )(h, mod, w, qg, kg, cos, sin, ones)


def _attn_kernel(q_ref, k_ref, v_ref, o_ref, *, tk):
    rep, hd, tq = q_ref.shape[1:]
    rows = v_ref.shape[2]
    nk = k_ref.shape[2] // tk
    m = [jnp.full((1, tq), -1e30, F32)] * rep
    acc = [jnp.zeros((rows, tq), F32)] * rep
    blocks = [(j, r) for j in range(nk) for r in range(rep)]
    scores = {}
    for i in range(len(blocks) + ATTN_LOOKAHEAD):
        if i < len(blocks):
            j, r = blocks[i]
            scores[i] = jnp.dot(k_ref[0, 0, j * tk:(j + 1) * tk, :], q_ref[0, r], preferred_element_type=F32)
        if i >= ATTN_LOOKAHEAD:
            j, r = blocks[i - ATTN_LOOKAHEAD]
            s = scores.pop(i - ATTN_LOOKAHEAD)
            m_new = jnp.maximum(m[r], jnp.max(s, axis=0, keepdims=True))
            p = jnp.exp2(s - m_new).astype(BF16)
            pv = jnp.dot(v_ref[0, 0, :, j * tk:(j + 1) * tk], p, preferred_element_type=F32)
            acc[r] = jnp.exp2(m[r] - m_new) * acc[r] + pv
            m[r] = m_new
    for c in range(rep // 2):
        pair = [acc[r][:hd] * (1.0 / acc[r][hd:hd + 1]) for r in (2 * c, 2 * c + 1)]
        o_ref[0, :, 2 * hd * c:2 * hd * (c + 1)] = jnp.concatenate(pair, axis=0).T.astype(o_ref.dtype)


def _attention(q, k, v, n_ctx):
    b, hq, hd, t = q.shape
    hkv = k.shape[1]
    rep = hq // hkv
    tq = 256
    tk = next(c for c in (256, 128) if t % c == 0)
    assert n_ctx % tq == 0
    return pl.pallas_call(
        functools.partial(_attn_kernel, tk=tk),
        grid=(b, hkv, (t - n_ctx) // tq),
        in_specs=[
            pl.BlockSpec((1, rep, hd, tq), lambda i, j, n: (i, j, 0, n + n_ctx // tq)),
            pl.BlockSpec((1, 1, t, hd), lambda i, j, n: (i, j, 0, 0)),
            pl.BlockSpec((1, 1, v.shape[2], t), lambda i, j, n: (i, j, 0, 0)),
        ],
        out_specs=pl.BlockSpec((1, tq, rep * hd), lambda i, j, n: (i, n, j)),
        out_shape=jax.ShapeDtypeStruct((b, t - n_ctx, hq * hd), BF16),
        compiler_params=_params(("parallel", "parallel", "parallel")),
        name="attention",
    )(q, k, v)


def _attn_out_kernel(o_ref, g_ref, h_ref, mod_ref, w_ref, fg_ref, out_ref):
    x = o_ref[0].astype(F32) * _silu(g_ref[0])
    y = jnp.dot(x.astype(BF16), w_ref[...], preferred_element_type=F32)
    hn = h_ref[0] + mod_ref[0, 0][2:3] * y
    out_ref[0] = _rms(hn) * fg_ref[...]


def _attn_out(o, g, h, mod, w_out, final_gain, n_ctx_tiles):
    b, n_lat, d = o.shape
    tm = TOKEN_TILE
    lat = lambda n: pl.BlockSpec((1, tm, n), lambda i, j: (i, j + n_ctx_tiles, 0))
    w_out = w_out.astype(BF16)
    return pl.pallas_call(
        _attn_out_kernel,
        grid=(b, n_lat // tm),
        in_specs=[
            pl.BlockSpec((1, tm, d), lambda i, j: (i, j, 0)),
            lat(d), lat(d),
            pl.BlockSpec((1, 1, 3, d), lambda i, j: (i, 1, 0, 0)),
            pl.BlockSpec(w_out.shape, lambda i, j: (0, 0)),
            pl.BlockSpec((1, d), lambda i, j: (0, 0)),
        ],
        out_specs=pl.BlockSpec((1, tm, d), lambda i, j: (i, j, 0)),
        out_shape=jax.ShapeDtypeStruct((b, n_lat, d), F32),
        compiler_params=_params(("parallel", "parallel")),
        name="attn_out",
    )(o, g, h, mod, w_out, final_gain.reshape(1, d))


def kernel(x, c, ctx, c_ctx, ada_w, ada_b, ev_w_in, ev_conv_w, ev_conv_b, ev_dt_bias, ev_a_log, ev_d_ssd, ev_ssd_norm, ev_lam_re, ev_lam_im, ev_log_step, ev_b_re, ev_b_im, ev_c_re, ev_c_im, ev_d_s5, ev_glu_w, ev_glu_b, ev_w_out, od_w_in, od_q_gain, od_k_gain, od_w_out, final_gain):
    b, n_lat, d = x.shape
    n_ctx = ctx.shape[1]
    assert ada_w.shape[0] == 2 and n_ctx % TOKEN_TILE == 0 and n_lat % TOKEN_TILE == 0
    n_ctx_tiles = n_ctx // TOKEN_TILE
    mods = _adaln(c, c_ctx, ada_w, ada_b)
    h = jnp.concatenate([ctx, x], axis=1)

    ws = SSD_HEADS * HEAD_DIM
    wx = ws + 2 * SSD_GROUPS * SSD_STATE
    w5 = ev_d_s5.shape[-1]
    w = ev_w_in[0]
    cuts = (ws, ws + wx, ws + wx + 2 * SSD_HEADS, ws + wx + 2 * SSD_HEADS + w5)
    w_dt = jnp.pad(w[:, cuts[1]:cuts[2]], ((0, 0), (0, 128 - 2 * SSD_HEADS)))
    w_dt_hi, w_dt_lo = _split_bf16(w_dt)
    w_cat = jnp.concatenate(
        [w[:, :cuts[1]].astype(BF16), w[:, cuts[2]:].astype(BF16), w_dt_hi, w_dt_lo], axis=1)
    z, xbc, u, g, dt = _even_in(h, mods[0], w_cat, n_ctx_tiles, (ws, wx, w5, w5))
    xbc = _conv(xbc, ev_conv_w[0], ev_conv_b[0], n_ctx_tiles)
    yf, yb = _ssd(xbc, dt, ev_dt_bias[0], ev_a_log[0], n_ctx)
    ops = _s5_operators(ev_lam_re[0], ev_lam_im[0], ev_log_step[0], ev_b_re[0], ev_b_im[0], ev_c_re[0], ev_c_im[0])
    y5 = _s5(u, ops, n_ctx // S5_BLOCK)
    h = _even_out(yf, yb, xbc, z, y5, u, g, h, mods[0], ev_ssd_norm[0], ev_d_ssd[0], ev_d_s5[0], ev_glu_w[0],
                  ev_glu_b[0], ev_w_out[0], n_ctx_tiles)

    cos, sin = _rope_tables(n_ctx, n_lat)
    q, k, v, g = _odd_in(h, mods[1], od_w_in[0].astype(BF16), od_q_gain[0], od_k_gain[0], cos, sin, n_ctx_tiles)
    o = _attention(q, k, v, n_ctx)
    return _attn_out(o, g, h, mods[1], od_w_out[0], final_gain, n_ctx_tiles)
```

```python
import functools
import math

import jax
import jax.numpy as jnp
from jax import lax
from jax.experimental import pallas as pl
from jax.experimental.pallas import tpu as pltpu

F32 = jnp.float32
BF16 = jnp.bfloat16
HIGHEST = lax.Precision.HIGHEST

NORM_EPS = 1e-6
GRID_W = 64
ROPE_THETA = 10000.0

HEAD_DIM = 64
SSD_HEADS = 16
SSD_GROUPS = 4
SSD_STATE = 128
SSD_CHUNK = 128
SSD_CONV = 5
S5_GROUP = 16
S5_STATE = 64
S5_BLOCK = 16
ATTN_Q_HEADS = 16
ATTN_KV_HEADS = 4
VT_ROWS = 80
ATTN_LOOKAHEAD = 8

TOKEN_TILE = 256
HALO_ROWS = 8
VMEM_LIMIT = 56 << 20


def _params(semantics):
    return pltpu.CompilerParams(dimension_semantics=semantics, vmem_limit_bytes=VMEM_LIMIT)


def _sigmoid(x):
    return 1.0 / (1.0 + jnp.exp(-x))


def _silu(x):
    return x * _sigmoid(x)


def _softplus(x):
    return jnp.maximum(x, 0.0) + jnp.log(1.0 + jnp.exp(-jnp.abs(x)))


def _gelu_tanh(x):
    return 0.5 * x * (1.0 + jnp.tanh(math.sqrt(2.0 / math.pi) * (x + 0.044715 * (x * x * x))))


def _rms(x):
    return x * lax.rsqrt(jnp.mean(x * x, axis=-1, keepdims=True) + NORM_EPS)


def _norm_mod(h, mod):
    return _rms(h) * (1.0 + mod[1:2]) + mod[0:1]


def _split_bf16(x):
    hi = x.astype(BF16)
    return hi, (x - hi.astype(F32)).astype(BF16)


def _adaln_kernel(s_ref, w_ref, b_ref, o_ref):
    s = _silu(s_ref[...])
    o_ref[0] = jnp.dot(s, w_ref[0], preferred_element_type=F32, precision=HIGHEST) + b_ref[0]


def _adaln(c, c_ctx, ada_w, ada_b):
    depth, d, d3 = ada_w.shape
    b = c.shape[0]
    assert b < 8
    rows = jnp.concatenate([c, c_ctx[None], jnp.zeros((7 - b, d), F32)], axis=0)
    out = pl.pallas_call(
        _adaln_kernel,
        grid=(depth, d3 // d),
        in_specs=[
            pl.BlockSpec((8, d), lambda i, j: (0, 0)),
            pl.BlockSpec((1, d, d), lambda i, j: (i, 0, j)),
            pl.BlockSpec((1, 1, d), lambda i, j: (i, 0, j)),
        ],
        out_specs=pl.BlockSpec((1, 8, d), lambda i, j: (i, 0, j)),
        out_shape=jax.ShapeDtypeStruct((depth, 8, d3), F32),
        compiler_params=_params(("arbitrary", "arbitrary")),
        name="adaln",
    )(rows, ada_w, ada_b.reshape(depth, 1, d3))
    m = out.reshape(depth, 8, 3, d)
    lat = m[:, :b]
    ctx = jnp.broadcast_to(m[:, b:b + 1], lat.shape)
    return jnp.stack([ctx, lat], axis=2)


def _stream_tile(ctx_ref, x_ref, n_ctx_tiles):
    return jnp.where(pl.program_id(1) < n_ctx_tiles, ctx_ref[0], x_ref[0])


def _stream_specs(tm, d, n_ctx_tiles):
    return [
        pl.BlockSpec((1, tm, d), lambda i, j: (i, jnp.minimum(j, n_ctx_tiles - 1), 0)),
        pl.BlockSpec((1, tm, d), lambda i, j: (i, jnp.maximum(j - n_ctx_tiles, 0), 0)),
    ]


def _even_in_kernel(ctx_ref, x_ref, mod_ref, w_ref, z_ref, xbc_ref, u_ref, g_ref, dt_ref, *, cuts, n_ctx_tiles):
    a = _norm_mod(_stream_tile(ctx_ref, x_ref, n_ctx_tiles), mod_ref[0, 0])
    a_hi, a_lo = _split_bf16(a)

    def mm(x, lo, hi):
        return jnp.dot(x, w_ref[:, lo:hi], preferred_element_type=F32)

    c0, c1, c2, c3, c4, c5 = cuts
    z_ref[0] = mm(a_hi, 0, c0).astype(z_ref.dtype)
    xbc_ref[0] = mm(a_hi, c0, c1)
    u_ref[0] = mm(a_hi, c1, c2)
    g_ref[0] = mm(a_hi, c2, c3).astype(g_ref.dtype)
    dt_ref[0] = mm(a_hi, c3, c4) + mm(a_lo, c3, c4) + mm(a_hi, c4, c5)


def _even_in(ctx, x, mod, w, n_ctx_tiles, widths):
    b, n_lat, d = x.shape
    t = ctx.shape[1] + n_lat
    tm = TOKEN_TILE
    wz, wx, wu, wg = widths
    cuts = (wz, wz + wx, wz + wx + wu, wz + wx + wu + wg, wz + wx + wu + wg + 128, wz + wx + wu + wg + 256)
    assert w.shape == (d, cuts[-1])
    tok = lambda n: pl.BlockSpec((1, tm, n), lambda i, j: (i, j, 0))
    return pl.pallas_call(
        functools.partial(_even_in_kernel, cuts=cuts, n_ctx_tiles=n_ctx_tiles),
        grid=(b, t // tm),
        in_specs=_stream_specs(tm, d, n_ctx_tiles) + [
            pl.BlockSpec((1, 1, 3, d), lambda i, j: (i, (j >= n_ctx_tiles).astype(jnp.int32), 0, 0)),
            pl.BlockSpec(w.shape, lambda i, j: (0, 0)),
        ],
        out_specs=[tok(wz), tok(wx), tok(wu), tok(wg), tok(128)],
        out_shape=[
            jax.ShapeDtypeStruct((b, t, wz), BF16),
            jax.ShapeDtypeStruct((b, t, wx), F32),
            jax.ShapeDtypeStruct((b, t, wu), F32),
            jax.ShapeDtypeStruct((b, t, wg), BF16),
            jax.ShapeDtypeStruct((b, t, 128), F32),
        ],
        compiler_params=_params(("parallel", "parallel")),
        name="even_in",
    )(ctx, x, mod, w)


def _conv_kernel(prev_ref, main_ref, next_ref, w_ref, b_ref, o_ref, xe_ref, *, n_ctx_tiles, n_tiles):
    tm = TOKEN_TILE
    ch = o_ref.shape[-1]
    t = pl.program_id(1)
    has_prev = jnp.logical_and(t != 0, t != n_ctx_tiles)
    has_next = jnp.logical_and(t != n_ctx_tiles - 1, t != n_tiles - 1)
    xe_ref[0:HALO_ROWS] = jnp.where(has_prev, prev_ref[0], 0.0)
    xe_ref[HALO_ROWS:HALO_ROWS + tm] = main_ref[0]
    xe_ref[HALO_ROWS + tm:2 * HALO_ROWS + tm] = jnp.where(has_next, next_ref[0], 0.0)
    rows = tm + 2 * HALO_ROWS
    half = SSD_CONV // 2
    lanes = 128
    for c0 in range(0, ch, lanes):
        cs = slice(c0, c0 + lanes)
        xe = xe_ref[:, cs]
        acc = b_ref[:, cs] + w_ref[half:half + 1, cs] * xe[HALO_ROWS:HALO_ROWS + tm]
        for k in range(SSD_CONV):
            if k != half:
                shifted = pltpu.roll(xe, (half - k) % rows, 0)
                acc = acc + w_ref[k:k + 1, cs] * shifted[HALO_ROWS:HALO_ROWS + tm]
        o_ref[0, :, cs] = _silu(acc)


def _conv(xbc, conv_w, conv_b, n_ctx_tiles):
    b, t, ch = xbc.shape
    tm = TOKEN_TILE
    n_tiles = t // tm
    per = tm // HALO_ROWS
    last = t // HALO_ROWS - 1
    w = jnp.pad(conv_w.T, ((0, 8 - SSD_CONV), (0, 0)))
    return pl.pallas_call(
        functools.partial(_conv_kernel, n_ctx_tiles=n_ctx_tiles, n_tiles=n_tiles),
        grid=(b, n_tiles),
        in_specs=[
            pl.BlockSpec((1, HALO_ROWS, ch), lambda i, j: (i, jnp.maximum(j * per - 1, 0), 0)),
            pl.BlockSpec((1, tm, ch), lambda i, j: (i, j, 0)),
            pl.BlockSpec((1, HALO_ROWS, ch), lambda i, j: (i, jnp.minimum((j + 1) * per, last), 0)),
            pl.BlockSpec((8, ch), lambda i, j: (0, 0)),
            pl.BlockSpec((1, ch), lambda i, j: (0, 0)),
        ],
        out_specs=pl.BlockSpec((1, tm, ch), lambda i, j: (i, j, 0)),
        out_shape=jax.ShapeDtypeStruct((b, t, ch), F32),
        scratch_shapes=[pltpu.VMEM((tm + 2 * HALO_ROWS, ch), F32)],
        compiler_params=_params(("parallel", "parallel")),
        name="conv",
    )(xbc, xbc, xbc, w, conv_b.reshape(1, ch))


def _ssd_kernel(xf_ref, xb_ref, dtf_ref, dtb_ref, bias_ref, a_ref, e_ref, yf_ref, yb_ref, st_ref):
    L = SSD_CHUNK
    width = SSD_HEADS * HEAD_DIM
    gw = width // SSD_GROUPS
    hpg = SSD_HEADS // SSD_GROUPS

    @pl.when(pl.program_id(1) == 0)
    def _():
        st_ref[...] = jnp.zeros_like(st_ref)

    row = lax.broadcasted_iota(jnp.int32, (L, L), 0)
    col = lax.broadcasted_iota(jnp.int32, (L, L), 1)
    lane_head = lax.broadcasted_iota(jnp.int32, (L, gw), 1) // HEAD_DIM
    sub16 = lax.broadcasted_iota(jnp.int32, (16, 128), 0)

    def split3(x):
        x1 = x.astype(BF16)
        r = x - x1.astype(F32)
        x2 = r.astype(BF16)
        return x1, x2, (r - x2.astype(F32)).astype(BF16)

    def chunk(d, x_ref, dt_ref, y_ref, r0):
        rows = pl.ds(r0, L)
        mask = (row >= col) if d == 0 else (row <= col)
        dtv = _softplus(dt_ref[0, rows, :] + bias_ref[...])
        loga = dtv * a_ref[...]
        tri = jnp.where(mask, 1.0, 0.0).astype(BF16)
        cs = sum(jnp.dot(tri, part, preferred_element_type=F32) for part in split3(loga))
        cs_t = cs.T
        total = jnp.sum(loga, axis=0, keepdims=True)
        t1, t2, t3 = (part.astype(F32) for part in split3(total))
        tot = jnp.where(sub16 == 0, t1, jnp.where(sub16 == 1, t2, jnp.where(sub16 == 2, t3, 0.0)))
        stack = jnp.concatenate(
            [dtv.astype(BF16), jnp.exp(cs).astype(BF16), jnp.exp(total - cs).astype(BF16), tot.astype(BF16)], axis=0)
        ex = jnp.dot(stack, e_ref[d], preferred_element_type=F32)
        dt_x = ex[0:L]
        ecs_x = ex[L:2 * L]
        wend_x = ex[2 * L:3 * L]
        edec_x = jnp.exp(ex[3 * L:3 * L + 1] + ex[3 * L + 1:3 * L + 2] + ex[3 * L + 2:3 * L + 3])

        for g in range(SSD_GROUPS):
            sl = slice(g * gw, (g + 1) * gw)
            xs_g = x_ref[0, rows, sl]
            b_g = x_ref[0, rows, width + g * SSD_STATE:width + (g + 1) * SSD_STATE]
            c_g = x_ref[0, rows, width + (SSD_GROUPS + g) * SSD_STATE:width + (SSD_GROUPS + g + 1) * SSD_STATE]
            cb = c_g.astype(BF16)
            xd = xs_g * dt_x[:, sl]
            gram = lax.dot_general(cb, b_g.astype(BF16), (((1,), (1,)), ((), ())), preferred_element_type=F32)
            scores = []
            for hh in range(hpg):
                li = SSD_HEADS * d + hpg * g + hh
                seg = cs[:, li:li + 1] - cs_t[li:li + 1, :]
                scores.append((gram * jnp.where(mask, jnp.exp(seg), 0.0)).astype(BF16))
            scores = jnp.concatenate(scores, axis=1)
            xd_blocks = jnp.concatenate(
                [jnp.where(lane_head == hh, xd, 0.0).astype(BF16) for hh in range(hpg)], axis=0)
            y = jnp.dot(scores, xd_blocks, preferred_element_type=F32)
            st = st_ref[d, :, sl]
            y = y + jnp.dot(cb, st.astype(BF16), preferred_element_type=F32) * ecs_x[:, sl]
            y_ref[0, rows, sl] = y.astype(y_ref.dtype)
            xdw = (xd * wend_x[:, sl]).astype(BF16)
            st_ref[d, :, sl] = st * edec_x[:, sl] + jnp.dot(b_g.T.astype(BF16), xdw, preferred_element_type=F32)

    chunk(0, xf_ref, dtf_ref, yf_ref, 0)
    chunk(1, xb_ref, dtb_ref, yb_ref, L)
    chunk(0, xf_ref, dtf_ref, yf_ref, L)
    chunk(1, xb_ref, dtb_ref, yb_ref, 0)


def _ssd(xbc, dt, dt_bias, a_log, n_ctx):
    b, t, ch = xbc.shape
    L = 2 * SSD_CHUNK
    assert n_ctx % L == 0 and t % L == 0
    n = t // L
    width = SSD_HEADS * HEAD_DIM
    ncc = n_ctx // L

    def fwd(i, j):
        return (i, j, 0)

    def bwd(i, j):
        return (i, jnp.where(j < ncc, ncc - 1 - j, n + ncc - 1 - j), 0)

    pad = lambda v: jnp.pad(v.reshape(1, -1), ((0, 0), (0, 128 - v.size)))
    bias = pad(dt_bias)
    a_neg = pad(-jnp.exp(a_log))
    head_of_lane = jnp.arange(width) // HEAD_DIM
    expand = jnp.stack([
        (jnp.arange(128)[:, None] == SSD_HEADS * d + head_of_lane[None, :]) for d in range(2)
    ]).astype(BF16)
    return pl.pallas_call(
        _ssd_kernel,
        grid=(b, n),
        in_specs=[
            pl.BlockSpec((1, L, ch), fwd),
            pl.BlockSpec((1, L, ch), bwd),
            pl.BlockSpec((1, L, 128), fwd),
            pl.BlockSpec((1, L, 128), bwd),
            pl.BlockSpec((1, 128), lambda i, j: (0, 0)),
            pl.BlockSpec((1, 128), lambda i, j: (0, 0)),
            pl.BlockSpec((2, 128, width), lambda i, j: (0, 0, 0)),
        ],
        out_specs=[pl.BlockSpec((1, L, width), fwd), pl.BlockSpec((1, L, width), bwd)],
        out_shape=[jax.ShapeDtypeStruct((b, t, width), BF16)] * 2,
        scratch_shapes=[pltpu.VMEM((2, SSD_STATE, width), F32)],
        compiler_params=_params(("parallel", "arbitrary")),
        name="ssd",
    )(xbc, xbc, dt, dt, bias, a_neg, expand)


def _toeplitz_kernel(z_ref, m_ref):
    nb, width = S5_BLOCK, S5_BLOCK * S5_GROUP
    lanes = z_ref.shape[-1]
    for g in range(z_ref.shape[0]):
        z = z_ref[g]
        for s in range(nb):
            off = (nb - 1 - s) * S5_GROUP
            win = pltpu.roll(z, (lanes - off) % lanes, 1) if off else z
            m_ref[g, s * S5_GROUP:(s + 1) * S5_GROUP, :] = win[:, :width].astype(m_ref.dtype)


def _toeplitz(zall):
    ng, nj, w = zall.shape
    lanes = -(-w // 128) * 128
    per = 8
    width = S5_BLOCK * S5_GROUP
    return pl.pallas_call(
        _toeplitz_kernel,
        grid=(ng // per,),
        in_specs=[pl.BlockSpec((per, nj, lanes), lambda i: (i, 0, 0))],
        out_specs=pl.BlockSpec((per, S5_BLOCK * nj, width), lambda i: (i, 0, 0)),
        out_shape=jax.ShapeDtypeStruct((ng, S5_BLOCK * nj, width), BF16),
        compiler_params=_params(("parallel",)),
        name="s5_toeplitz",
    )(jnp.pad(zall, ((0, 0), (0, 0), (0, lanes - w))))


def _s5_operators(lam_re, lam_im, log_step, b_re, b_im, c_re, c_im):
    nb = S5_BLOCK
    ng, ns = lam_re.shape[1:]
    hp = functools.partial(jnp.einsum, precision=HIGHEST)
    step = jnp.exp(log_step)[..., None]
    k = jnp.arange(nb + 1, dtype=F32)[:, None, None, None]
    mag = jnp.exp(k * (lam_re * step))
    ak_re = mag * jnp.cos(k * (lam_im * step))
    ak_im = mag * jnp.sin(k * (lam_im * step))
    ab_re, ab_im = ak_re[1], ak_im[1]
    den = lam_re * lam_re + lam_im * lam_im
    f_re = ((ab_re - 1.0) * lam_re + ab_im * lam_im) / den
    f_im = (ab_im * lam_re - (ab_re - 1.0) * lam_im) / den
    tr = lambda a: jnp.swapaxes(a, -1, -2)
    bb_re = tr(f_re[..., None] * b_re - f_im[..., None] * b_im)
    bb_im = tr(f_re[..., None] * b_im + f_im[..., None] * b_re)
    akr, aki = ak_re[:, :, :, None, :], ak_im[:, :, :, None, :]
    w_re = akr * bb_re - aki * bb_im
    w_im = akr * bb_im + aki * bb_re
    kern = hp('kdgjp,dgip->kdgji', w_re[:nb], c_re) - hp('kdgjp,dgip->kdgji', w_im[:nb], c_im)
    kf, kb = kern[:, 0], kern[:, 1]
    zall = jnp.concatenate([kb[:0:-1], (kf[0] + kb[0])[None], kf[1:]], axis=0).transpose(1, 2, 0, 3)
    m = _toeplitz(zall.reshape(ng, S5_GROUP, (2 * nb - 1) * S5_GROUP))
    inj = lambda w, d, rev: jnp.swapaxes(w[nb - 1::-1, d] if rev else w[:nb, d], 0, 1)
    def readout(d, ks):
        ar, ai = (a[ks, d].transpose(1, 2, 0)[..., None] for a in (ak_re, ak_im))
        cr, ci = (tr(c[d])[:, :, None, :] for c in (c_re, c_im))
        return cr * ar - ci * ai, -(cr * ai + ci * ar)
    of_re, of_im = readout(0, jnp.arange(1, nb + 1))
    ob_re, ob_im = readout(1, jnp.arange(nb, 0, -1))
    odd = (jnp.arange(ng) % 2 == 1)
    pick = lambda a, b: jnp.where(odd.reshape((ng,) + (1,) * (a.ndim - 1)), b, a)
    sf_re, sb_re, sf_im, sb_im = inj(w_re, 0, True), inj(w_re, 1, False), inj(w_im, 0, True), inj(w_im, 1, False)
    smat = jnp.stack([pick(sf_re, sb_re), pick(sb_re, sf_re), pick(sf_im, sb_im), pick(sb_im, sf_im)], axis=3)
    smat = smat.reshape(ng, nb * S5_GROUP, 4 * ns)
    o_all = jnp.stack([pick(of_re, ob_re), pick(ob_re, of_re), pick(of_im, ob_im), pick(ob_im, of_im)], axis=1)
    o_all = o_all.reshape(ng, 4 * ns, nb * S5_GROUP)
    swap = lambda a: a.reshape(ng // 2, 2, ns)[:, ::-1].reshape(ng * ns)
    dec = jnp.stack([ak_re[nb, 0].reshape(-1), ak_im[nb, 0].reshape(-1), swap(ak_re[nb, 1]), swap(ak_im[nb, 1])])
    return smat.astype(BF16), m, o_all.astype(BF16), dec


def _s5_kernel(u_ref, smat_ref, m_ref, ow_ref, dec_ref, y_ref, us, uf, ys, sfr, sbr, sfi, sbi, hfr, hbr, hfi, hbi, *,
               n_ctx_blocks):
    nb = S5_BLOCK
    groups = smat_ref.shape[0]
    n = u_ref.shape[1] // nb
    ncb = n_ctx_blocks
    seg = S5_GROUP
    per_col = 128 // seg
    lane_seg = lax.broadcasted_iota(jnp.int32, (n, 128), 1) // seg
    low = lax.broadcasted_iota(jnp.int32, (n, 128), 1) < S5_STATE

    for s in range(nb):
        us[s] = u_ref[0, pl.ds(s, n, stride=nb), :]

    def fold(g):
        for v in range(nb // per_col):
            col = None
            for k in range(per_col):
                x = us[v * per_col + k]
                shift = (seg * (k - g)) % 128
                if shift:
                    x = pltpu.roll(x, shift, 1)
                col = x if col is None else jnp.where(lane_seg == k, x, col)
            uf[g, :, 128 * v:128 * (v + 1)] = col.astype(BF16)
        return jnp.dot(uf[g], smat_ref[g], preferred_element_type=F32)

    for p in range(groups // 2):
        se, so = fold(2 * p), fold(2 * p + 1)
        sl = slice(128 * p, 128 * (p + 1))
        sfr[:, sl] = jnp.where(low, se[:, :128], so[:, :128])
        sbr[:, sl] = jnp.where(low, so[:, :128], se[:, :128])
        sfi[:, sl] = jnp.where(low, se[:, 128:], so[:, 128:])
        sbi[:, sl] = jnp.where(low, so[:, 128:], se[:, 128:])
    arf, aif, arb, aib = dec_ref[0, 0:1], dec_ref[0, 1:2], dec_ref[0, 2:3], dec_ref[0, 3:4]

    def step(i, carry):
        fr, fi, br, bi = carry
        rf = i
        rb = jnp.where(i < ncb, ncb - 1 - i, n + ncb - 1 - i)
        hfr[pl.ds(rf, 1), :] = fr
        hfi[pl.ds(rf, 1), :] = fi
        hbr[pl.ds(rb, 1), :] = br
        hbi[pl.ds(rb, 1), :] = bi
        nfr = arf * fr - aif * fi + sfr[pl.ds(rf, 1), :]
        nfi = arf * fi + aif * fr + sfi[pl.ds(rf, 1), :]
        nbr = arb * br - aib * bi + sbr[pl.ds(rb, 1), :]
        nbi = arb * bi + aib * br + sbi[pl.ds(rb, 1), :]
        return nfr, nfi, nbr, nbi

    zero = jnp.zeros((1, 128 * (groups // 2)), F32)
    lax.fori_loop(0, n, step, (zero, zero, zero, zero))
    for g in range(groups):
        sl = slice(128 * (g // 2), 128 * (g // 2 + 1))
        first, second = ((hfr, hfi), (hbr, hbi)) if g % 2 == 0 else ((hbr, hbi), (hfr, hfi))
        states = jnp.concatenate(
            [jnp.where(low, a[:, sl], b[:, sl]).astype(BF16) for a, b in zip(first, second)], axis=1)
        ys[g] = (jnp.dot(uf[g], m_ref[g], preferred_element_type=F32)
                 + jnp.dot(states, ow_ref[g], preferred_element_type=F32))
    for l in range(nb):
        v, k = divmod(l, per_col)
        out = None
        for g in range(groups):
            x = ys[g, :, 128 * v:128 * (v + 1)]
            shift = (seg * (g - k)) % 128
            if shift:
                x = pltpu.roll(x, shift, 1)
            out = x if out is None else jnp.where(lane_seg == g, x, out)
        y_ref[0, pl.ds(l, n, stride=nb), :] = out


def _s5(u, ops, n_ctx_blocks):
    smat, m, ow, dec = ops
    b, t, w = u.shape
    nb = S5_BLOCK
    n = t // nb
    gw = nb * S5_GROUP
    gpb = 128 // S5_GROUP
    steps = w // 128
    lanes = gpb * S5_STATE
    dec = dec.reshape(4, steps, lanes).transpose(1, 0, 2)
    dec = jnp.concatenate([dec, jnp.zeros_like(dec)], axis=1)
    return pl.pallas_call(
        functools.partial(_s5_kernel, n_ctx_blocks=n_ctx_blocks),
        grid=(b, steps),
        in_specs=[
            pl.BlockSpec((1, t, 128), lambda i, j: (i, 0, j)),
            pl.BlockSpec((gpb,) + smat.shape[1:], lambda i, j: (j, 0, 0)),
            pl.BlockSpec((gpb,) + m.shape[1:], lambda i, j: (j, 0, 0)),
            pl.BlockSpec((gpb,) + ow.shape[1:], lambda i, j: (j, 0, 0)),
            pl.BlockSpec((1, 8, lanes), lambda i, j: (j, 0, 0)),
        ],
        out_specs=pl.BlockSpec((1, t, 128), lambda i, j: (i, 0, j)),
        out_shape=jax.ShapeDtypeStruct((b, t, w), F32),
        scratch_shapes=[pltpu.VMEM((nb, n, 128), F32), pltpu.VMEM((gpb, n, gw), BF16), pltpu.VMEM((gpb, n, gw), F32)]
        + [pltpu.VMEM((n, lanes), F32)] * 8,
        compiler_params=_params(("parallel", "parallel")),
        name="s5",
    )(u, smat, m, ow, dec)


def _even_out_kernel(yf_ref, yb_ref, xs_ref, z_ref, y5_ref, u_ref, g_ref, ctx_ref, x_ref, mod_ref, vs_ref, v5_ref,
                     glu_ref, w_ref, o_ref, *, n_ctx_tiles):
    ws = z_ref.shape[-1]
    y = _gelu_tanh(y5_ref[0] + v5_ref[0:1] * u_ref[0])
    glu = jnp.dot(y.astype(BF16), glu_ref[...], preferred_element_type=F32)
    ys = yf_ref[0].astype(F32) + yb_ref[0].astype(F32) + vs_ref[1:2] * xs_ref[0]
    s = _rms(ys * _silu(z_ref[0].astype(F32))) * vs_ref[0:1]
    o = jnp.dot(s.astype(BF16), w_ref[0:ws], preferred_element_type=F32)
    y = y * _sigmoid(glu + v5_ref[1:2]) * _silu(g_ref[0].astype(F32))
    o = o + jnp.dot(y.astype(BF16), w_ref[ws:], preferred_element_type=F32)
    o_ref[0] = _stream_tile(ctx_ref, x_ref, n_ctx_tiles) + mod_ref[0, 0][2:3] * o


def _even_out(yf, yb, xbc, z, y5, u, g, ctx, x, mod, ssd_norm, d_ssd, d_s5, glu_w, glu_b, w_out, n_ctx_tiles):
    b, t = z.shape[:2]
    d = x.shape[-1]
    tm = TOKEN_TILE
    ws, w5 = z.shape[-1], u.shape[-1]
    vs = jnp.pad(jnp.stack([ssd_norm, jnp.repeat(d_ssd, HEAD_DIM)]), ((0, 6), (0, 0)))
    v5 = jnp.pad(jnp.stack([d_s5, glu_b]), ((0, 6), (0, 0)))
    tok = lambda n: pl.BlockSpec((1, tm, n), lambda i, j: (i, j, 0))
    const = lambda a: pl.BlockSpec(a.shape, lambda i, j: (0,) * a.ndim)
    glu_w = glu_w.astype(BF16)
    w_out = w_out.astype(BF16)
    return pl.pallas_call(
        functools.partial(_even_out_kernel, n_ctx_tiles=n_ctx_tiles),
        grid=(b, t // tm),
        in_specs=[tok(ws), tok(ws), tok(ws), tok(ws), tok(w5), tok(w5), tok(w5)] + _stream_specs(tm, d, n_ctx_tiles) + [
            pl.BlockSpec((1, 1, 3, d), lambda i, j: (i, (j >= n_ctx_tiles).astype(jnp.int32), 0, 0)),
            const(vs), const(v5), const(glu_w), const(w_out),
        ],
        out_specs=tok(d),
        out_shape=jax.ShapeDtypeStruct((b, t, d), F32),
        compiler_params=_params(("parallel", "parallel")),
        name="even_out",
    )(yf, yb, xbc, z, y5, u, g, ctx, x, mod, vs, v5, glu_w, w_out)


def _odd_in_kernel(h_ref, mod_ref, w_ref, qg_ref, kg_ref, cos_ref, sin_ref, ones_ref, q_ref, k_ref, v_ref, g_ref, *,
                   q_w, kv_w):
    tm = TOKEN_TILE
    a = _norm_mod(h_ref[0], mod_ref[0, 0]).astype(BF16)
    cosv = cos_ref[...]
    sinv = sin_ref[...]
    first_half = (lax.broadcasted_iota(jnp.int32, (tm, 128), 1) % (HEAD_DIM // 2)) < (HEAD_DIM // 4)

    def project(lo, width=256):
        return jnp.dot(a, w_ref[:, lo:lo + width], preferred_element_type=F32)

    def head_norm_rope(x, gain, out_ref, c, transposed):
        ms = jnp.dot((x * x).astype(BF16), ones_ref[...], preferred_element_type=F32) * (1.0 / HEAD_DIM)
        xn = x * lax.rsqrt(ms + NORM_EPS) * gain
        for s in range(2):
            xb = xn[:, 128 * s:128 * (s + 1)]
            partner = jnp.where(first_half, pltpu.roll(xb, 128 - HEAD_DIM // 4, 1), pltpu.roll(xb, HEAD_DIM // 4, 1))
            r = xb * cosv + partner * sinv
            head = 4 * c + 2 * s
            if transposed:
                rt = r.T
                out_ref[0, head] = rt[:HEAD_DIM].astype(out_ref.dtype)
                out_ref[0, head + 1] = rt[HEAD_DIM:].astype(out_ref.dtype)
            else:
                out_ref[0, head] = r[:, :HEAD_DIM].astype(out_ref.dtype)
                out_ref[0, head + 1] = pltpu.roll(r, HEAD_DIM, 1)[:, :HEAD_DIM].astype(out_ref.dtype)

    chunks = [(256 * c, qg_ref, q_ref, c) for c in range(q_w // 256)] + \
             [(q_w + 256 * c, kg_ref, k_ref, c) for c in range(kv_w // 256)]
    g_lo = q_w + 2 * kv_w
    g_w = w_ref.shape[1] - g_lo
    x_next = project(chunks[0][0])
    for i, (lo, gain_ref, out_ref, c) in enumerate(chunks):
        x = x_next
        if i + 1 < len(chunks):
            x_next = project(chunks[i + 1][0])
        if 256 * i < g_w:
            g_ref[0, :, 256 * i:256 * (i + 1)] = project(g_lo + 256 * i)
        head_norm_rope(x, gain_ref[:, 256 * c:256 * (c + 1)], out_ref, c, out_ref is q_ref)
    assert 256 * len(chunks) >= g_w
    lane = lax.broadcasted_iota(jnp.int32, (tm, 128), 1)
    vt_rows = v_ref.shape[2]
    for c in range(kv_w // 128):
        x = project(q_w + kv_w + 128 * c, 128)
        for s in range(2):
            xs = pltpu.roll(x, HEAD_DIM, 1) if s else x
            vh = jnp.where(lane < HEAD_DIM, xs, jnp.where(lane == HEAD_DIM, 1.0, 0.0))
            v_ref[0, 2 * c + s] = vh.T[:vt_rows].astype(v_ref.dtype)


def _rope_tables(n_ctx, n_lat):
    pairs = HEAD_DIM // 4
    pos = jnp.arange(n_lat)
    row = (pos // GRID_W).astype(F32)
    colp = (pos % GRID_W).astype(F32)
    inv = ROPE_THETA ** (-jnp.arange(pairs, dtype=F32) / pairs)
    lane = jnp.arange(128) % HEAD_DIM
    axis_is_col = (lane // (HEAD_DIM // 2)) == 1
    ang = jnp.where(axis_is_col[None, :], colp[:, None], row[:, None]) * inv[lane % pairs][None, :]
    sign = jnp.where((lane % (HEAD_DIM // 2)) < pairs, -1.0, 1.0)
    cos = jnp.concatenate([jnp.ones((n_ctx, 128), F32), jnp.cos(ang)], axis=0)
    sin = jnp.concatenate([jnp.zeros((n_ctx, 128), F32), jnp.sin(ang) * sign[None, :]], axis=0)
    return cos, sin


def _odd_in(h, mod, w, q_gain, k_gain, cos, sin, n_ctx_tiles):
    b, t, d = h.shape
    tm = TOKEN_TILE
    q_w = ATTN_Q_HEADS * HEAD_DIM
    kv_w = ATTN_KV_HEADS * HEAD_DIM
    qg = (jnp.tile(q_gain, ATTN_Q_HEADS) * (HEAD_DIM ** -0.5 * math.log2(math.e))).reshape(1, q_w)
    kg = jnp.tile(k_gain, ATTN_KV_HEADS).reshape(1, kv_w)
    blk = jnp.arange(256) // HEAD_DIM
    ones = (blk[:, None] == blk[None, :]).astype(BF16)
    tok = lambda n: pl.BlockSpec((1, tm, n), lambda i, j: (i, j, 0))
    heads = lambda nh, n: pl.BlockSpec((1, nh, tm, n), lambda i, j: (i, 0, j, 0))
    heads_t = lambda nh, n: pl.BlockSpec((1, nh, n, tm), lambda i, j: (i, 0, 0, j))
    const = lambda a: pl.BlockSpec(a.shape, lambda i, j: (0,) * a.ndim)
    return pl.pallas_call(
        functools.partial(_odd_in_kernel, q_w=q_w, kv_w=kv_w),
        grid=(b, t // tm),
        in_specs=[
            tok(d),
            pl.BlockSpec((1, 1, 3, d), lambda i, j: (i, (j >= n_ctx_tiles).astype(jnp.int32), 0, 0)),
            const(w), const(qg), const(kg),
            pl.BlockSpec((tm, 128), lambda i, j: (j, 0)),
            pl.BlockSpec((tm, 128), lambda i, j: (j, 0)),
            const(ones),
        ],
        out_specs=[heads_t(ATTN_Q_HEADS, HEAD_DIM), heads(ATTN_KV_HEADS, HEAD_DIM), heads_t(ATTN_KV_HEADS, VT_ROWS),
                   tok(q_w)],
        out_shape=[
            jax.ShapeDtypeStruct((b, ATTN_Q_HEADS, HEAD_DIM, t), BF16),
            jax.ShapeDtypeStruct((b, ATTN_KV_HEADS, t, HEAD_DIM), BF16),
            jax.ShapeDtypeStruct((b, ATTN_KV_HEADS, VT_ROWS, t), BF16),
            jax.ShapeDtypeStruct((b, t, q_w), F32),
        ],
        compiler_params=_params(("parallel", "parallel")),
        name="odd_in",
    )(h, mod, w, qg, kg, cos, sin, ones)


def _attn_kernel(q_ref, k_ref, v_ref, o_ref, *, tk):
    rep, hd, tq = q_ref.shape[1:]
    rows = v_ref.shape[2]
    nk = k_ref.shape[2] // tk
    m = [jnp.full((1, tq), -1e30, F32)] * rep
    acc = [jnp.zeros((rows, tq), F32)] * rep
    blocks = [(j, r) for j in range(nk) for r in range(rep)]
    scores = {}
    for i in range(len(blocks) + ATTN_LOOKAHEAD):
        if i < len(blocks):
            j, r = blocks[i]
            scores[i] = jnp.dot(k_ref[0, 0, j * tk:(j + 1) * tk, :], q_ref[0, r], preferred_element_type=F32)
        if i >= ATTN_LOOKAHEAD:
            j, r = blocks[i - ATTN_LOOKAHEAD]
            s = scores.pop(i - ATTN_LOOKAHEAD)
            m_new = jnp.maximum(m[r], jnp.max(s, axis=0, keepdims=True))
            p = jnp.exp2(s - m_new).astype(BF16)
            pv = jnp.dot(v_ref[0, 0, :, j * tk:(j + 1) * tk], p, preferred_element_type=F32)
            acc[r] = jnp.exp2(m[r] - m_new) * acc[r] + pv
            m[r] = m_new
    for c in range(rep // 2):
        pair = [acc[r][:hd] * (1.0 / acc[r][hd:hd + 1]) for r in (2 * c, 2 * c + 1)]
        o_ref[0, :, 2 * hd * c:2 * hd * (c + 1)] = jnp.concatenate(pair, axis=0).T.astype(o_ref.dtype)


def _attention(q, k, v, n_ctx):
    b, hq, hd, t = q.shape
    hkv = k.shape[1]
    rep = hq // hkv
    tq = 256
    tk = next(c for c in (256, 128) if t % c == 0)
    assert n_ctx % tq == 0
    return pl.pallas_call(
        functools.partial(_attn_kernel, tk=tk),
        grid=(b, hkv, (t - n_ctx) // tq),
        in_specs=[
            pl.BlockSpec((1, rep, hd, tq), lambda i, j, n: (i, j, 0, n + n_ctx // tq)),
            pl.BlockSpec((1, 1, t, hd), lambda i, j, n: (i, j, 0, 0)),
            pl.BlockSpec((1, 1, v.shape[2], t), lambda i, j, n: (i, j, 0, 0)),
        ],
        out_specs=pl.BlockSpec((1, tq, rep * hd), lambda i, j, n: (i, n, j)),
        out_shape=jax.ShapeDtypeStruct((b, t - n_ctx, hq * hd), BF16),
        compiler_params=_params(("parallel", "parallel", "parallel")),
        name="attention",
    )(q, k, v)


def _attn_out_kernel(o_ref, g_ref, h_ref, mod_ref, w_ref, fg_ref, out_ref):
    x = o_ref[0].astype(F32) * _silu(g_ref[0])
    y = jnp.dot(x.astype(BF16), w_ref[...], preferred_element_type=F32)
    hn = h_ref[0] + mod_ref[0, 0][2:3] * y
    out_ref[0] = _rms(hn) * fg_ref[...]


def _attn_out(o, g, h, mod, w_out, final_gain, n_ctx_tiles):
    b, n_lat, d = o.shape
    tm = TOKEN_TILE
    lat = lambda n: pl.BlockSpec((1, tm, n), lambda i, j: (i, j + n_ctx_tiles, 0))
    w_out = w_out.astype(BF16)
    return pl.pallas_call(
        _attn_out_kernel,
        grid=(b, n_lat // tm),
        in_specs=[
            pl.BlockSpec((1, tm, d), lambda i, j: (i, j, 0)),
            lat(d), lat(d),
            pl.BlockSpec((1, 1, 3, d), lambda i, j: (i, 1, 0, 0)),
            pl.BlockSpec(w_out.shape, lambda i, j: (0, 0)),
            pl.BlockSpec((1, d), lambda i, j: (0, 0)),
        ],
        out_specs=pl.BlockSpec((1, tm, d), lambda i, j: (i, j, 0)),
        out_shape=jax.ShapeDtypeStruct((b, n_lat, d), F32),
        compiler_params=_params(("parallel", "parallel")),
        name="attn_out",
    )(o, g, h, mod, w_out, final_gain.reshape(1, d))


def kernel(x, c, ctx, c_ctx, ada_w, ada_b, ev_w_in, ev_conv_w, ev_conv_b, ev_dt_bias, ev_a_log, ev_d_ssd, ev_ssd_norm, ev_lam_re, ev_lam_im, ev_log_step, ev_b_re, ev_b_im, ev_c_re, ev_c_im, ev_d_s5, ev_glu_w, ev_glu_b, ev_w_out, od_w_in, od_q_gain, od_k_gain, od_w_out, final_gain):
    b, n_lat, d = x.shape
    n_ctx = ctx.shape[1]
    assert ada_w.shape[0] == 2 and n_ctx % TOKEN_TILE == 0 and n_lat % TOKEN_TILE == 0
    n_ctx_tiles = n_ctx // TOKEN_TILE
    mods = _adaln(c, c_ctx, ada_w, ada_b)

    ws = SSD_HEADS * HEAD_DIM
    wx = ws + 2 * SSD_GROUPS * SSD_STATE
    w5 = ev_d_s5.shape[-1]
    w = ev_w_in[0]
    cuts = (ws, ws + wx, ws + wx + 2 * SSD_HEADS, ws + wx + 2 * SSD_HEADS + w5)
    w_dt = jnp.pad(w[:, cuts[1]:cuts[2]], ((0, 0), (0, 128 - 2 * SSD_HEADS)))
    w_dt_hi, w_dt_lo = _split_bf16(w_dt)
    w_cat = jnp.concatenate(
        [w[:, :cuts[1]].astype(BF16), w[:, cuts[2]:].astype(BF16), w_dt_hi, w_dt_lo], axis=1)
    z, xbc, u, g, dt = _even_in(ctx, x, mods[0], w_cat, n_ctx_tiles, (ws, wx, w5, w5))
    xbc = _conv(xbc, ev_conv_w[0], ev_conv_b[0], n_ctx_tiles)
    yf, yb = _ssd(xbc, dt, ev_dt_bias[0], ev_a_log[0], n_ctx)
    ops = _s5_operators(ev_lam_re[0], ev_lam_im[0], ev_log_step[0], ev_b_re[0], ev_b_im[0], ev_c_re[0], ev_c_im[0])
    y5 = _s5(u, ops, n_ctx // S5_BLOCK)
    h = _even_out(yf, yb, xbc, z, y5, u, g, ctx, x, mods[0], ev_ssd_norm[0], ev_d_ssd[0], ev_d_s5[0], ev_glu_w[0],
                  ev_glu_b[0], ev_w_out[0], n_ctx_tiles)

    cos, sin = _rope_tables(n_ctx, n_lat)
    q, k, v, g = _odd_in(h, mods[1], od_w_in[0].astype(BF16), od_q_gain[0], od_k_gain[0], cos, sin, n_ctx_tiles)
    o = _attention(q, k, v, n_ctx)
    return _attn_out(o, g, h, mods[1], od_w_out[0], final_gain, n_ctx_tiles)
```

```python
import functools
import math

import jax
import jax.numpy as jnp
from jax import lax
from jax.experimental import pallas as pl
from jax.experimental.pallas import tpu as pltpu

F32 = jnp.float32
BF16 = jnp.bfloat16
HIGHEST = lax.Precision.HIGHEST

NORM_EPS = 1e-6
GRID_W = 64
ROPE_THETA = 10000.0

HEAD_DIM = 64
SSD_HEADS = 16
SSD_GROUPS = 4
SSD_STATE = 128
SSD_CHUNK = 128
SSD_CONV = 5
S5_GROUP = 16
S5_STATE = 64
S5_BLOCK = 16
ATTN_Q_HEADS = 16
ATTN_KV_HEADS = 4
VT_ROWS = 80
ATTN_Q_TILE = 256
ATTN_LOOKAHEAD = 8

TOKEN_TILE = 256
HALO_ROWS = 8
VMEM_LIMIT = 56 << 20


def _params(semantics):
    return pltpu.CompilerParams(dimension_semantics=semantics, vmem_limit_bytes=VMEM_LIMIT)


def _sigmoid(x):
    return 1.0 / (1.0 + jnp.exp(-x))


def _silu(x):
    return x * _sigmoid(x)


def _softplus(x):
    return jnp.maximum(x, 0.0) + jnp.log(1.0 + jnp.exp(-jnp.abs(x)))


def _gelu_tanh(x):
    return 0.5 * x * (1.0 + jnp.tanh(math.sqrt(2.0 / math.pi) * (x + 0.044715 * (x * x * x))))


def _rms(x):
    return x * lax.rsqrt(jnp.mean(x * x, axis=-1, keepdims=True) + NORM_EPS)


def _norm_mod(h, mod):
    return _rms(h) * (1.0 + mod[1:2]) + mod[0:1]


def _split_bf16(x):
    hi = x.astype(BF16)
    return hi, (x - hi.astype(F32)).astype(BF16)


def _adaln_kernel(s_ref, w_ref, b_ref, o_ref):
    s = _silu(s_ref[...])
    o_ref[0] = jnp.dot(s, w_ref[0], preferred_element_type=F32, precision=HIGHEST) + b_ref[0]


def _adaln(c, c_ctx, ada_w, ada_b):
    depth, d, d3 = ada_w.shape
    b = c.shape[0]
    assert b < 8
    rows = jnp.concatenate([c, c_ctx[None], jnp.zeros((7 - b, d), F32)], axis=0)
    out = pl.pallas_call(
        _adaln_kernel,
        grid=(depth, d3 // d),
        in_specs=[
            pl.BlockSpec((8, d), lambda i, j: (0, 0)),
            pl.BlockSpec((1, d, d), lambda i, j: (i, 0, j)),
            pl.BlockSpec((1, 1, d), lambda i, j: (i, 0, j)),
        ],
        out_specs=pl.BlockSpec((1, 8, d), lambda i, j: (i, 0, j)),
        out_shape=jax.ShapeDtypeStruct((depth, 8, d3), F32),
        compiler_params=_params(("arbitrary", "arbitrary")),
        name="adaln",
    )(rows, ada_w, ada_b.reshape(depth, 1, d3))
    m = out.reshape(depth, 8, 3, d)
    lat = m[:, :b]
    ctx = jnp.broadcast_to(m[:, b:b + 1], lat.shape)
    return jnp.stack([ctx, lat], axis=2)


def _stream_tile(ctx_ref, x_ref, n_ctx_tiles):
    return jnp.where(pl.program_id(1) < n_ctx_tiles, ctx_ref[0], x_ref[0])


def _stream_specs(tm, d, n_ctx_tiles):
    return [
        pl.BlockSpec((1, tm, d), lambda i, j: (i, jnp.minimum(j, n_ctx_tiles - 1), 0)),
        pl.BlockSpec((1, tm, d), lambda i, j: (i, jnp.maximum(j - n_ctx_tiles, 0), 0)),
    ]


def _even_in_kernel(ctx_ref, x_ref, mod_ref, w_ref, z_ref, xbc_ref, u_ref, g_ref, dt_ref, *, cuts, n_ctx_tiles):
    a = _norm_mod(_stream_tile(ctx_ref, x_ref, n_ctx_tiles), mod_ref[0, 0])
    a_hi, a_lo = _split_bf16(a)

    def mm(x, lo, hi):
        return jnp.dot(x, w_ref[:, lo:hi], preferred_element_type=F32)

    c0, c1, c2, c3, c4, c5 = cuts
    z_ref[0] = mm(a_hi, 0, c0).astype(z_ref.dtype)
    xbc_ref[0] = mm(a_hi, c0, c1)
    u_ref[0] = mm(a_hi, c1, c2)
    g_ref[0] = mm(a_hi, c2, c3).astype(g_ref.dtype)
    dt_ref[0] = mm(a_hi, c3, c4) + mm(a_lo, c3, c4) + mm(a_hi, c4, c5)


def _even_in(ctx, x, mod, w, n_ctx_tiles, widths):
    b, n_lat, d = x.shape
    t = ctx.shape[1] + n_lat
    tm = TOKEN_TILE
    wz, wx, wu, wg = widths
    cuts = (wz, wz + wx, wz + wx + wu, wz + wx + wu + wg, wz + wx + wu + wg + 128, wz + wx + wu + wg + 256)
    assert w.shape == (d, cuts[-1])
    tok = lambda n: pl.BlockSpec((1, tm, n), lambda i, j: (i, j, 0))
    return pl.pallas_call(
        functools.partial(_even_in_kernel, cuts=cuts, n_ctx_tiles=n_ctx_tiles),
        grid=(b, t // tm),
        in_specs=_stream_specs(tm, d, n_ctx_tiles) + [
            pl.BlockSpec((1, 1, 3, d), lambda i, j: (i, (j >= n_ctx_tiles).astype(jnp.int32), 0, 0)),
            pl.BlockSpec(w.shape, lambda i, j: (0, 0)),
        ],
        out_specs=[tok(wz), tok(wx), tok(wu), tok(wg), tok(128)],
        out_shape=[
            jax.ShapeDtypeStruct((b, t, wz), BF16),
            jax.ShapeDtypeStruct((b, t, wx), F32),
            jax.ShapeDtypeStruct((b, t, wu), F32),
            jax.ShapeDtypeStruct((b, t, wg), BF16),
            jax.ShapeDtypeStruct((b, t, 128), F32),
        ],
        compiler_params=_params(("parallel", "parallel")),
        name="even_in",
    )(ctx, x, mod, w)


def _conv_kernel(prev_ref, main_ref, next_ref, w_ref, b_ref, o_ref, xe_ref, *, n_ctx_tiles, n_tiles):
    tm = TOKEN_TILE
    ch = o_ref.shape[-1]
    t = pl.program_id(1)
    has_prev = jnp.logical_and(t != 0, t != n_ctx_tiles)
    has_next = jnp.logical_and(t != n_ctx_tiles - 1, t != n_tiles - 1)
    xe_ref[0:HALO_ROWS] = jnp.where(has_prev, prev_ref[0], 0.0)
    xe_ref[HALO_ROWS:HALO_ROWS + tm] = main_ref[0]
    xe_ref[HALO_ROWS + tm:2 * HALO_ROWS + tm] = jnp.where(has_next, next_ref[0], 0.0)
    rows = tm + 2 * HALO_ROWS
    half = SSD_CONV // 2
    lanes = 128
    for c0 in range(0, ch, lanes):
        cs = slice(c0, c0 + lanes)
        xe = xe_ref[:, cs]
        acc = b_ref[:, cs] + w_ref[half:half + 1, cs] * xe[HALO_ROWS:HALO_ROWS + tm]
        for k in range(SSD_CONV):
            if k != half:
                shifted = pltpu.roll(xe, (half - k) % rows, 0)
                acc = acc + w_ref[k:k + 1, cs] * shifted[HALO_ROWS:HALO_ROWS + tm]
        o_ref[0, :, cs] = _silu(acc)


def _conv(xbc, conv_w, conv_b, n_ctx_tiles):
    b, t, ch = xbc.shape
    tm = TOKEN_TILE
    n_tiles = t // tm
    per = tm // HALO_ROWS
    last = t // HALO_ROWS - 1
    w = jnp.pad(conv_w.T, ((0, 8 - SSD_CONV), (0, 0)))
    return pl.pallas_call(
        functools.partial(_conv_kernel, n_ctx_tiles=n_ctx_tiles, n_tiles=n_tiles),
        grid=(b, n_tiles),
        in_specs=[
            pl.BlockSpec((1, HALO_ROWS, ch), lambda i, j: (i, jnp.maximum(j * per - 1, 0), 0)),
            pl.BlockSpec((1, tm, ch), lambda i, j: (i, j, 0)),
            pl.BlockSpec((1, HALO_ROWS, ch), lambda i, j: (i, jnp.minimum((j + 1) * per, last), 0)),
            pl.BlockSpec((8, ch), lambda i, j: (0, 0)),
            pl.BlockSpec((1, ch), lambda i, j: (0, 0)),
        ],
        out_specs=pl.BlockSpec((1, tm, ch), lambda i, j: (i, j, 0)),
        out_shape=jax.ShapeDtypeStruct((b, t, ch), F32),
        scratch_shapes=[pltpu.VMEM((tm + 2 * HALO_ROWS, ch), F32)],
        compiler_params=_params(("parallel", "parallel")),
        name="conv",
    )(xbc, xbc, xbc, w, conv_b.reshape(1, ch))


def _ssd_kernel(xf_ref, xb_ref, dtf_ref, dtb_ref, bias_ref, a_ref, e_ref, yf_ref, yb_ref, st_ref):
    L = SSD_CHUNK
    width = SSD_HEADS * HEAD_DIM
    gw = width // SSD_GROUPS
    hpg = SSD_HEADS // SSD_GROUPS

    @pl.when(pl.program_id(1) == 0)
    def _():
        st_ref[...] = jnp.zeros_like(st_ref)

    row = lax.broadcasted_iota(jnp.int32, (L, L), 0)
    col = lax.broadcasted_iota(jnp.int32, (L, L), 1)
    lane_head = lax.broadcasted_iota(jnp.int32, (L, gw), 1) // HEAD_DIM
    sub16 = lax.broadcasted_iota(jnp.int32, (16, 128), 0)

    def split3(x):
        x1 = x.astype(BF16)
        r = x - x1.astype(F32)
        x2 = r.astype(BF16)
        return x1, x2, (r - x2.astype(F32)).astype(BF16)

    def chunk(d, x_ref, dt_ref, y_ref, r0):
        rows = pl.ds(r0, L)
        mask = (row >= col) if d == 0 else (row <= col)
        dtv = _softplus(dt_ref[0, rows, :] + bias_ref[...])
        loga = dtv * a_ref[...]
        tri = jnp.where(mask, 1.0, 0.0).astype(BF16)
        cs = sum(jnp.dot(tri, part, preferred_element_type=F32) for part in split3(loga))
        cs_t = cs.T
        total = jnp.sum(loga, axis=0, keepdims=True)
        t1, t2, t3 = (part.astype(F32) for part in split3(total))
        tot = jnp.where(sub16 == 0, t1, jnp.where(sub16 == 1, t2, jnp.where(sub16 == 2, t3, 0.0)))
        stack = jnp.concatenate(
            [dtv.astype(BF16), jnp.exp(cs).astype(BF16), jnp.exp(total - cs).astype(BF16), tot.astype(BF16)], axis=0)
        ex = jnp.dot(stack, e_ref[d], preferred_element_type=F32)
        dt_x = ex[0:L]
        ecs_x = ex[L:2 * L]
        wend_x = ex[2 * L:3 * L]
        edec_x = jnp.exp(ex[3 * L:3 * L + 1] + ex[3 * L + 1:3 * L + 2] + ex[3 * L + 2:3 * L + 3])

        groups = []
        for g in range(SSD_GROUPS):
            sl = slice(g * gw, (g + 1) * gw)
            xs_g = x_ref[0, rows, sl]
            b_g = x_ref[0, rows, width + g * SSD_STATE:width + (g + 1) * SSD_STATE]
            c_g = x_ref[0, rows, width + (SSD_GROUPS + g) * SSD_STATE:width + (SSD_GROUPS + g + 1) * SSD_STATE]
            cb = c_g.astype(BF16)
            xd = xs_g * dt_x[:, sl]
            gram = lax.dot_general(cb, b_g.astype(BF16), (((1,), (1,)), ((), ())), preferred_element_type=F32)
            scores = []
            for hh in range(hpg):
                li = SSD_HEADS * d + hpg * g + hh
                seg = cs[:, li:li + 1] - cs_t[li:li + 1, :]
                scores.append((gram * jnp.where(mask, jnp.exp(seg), 0.0)).astype(BF16))
            scores = jnp.concatenate(scores, axis=1)
            xd_blocks = jnp.concatenate(
                [jnp.where(lane_head == hh, xd, 0.0).astype(BF16) for hh in range(hpg)], axis=0)
            y = jnp.dot(scores, xd_blocks, preferred_element_type=F32)
            xdw = (xd * wend_x[:, sl]).astype(BF16)
            groups.append((y, cb, b_g.T.astype(BF16), xdw, ecs_x[:, sl], edec_x[:, sl]))

        def carry():
            for g, (y, cb, bt, xdw, ecs_g, edec_g) in enumerate(groups):
                sl = slice(g * gw, (g + 1) * gw)
                st = st_ref[d, :, sl]
                y_ref[0, rows, sl] = (y + jnp.dot(cb, st.astype(BF16), preferred_element_type=F32) * ecs_g
                                      ).astype(y_ref.dtype)
                st_ref[d, :, sl] = st * edec_g + jnp.dot(bt, xdw, preferred_element_type=F32)

        return carry

    carries = [chunk(0, xf_ref, dtf_ref, yf_ref, 0), chunk(1, xb_ref, dtb_ref, yb_ref, L),
               chunk(0, xf_ref, dtf_ref, yf_ref, L), chunk(1, xb_ref, dtb_ref, yb_ref, 0)]
    for carry in carries:
        carry()


def _ssd(xbc, dt, dt_bias, a_log, n_ctx):
    b, t, ch = xbc.shape
    L = 2 * SSD_CHUNK
    assert n_ctx % L == 0 and t % L == 0
    n = t // L
    width = SSD_HEADS * HEAD_DIM
    ncc = n_ctx // L

    def fwd(i, j):
        return (i, j, 0)

    def bwd(i, j):
        return (i, jnp.where(j < ncc, ncc - 1 - j, n + ncc - 1 - j), 0)

    pad = lambda v: jnp.pad(v.reshape(1, -1), ((0, 0), (0, 128 - v.size)))
    bias = pad(dt_bias)
    a_neg = pad(-jnp.exp(a_log))
    head_of_lane = jnp.arange(width) // HEAD_DIM
    expand = jnp.stack([
        (jnp.arange(128)[:, None] == SSD_HEADS * d + head_of_lane[None, :]) for d in range(2)
    ]).astype(BF16)
    return pl.pallas_call(
        _ssd_kernel,
        grid=(b, n),
        in_specs=[
            pl.BlockSpec((1, L, ch), fwd),
            pl.BlockSpec((1, L, ch), bwd),
            pl.BlockSpec((1, L, 128), fwd),
            pl.BlockSpec((1, L, 128), bwd),
            pl.BlockSpec((1, 128), lambda i, j: (0, 0)),
            pl.BlockSpec((1, 128), lambda i, j: (0, 0)),
            pl.BlockSpec((2, 128, width), lambda i, j: (0, 0, 0)),
        ],
        out_specs=[pl.BlockSpec((1, L, width), fwd), pl.BlockSpec((1, L, width), bwd)],
        out_shape=[jax.ShapeDtypeStruct((b, t, width), BF16)] * 2,
        scratch_shapes=[pltpu.VMEM((2, SSD_STATE, width), F32)],
        compiler_params=_params(("parallel", "arbitrary")),
        name="ssd",
    )(xbc, xbc, dt, dt, bias, a_neg, expand)


def _toeplitz_kernel(z_ref, m_ref):
    nb, width = S5_BLOCK, S5_BLOCK * S5_GROUP
    lanes = z_ref.shape[-1]
    for g in range(z_ref.shape[0]):
        z = z_ref[g]
        for s in range(nb):
            off = (nb - 1 - s) * S5_GROUP
            win = pltpu.roll(z, (lanes - off) % lanes, 1) if off else z
            m_ref[g, s * S5_GROUP:(s + 1) * S5_GROUP, :] = win[:, :width].astype(m_ref.dtype)


def _toeplitz(zall):
    ng, nj, w = zall.shape
    lanes = -(-w // 128) * 128
    per = 8
    width = S5_BLOCK * S5_GROUP
    return pl.pallas_call(
        _toeplitz_kernel,
        grid=(ng // per,),
        in_specs=[pl.BlockSpec((per, nj, lanes), lambda i: (i, 0, 0))],
        out_specs=pl.BlockSpec((per, S5_BLOCK * nj, width), lambda i: (i, 0, 0)),
        out_shape=jax.ShapeDtypeStruct((ng, S5_BLOCK * nj, width), BF16),
        compiler_params=_params(("parallel",)),
        name="s5_toeplitz",
    )(jnp.pad(zall, ((0, 0), (0, 0), (0, lanes - w))))


def _s5_operators(lam_re, lam_im, log_step, b_re, b_im, c_re, c_im):
    nb = S5_BLOCK
    ng, ns = lam_re.shape[1:]
    hp = functools.partial(jnp.einsum, precision=HIGHEST)
    step = jnp.exp(log_step)[..., None]
    k = jnp.arange(nb + 1, dtype=F32)[:, None, None, None]
    mag = jnp.exp(k * (lam_re * step))
    ak_re = mag * jnp.cos(k * (lam_im * step))
    ak_im = mag * jnp.sin(k * (lam_im * step))
    ab_re, ab_im = ak_re[1], ak_im[1]
    den = lam_re * lam_re + lam_im * lam_im
    f_re = ((ab_re - 1.0) * lam_re + ab_im * lam_im) / den
    f_im = (ab_im * lam_re - (ab_re - 1.0) * lam_im) / den
    tr = lambda a: jnp.swapaxes(a, -1, -2)
    bb_re = tr(f_re[..., None] * b_re - f_im[..., None] * b_im)
    bb_im = tr(f_re[..., None] * b_im + f_im[..., None] * b_re)
    akr, aki = ak_re[:, :, :, None, :], ak_im[:, :, :, None, :]
    w_re = akr * bb_re - aki * bb_im
    w_im = akr * bb_im + aki * bb_re
    kern = hp('kdgjp,dgip->kdgji', w_re[:nb], c_re) - hp('kdgjp,dgip->kdgji', w_im[:nb], c_im)
    kf, kb = kern[:, 0], kern[:, 1]
    zall = jnp.concatenate([kb[:0:-1], (kf[0] + kb[0])[None], kf[1:]], axis=0).transpose(1, 2, 0, 3)
    m = _toeplitz(zall.reshape(ng, S5_GROUP, (2 * nb - 1) * S5_GROUP))
    inj = lambda w, d, rev: jnp.swapaxes(w[nb - 1::-1, d] if rev else w[:nb, d], 0, 1)
    def readout(d, ks):
        ar, ai = (a[ks, d].transpose(1, 2, 0)[..., None] for a in (ak_re, ak_im))
        cr, ci = (tr(c[d])[:, :, None, :] for c in (c_re, c_im))
        return cr * ar - ci * ai, -(cr * ai + ci * ar)
    of_re, of_im = readout(0, jnp.arange(1, nb + 1))
    ob_re, ob_im = readout(1, jnp.arange(nb, 0, -1))
    odd = (jnp.arange(ng) % 2 == 1)
    pick = lambda a, b: jnp.where(odd.reshape((ng,) + (1,) * (a.ndim - 1)), b, a)
    sf_re, sb_re, sf_im, sb_im = inj(w_re, 0, True), inj(w_re, 1, False), inj(w_im, 0, True), inj(w_im, 1, False)
    smat = jnp.stack([pick(sf_re, sb_re), pick(sb_re, sf_re), pick(sf_im, sb_im), pick(sb_im, sf_im)], axis=3)
    smat = smat.reshape(ng, nb * S5_GROUP, 4 * ns)
    o_all = jnp.stack([pick(of_re, ob_re), pick(ob_re, of_re), pick(of_im, ob_im), pick(ob_im, of_im)], axis=1)
    o_all = o_all.reshape(ng, 4 * ns, nb * S5_GROUP)
    swap = lambda a: a.reshape(ng // 2, 2, ns)[:, ::-1].reshape(ng * ns)
    dec = jnp.stack([ak_re[nb, 0].reshape(-1), ak_im[nb, 0].reshape(-1), swap(ak_re[nb, 1]), swap(ak_im[nb, 1])])
    return smat.astype(BF16), m, o_all.astype(BF16), dec


def _s5_kernel(u_ref, smat_ref, m_ref, ow_ref, dec_ref, y_ref, us, uf, ys, sfr, sbr, sfi, sbi, hfr, hbr, hfi, hbi, *,
               n_ctx_blocks):
    nb = S5_BLOCK
    groups = smat_ref.shape[0]
    n = u_ref.shape[1] // nb
    ncb = n_ctx_blocks
    seg = S5_GROUP
    per_col = 128 // seg
    lane_seg = lax.broadcasted_iota(jnp.int32, (n, 128), 1) // seg
    low = lax.broadcasted_iota(jnp.int32, (n, 128), 1) < S5_STATE

    for s in range(nb):
        us[s] = u_ref[0, pl.ds(s, n, stride=nb), :]

    def fold(g):
        for v in range(nb // per_col):
            col = None
            for k in range(per_col):
                x = us[v * per_col + k]
                shift = (seg * (k - g)) % 128
                if shift:
                    x = pltpu.roll(x, shift, 1)
                col = x if col is None else jnp.where(lane_seg == k, x, col)
            uf[g, :, 128 * v:128 * (v + 1)] = col.astype(BF16)
        return jnp.dot(uf[g], smat_ref[g], preferred_element_type=F32)

    for p in range(groups // 2):
        se, so = fold(2 * p), fold(2 * p + 1)
        sl = slice(128 * p, 128 * (p + 1))
        sfr[:, sl] = jnp.where(low, se[:, :128], so[:, :128])
        sbr[:, sl] = jnp.where(low, so[:, :128], se[:, :128])
        sfi[:, sl] = jnp.where(low, se[:, 128:], so[:, 128:])
        sbi[:, sl] = jnp.where(low, so[:, 128:], se[:, 128:])
    arf, aif, arb, aib = dec_ref[0, 0:1], dec_ref[0, 1:2], dec_ref[0, 2:3], dec_ref[0, 3:4]

    def step(i, carry):
        fr, fi, br, bi = carry
        rf = i
        rb = jnp.where(i < ncb, ncb - 1 - i, n + ncb - 1 - i)
        hfr[pl.ds(rf, 1), :] = fr
        hfi[pl.ds(rf, 1), :] = fi
        hbr[pl.ds(rb, 1), :] = br
        hbi[pl.ds(rb, 1), :] = bi
        nfr = arf * fr - aif * fi + sfr[pl.ds(rf, 1), :]
        nfi = arf * fi + aif * fr + sfi[pl.ds(rf, 1), :]
        nbr = arb * br - aib * bi + sbr[pl.ds(rb, 1), :]
        nbi = arb * bi + aib * br + sbi[pl.ds(rb, 1), :]
        return nfr, nfi, nbr, nbi

    zero = jnp.zeros((1, 128 * (groups // 2)), F32)
    lax.fori_loop(0, n, step, (zero, zero, zero, zero))
    for g in range(groups):
        sl = slice(128 * (g // 2), 128 * (g // 2 + 1))
        first, second = ((hfr, hfi), (hbr, hbi)) if g % 2 == 0 else ((hbr, hbi), (hfr, hfi))
        states = jnp.concatenate(
            [jnp.where(low, a[:, sl], b[:, sl]).astype(BF16) for a, b in zip(first, second)], axis=1)
        ys[g] = (jnp.dot(uf[g], m_ref[g], preferred_element_type=F32)
                 + jnp.dot(states, ow_ref[g], preferred_element_type=F32))
    for l in range(nb):
        v, k = divmod(l, per_col)
        out = None
        for g in range(groups):
            x = ys[g, :, 128 * v:128 * (v + 1)]
            shift = (seg * (g - k)) % 128
            if shift:
                x = pltpu.roll(x, shift, 1)
            out = x if out is None else jnp.where(lane_seg == g, x, out)
        y_ref[0, pl.ds(l, n, stride=nb), :] = out


def _s5(u, ops, n_ctx_blocks):
    smat, m, ow, dec = ops
    b, t, w = u.shape
    nb = S5_BLOCK
    n = t // nb
    gw = nb * S5_GROUP
    gpb = 128 // S5_GROUP
    steps = w // 128
    lanes = gpb * S5_STATE
    dec = dec.reshape(4, steps, lanes).transpose(1, 0, 2)
    dec = jnp.concatenate([dec, jnp.zeros_like(dec)], axis=1)
    return pl.pallas_call(
        functools.partial(_s5_kernel, n_ctx_blocks=n_ctx_blocks),
        grid=(b, steps),
        in_specs=[
            pl.BlockSpec((1, t, 128), lambda i, j: (i, 0, j)),
            pl.BlockSpec((gpb,) + smat.shape[1:], lambda i, j: (j, 0, 0)),
            pl.BlockSpec((gpb,) + m.shape[1:], lambda i, j: (j, 0, 0)),
            pl.BlockSpec((gpb,) + ow.shape[1:], lambda i, j: (j, 0, 0)),
            pl.BlockSpec((1, 8, lanes), lambda i, j: (j, 0, 0)),
        ],
        out_specs=pl.BlockSpec((1, t, 128), lambda i, j: (i, 0, j)),
        out_shape=jax.ShapeDtypeStruct((b, t, w), F32),
        scratch_shapes=[pltpu.VMEM((nb, n, 128), F32), pltpu.VMEM((gpb, n, gw), BF16), pltpu.VMEM((gpb, n, gw), F32)]
        + [pltpu.VMEM((n, lanes), F32)] * 8,
        compiler_params=_params(("parallel", "parallel")),
        name="s5",
    )(u, smat, m, ow, dec)


def _even_out_kernel(yf_ref, yb_ref, xs_ref, z_ref, y5_ref, u_ref, g_ref, ctx_ref, x_ref, mod_ref, vs_ref, v5_ref,
                     glu_ref, w_ref, o_ref, *, n_ctx_tiles):
    ws = z_ref.shape[-1]
    y = _gelu_tanh(y5_ref[0] + v5_ref[0:1] * u_ref[0])
    glu = jnp.dot(y.astype(BF16), glu_ref[...], preferred_element_type=F32)
    ys = yf_ref[0].astype(F32) + yb_ref[0].astype(F32) + vs_ref[1:2] * xs_ref[0]
    s = _rms(ys * _silu(z_ref[0].astype(F32))) * vs_ref[0:1]
    o = jnp.dot(s.astype(BF16), w_ref[0:ws], preferred_element_type=F32)
    y = y * _sigmoid(glu + v5_ref[1:2]) * _silu(g_ref[0].astype(F32))
    o = o + jnp.dot(y.astype(BF16), w_ref[ws:], preferred_element_type=F32)
    o_ref[0] = _stream_tile(ctx_ref, x_ref, n_ctx_tiles) + mod_ref[0, 0][2:3] * o


def _even_out(yf, yb, xbc, z, y5, u, g, ctx, x, mod, ssd_norm, d_ssd, d_s5, glu_w, glu_b, w_out, n_ctx_tiles):
    b, t = z.shape[:2]
    d = x.shape[-1]
    tm = TOKEN_TILE
    ws, w5 = z.shape[-1], u.shape[-1]
    vs = jnp.pad(jnp.stack([ssd_norm, jnp.repeat(d_ssd, HEAD_DIM)]), ((0, 6), (0, 0)))
    v5 = jnp.pad(jnp.stack([d_s5, glu_b]), ((0, 6), (0, 0)))
    tok = lambda n: pl.BlockSpec((1, tm, n), lambda i, j: (i, j, 0))
    const = lambda a: pl.BlockSpec(a.shape, lambda i, j: (0,) * a.ndim)
    glu_w = glu_w.astype(BF16)
    w_out = w_out.astype(BF16)
    return pl.pallas_call(
        functools.partial(_even_out_kernel, n_ctx_tiles=n_ctx_tiles),
        grid=(b, t // tm),
        in_specs=[tok(ws), tok(ws), tok(ws), tok(ws), tok(w5), tok(w5), tok(w5)] + _stream_specs(tm, d, n_ctx_tiles) + [
            pl.BlockSpec((1, 1, 3, d), lambda i, j: (i, (j >= n_ctx_tiles).astype(jnp.int32), 0, 0)),
            const(vs), const(v5), const(glu_w), const(w_out),
        ],
        out_specs=tok(d),
        out_shape=jax.ShapeDtypeStruct((b, t, d), F32),
        compiler_params=_params(("parallel", "parallel")),
        name="even_out",
    )(yf, yb, xbc, z, y5, u, g, ctx, x, mod, vs, v5, glu_w, w_out)


def _odd_in_kernel(h_ref, mod_ref, w_ref, qg_ref, kg_ref, cos_ref, sin_ref, ones_ref, q_ref, k_ref, v_ref, g_ref, *,
                   q_w, kv_w):
    tm = TOKEN_TILE
    a = _norm_mod(h_ref[0], mod_ref[0, 0]).astype(BF16)
    cosv = cos_ref[...]
    sinv = sin_ref[...]
    first_half = (lax.broadcasted_iota(jnp.int32, (tm, 128), 1) % (HEAD_DIM // 2)) < (HEAD_DIM // 4)

    def project(lo, width=256):
        return jnp.dot(a, w_ref[:, lo:lo + width], preferred_element_type=F32)

    def head_norm_rope(x, gain, out_ref, c, transposed):
        ms = jnp.dot((x * x).astype(BF16), ones_ref[...], preferred_element_type=F32) * (1.0 / HEAD_DIM)
        xn = x * lax.rsqrt(ms + NORM_EPS) * gain
        for s in range(2):
            xb = xn[:, 128 * s:128 * (s + 1)]
            partner = jnp.where(first_half, pltpu.roll(xb, 128 - HEAD_DIM // 4, 1), pltpu.roll(xb, HEAD_DIM // 4, 1))
            r = xb * cosv + partner * sinv
            head = 4 * c + 2 * s
            if transposed:
                rt = r.T
                out_ref[0, head] = rt[:HEAD_DIM].astype(out_ref.dtype)
                out_ref[0, head + 1] = rt[HEAD_DIM:].astype(out_ref.dtype)
            else:
                out_ref[0, head] = r[:, :HEAD_DIM].astype(out_ref.dtype)
                out_ref[0, head + 1] = pltpu.roll(r, HEAD_DIM, 1)[:, :HEAD_DIM].astype(out_ref.dtype)

    chunks = [(256 * c, qg_ref, q_ref, c) for c in range(q_w // 256)] + \
             [(q_w + 256 * c, kg_ref, k_ref, c) for c in range(kv_w // 256)]
    g_lo = q_w + 2 * kv_w
    g_w = w_ref.shape[1] - g_lo
    x_next = project(chunks[0][0])
    for i, (lo, gain_ref, out_ref, c) in enumerate(chunks):
        x = x_next
        if i + 1 < len(chunks):
            x_next = project(chunks[i + 1][0])
        if 256 * i < g_w:
            g_ref[0, :, 256 * i:256 * (i + 1)] = project(g_lo + 256 * i)
        head_norm_rope(x, gain_ref[:, 256 * c:256 * (c + 1)], out_ref, c, out_ref is q_ref)
    assert 256 * len(chunks) >= g_w
    lane = lax.broadcasted_iota(jnp.int32, (tm, 128), 1)
    vt_rows = v_ref.shape[2]
    for c in range(kv_w // 128):
        x = project(q_w + kv_w + 128 * c, 128)
        for s in range(2):
            xs = pltpu.roll(x, HEAD_DIM, 1) if s else x
            vh = jnp.where(lane < HEAD_DIM, xs, jnp.where(lane == HEAD_DIM, 1.0, 0.0))
            v_ref[0, 2 * c + s] = vh.T[:vt_rows].astype(v_ref.dtype)


def _rope_tables(n_ctx, n_lat):
    pairs = HEAD_DIM // 4
    pos = jnp.arange(n_lat)
    row = (pos // GRID_W).astype(F32)
    colp = (pos % GRID_W).astype(F32)
    inv = ROPE_THETA ** (-jnp.arange(pairs, dtype=F32) / pairs)
    lane = jnp.arange(128) % HEAD_DIM
    axis_is_col = (lane // (HEAD_DIM // 2)) == 1
    ang = jnp.where(axis_is_col[None, :], colp[:, None], row[:, None]) * inv[lane % pairs][None, :]
    sign = jnp.where((lane % (HEAD_DIM // 2)) < pairs, -1.0, 1.0)
    cos = jnp.concatenate([jnp.ones((n_ctx, 128), F32), jnp.cos(ang)], axis=0)
    sin = jnp.concatenate([jnp.zeros((n_ctx, 128), F32), jnp.sin(ang) * sign[None, :]], axis=0)
    return cos, sin


def _odd_in(h, mod, w, q_gain, k_gain, cos, sin, n_ctx_tiles):
    b, t, d = h.shape
    tm = TOKEN_TILE
    q_w = ATTN_Q_HEADS * HEAD_DIM
    kv_w = ATTN_KV_HEADS * HEAD_DIM
    qg = (jnp.tile(q_gain, ATTN_Q_HEADS) * (HEAD_DIM ** -0.5 * math.log2(math.e))).reshape(1, q_w)
    kg = jnp.tile(k_gain, ATTN_KV_HEADS).reshape(1, kv_w)
    blk = jnp.arange(256) // HEAD_DIM
    ones = (blk[:, None] == blk[None, :]).astype(BF16)
    tok = lambda n: pl.BlockSpec((1, tm, n), lambda i, j: (i, j, 0))
    heads = lambda nh, n: pl.BlockSpec((1, nh, tm, n), lambda i, j: (i, 0, j, 0))
    heads_t = lambda nh, n: pl.BlockSpec((1, nh, n, tm), lambda i, j: (i, 0, 0, j))
    const = lambda a: pl.BlockSpec(a.shape, lambda i, j: (0,) * a.ndim)
    return pl.pallas_call(
        functools.partial(_odd_in_kernel, q_w=q_w, kv_w=kv_w),
        grid=(b, t // tm),
        in_specs=[
            tok(d),
            pl.BlockSpec((1, 1, 3, d), lambda i, j: (i, (j >= n_ctx_tiles).astype(jnp.int32), 0, 0)),
            const(w), const(qg), const(kg),
            pl.BlockSpec((tm, 128), lambda i, j: (j, 0)),
            pl.BlockSpec((tm, 128), lambda i, j: (j, 0)),
            const(ones),
        ],
        out_specs=[
            pl.BlockSpec((1, ATTN_Q_HEADS, HEAD_DIM, tm), lambda i, j: (i, 0, 0, jnp.maximum(j - n_ctx_tiles, 0))),
            heads(ATTN_KV_HEADS, HEAD_DIM), heads_t(ATTN_KV_HEADS, VT_ROWS), tok(q_w)],
        out_shape=[
            jax.ShapeDtypeStruct((b, ATTN_Q_HEADS, HEAD_DIM, t - n_ctx_tiles * tm), BF16),
            jax.ShapeDtypeStruct((b, ATTN_KV_HEADS, t, HEAD_DIM), BF16),
            jax.ShapeDtypeStruct((b, ATTN_KV_HEADS, VT_ROWS, t), BF16),
            jax.ShapeDtypeStruct((b, t, q_w), F32),
        ],
        compiler_params=_params(("parallel", "arbitrary")),
        name="odd_in",
    )(h, mod, w, qg, kg, cos, sin, ones)


def _attn_kernel(q_ref, k_ref, v_ref, o_ref, *, tk):
    rep, hd, tq = q_ref.shape[1:]
    rows = v_ref.shape[2]
    nk = k_ref.shape[2] // tk
    m = [jnp.full((1, tq), -1e30, F32)] * rep
    acc = [jnp.zeros((rows, tq), F32)] * rep
    blocks = [(j, r) for j in range(nk) for r in range(rep)]
    scores = {}
    for i in range(len(blocks) + ATTN_LOOKAHEAD):
        if i < len(blocks):
            j, r = blocks[i]
            scores[i] = jnp.dot(k_ref[0, 0, j * tk:(j + 1) * tk, :], q_ref[0, r], preferred_element_type=F32)
        if i >= ATTN_LOOKAHEAD:
            j, r = blocks[i - ATTN_LOOKAHEAD]
            s = scores.pop(i - ATTN_LOOKAHEAD)
            m_new = jnp.maximum(m[r], jnp.max(s, axis=0, keepdims=True))
            p = jnp.exp2(s - m_new).astype(BF16)
            pv = jnp.dot(v_ref[0, 0, :, j * tk:(j + 1) * tk], p, preferred_element_type=F32)
            acc[r] = jnp.exp2(m[r] - m_new) * acc[r] + pv
            m[r] = m_new
    for c in range(rep // 2):
        pair = [acc[r][:hd] * (1.0 / acc[r][hd:hd + 1]) for r in (2 * c, 2 * c + 1)]
        o_ref[0, :, 2 * hd * c:2 * hd * (c + 1)] = jnp.concatenate(pair, axis=0).T.astype(o_ref.dtype)


def _attention(q, k, v):
    b, hq, hd, n_lat = q.shape
    hkv, t = k.shape[1:3]
    rep = hq // hkv
    tq = ATTN_Q_TILE
    tk = next(c for c in (256, 128) if t % c == 0)
    assert n_lat % tq == 0
    return pl.pallas_call(
        functools.partial(_attn_kernel, tk=tk),
        grid=(b, hkv, n_lat // tq),
        in_specs=[
            pl.BlockSpec((1, rep, hd, tq), lambda i, j, n: (i, j, 0, n)),
            pl.BlockSpec((1, 1, t, hd), lambda i, j, n: (i, j, 0, 0)),
            pl.BlockSpec((1, 1, v.shape[2], t), lambda i, j, n: (i, j, 0, 0)),
        ],
        out_specs=pl.BlockSpec((1, tq, rep * hd), lambda i, j, n: (i, n, j)),
        out_shape=jax.ShapeDtypeStruct((b, n_lat, hq * hd), BF16),
        compiler_params=_params(("parallel", "parallel", "parallel")),
        name="attention",
    )(q, k, v)


def _attn_out_kernel(o_ref, g_ref, h_ref, mod_ref, w_ref, fg_ref, out_ref):
    x = o_ref[0].astype(F32) * _silu(g_ref[0])
    y = jnp.dot(x.astype(BF16), w_ref[...], preferred_element_type=F32)
    hn = h_ref[0] + mod_ref[0, 0][2:3] * y
    out_ref[0] = _rms(hn) * fg_ref[...]


def _attn_out(o, g, h, mod, w_out, final_gain, n_ctx_tiles):
    b, n_lat, d = o.shape
    tm = TOKEN_TILE
    lat = lambda n: pl.BlockSpec((1, tm, n), lambda i, j: (i, j + n_ctx_tiles, 0))
    w_out = w_out.astype(BF16)
    return pl.pallas_call(
        _attn_out_kernel,
        grid=(b, n_lat // tm),
        in_specs=[
            pl.BlockSpec((1, tm, d), lambda i, j: (i, j, 0)),
            lat(d), lat(d),
            pl.BlockSpec((1, 1, 3, d), lambda i, j: (i, 1, 0, 0)),
            pl.BlockSpec(w_out.shape, lambda i, j: (0, 0)),
            pl.BlockSpec((1, d), lambda i, j: (0, 0)),
        ],
        out_specs=pl.BlockSpec((1, tm, d), lambda i, j: (i, j, 0)),
        out_shape=jax.ShapeDtypeStruct((b, n_lat, d), F32),
        compiler_params=_params(("parallel", "parallel")),
        name="attn_out",
    )(o, g, h, mod, w_out, final_gain.reshape(1, d))


def kernel(x, c, ctx, c_ctx, ada_w, ada_b, ev_w_in, ev_conv_w, ev_conv_b, ev_dt_bias, ev_a_log, ev_d_ssd, ev_ssd_norm, ev_lam_re, ev_lam_im, ev_log_step, ev_b_re, ev_b_im, ev_c_re, ev_c_im, ev_d_s5, ev_glu_w, ev_glu_b, ev_w_out, od_w_in, od_q_gain, od_k_gain, od_w_out, final_gain):
    b, n_lat, d = x.shape
    n_ctx = ctx.shape[1]
    assert ada_w.shape[0] == 2 and n_ctx % TOKEN_TILE == 0 and n_lat % TOKEN_TILE == 0
    n_ctx_tiles = n_ctx // TOKEN_TILE
    mods = _adaln(c, c_ctx, ada_w, ada_b)

    ws = SSD_HEADS * HEAD_DIM
    wx = ws + 2 * SSD_GROUPS * SSD_STATE
    w5 = ev_d_s5.shape[-1]
    w = ev_w_in[0]
    cuts = (ws, ws + wx, ws + wx + 2 * SSD_HEADS, ws + wx + 2 * SSD_HEADS + w5)
    w_dt = jnp.pad(w[:, cuts[1]:cuts[2]], ((0, 0), (0, 128 - 2 * SSD_HEADS)))
    w_dt_hi, w_dt_lo = _split_bf16(w_dt)
    w_cat = jnp.concatenate(
        [w[:, :cuts[1]].astype(BF16), w[:, cuts[2]:].astype(BF16), w_dt_hi, w_dt_lo], axis=1)
    z, xbc, u, g, dt = _even_in(ctx, x, mods[0], w_cat, n_ctx_tiles, (ws, wx, w5, w5))
    xbc = _conv(xbc, ev_conv_w[0], ev_conv_b[0], n_ctx_tiles)
    yf, yb = _ssd(xbc, dt, ev_dt_bias[0], ev_a_log[0], n_ctx)
    ops = _s5_operators(ev_lam_re[0], ev_lam_im[0], ev_log_step[0], ev_b_re[0], ev_b_im[0], ev_c_re[0], ev_c_im[0])
    y5 = _s5(u, ops, n_ctx // S5_BLOCK)
    h = _even_out(yf, yb, xbc, z, y5, u, g, ctx, x, mods[0], ev_ssd_norm[0], ev_d_ssd[0], ev_d_s5[0], ev_glu_w[0],
                  ev_glu_b[0], ev_w_out[0], n_ctx_tiles)

    cos, sin = _rope_tables(n_ctx, n_lat)
    q, k, v, g = _odd_in(h, mods[1], od_w_in[0].astype(BF16), od_q_gain[0], od_k_gain[0], cos, sin, n_ctx_tiles)
    o = _attention(q, k, v)
    return _attn_out(o, g, h, mods[1], od_w_out[0], final_gain, n_ctx_tiles)
```

```python
import functools
import math

import jax
import jax.numpy as jnp
from jax import lax
from jax.experimental import pallas as pl
from jax.experimental.pallas import tpu as pltpu

F32 = jnp.float32
BF16 = jnp.bfloat16
HIGHEST = lax.Precision.HIGHEST

NORM_EPS = 1e-6
GRID_W = 64
ROPE_THETA = 10000.0

HEAD_DIM = 64
SSD_HEADS = 16
SSD_GROUPS = 4
SSD_STATE = 128
SSD_CHUNK = 128
SSD_CONV = 5
S5_GROUP = 16
S5_STATE = 64
S5_BLOCK = 16
ATTN_Q_HEADS = 16
ATTN_KV_HEADS = 4
VT_ROWS = 80
ATTN_Q_TILE = 256
ATTN_LOOKAHEAD = 8

TOKEN_TILE = 256
HALO_ROWS = 8
VMEM_LIMIT = 56 << 20


def _params(semantics):
    return pltpu.CompilerParams(dimension_semantics=semantics, vmem_limit_bytes=VMEM_LIMIT)


def _sigmoid(x):
    return 1.0 / (1.0 + jnp.exp(-x))


def _silu(x):
    return x * _sigmoid(x)


def _softplus(x):
    return jnp.maximum(x, 0.0) + jnp.log(1.0 + jnp.exp(-jnp.abs(x)))


def _gelu_tanh(x):
    return 0.5 * x * (1.0 + jnp.tanh(math.sqrt(2.0 / math.pi) * (x + 0.044715 * (x * x * x))))


def _rms(x):
    return x * lax.rsqrt(jnp.mean(x * x, axis=-1, keepdims=True) + NORM_EPS)


def _norm_mod(h, mod):
    return _rms(h) * (1.0 + mod[1:2]) + mod[0:1]


def _split_bf16(x):
    hi = x.astype(BF16)
    return hi, (x - hi.astype(F32)).astype(BF16)


def _adaln_kernel(s_ref, w_ref, b_ref, o_ref):
    s = _silu(s_ref[...])
    o_ref[0] = jnp.dot(s, w_ref[0], preferred_element_type=F32, precision=HIGHEST) + b_ref[0]


def _adaln(c, c_ctx, ada_w, ada_b):
    depth, d, d3 = ada_w.shape
    b = c.shape[0]
    assert b < 8
    rows = jnp.concatenate([c, c_ctx[None], jnp.zeros((7 - b, d), F32)], axis=0)
    out = pl.pallas_call(
        _adaln_kernel,
        grid=(depth, d3 // d),
        in_specs=[
            pl.BlockSpec((8, d), lambda i, j: (0, 0)),
            pl.BlockSpec((1, d, d), lambda i, j: (i, 0, j)),
            pl.BlockSpec((1, 1, d), lambda i, j: (i, 0, j)),
        ],
        out_specs=pl.BlockSpec((1, 8, d), lambda i, j: (i, 0, j)),
        out_shape=jax.ShapeDtypeStruct((depth, 8, d3), F32),
        compiler_params=_params(("arbitrary", "arbitrary")),
        name="adaln",
    )(rows, ada_w, ada_b.reshape(depth, 1, d3))
    m = out.reshape(depth, 8, 3, d)
    lat = m[:, :b]
    ctx = jnp.broadcast_to(m[:, b:b + 1], lat.shape)
    return jnp.stack([ctx, lat], axis=2)


def _stream_tile(ctx_ref, x_ref, n_ctx_tiles):
    return jnp.where(pl.program_id(1) < n_ctx_tiles, ctx_ref[0], x_ref[0])


def _stream_specs(tm, d, n_ctx_tiles):
    return [
        pl.BlockSpec((1, tm, d), lambda i, j: (i, jnp.minimum(j, n_ctx_tiles - 1), 0)),
        pl.BlockSpec((1, tm, d), lambda i, j: (i, jnp.maximum(j - n_ctx_tiles, 0), 0)),
    ]


def _even_in_kernel(ctx_ref, x_ref, mod_ref, w_ref, z_ref, xbc_ref, u_ref, g_ref, dt_ref, *, cuts, n_ctx_tiles):
    a = _norm_mod(_stream_tile(ctx_ref, x_ref, n_ctx_tiles), mod_ref[0, 0])
    a_hi, a_lo = _split_bf16(a)

    def mm(x, lo, hi):
        return jnp.dot(x, w_ref[:, lo:hi], preferred_element_type=F32)

    c0, c1, c2, c3, c4, c5 = cuts
    z_ref[0] = mm(a_hi, 0, c0).astype(z_ref.dtype)
    xbc_ref[0] = mm(a_hi, c0, c1)
    u_ref[0] = mm(a_hi, c1, c2)
    g_ref[0] = mm(a_hi, c2, c3).astype(g_ref.dtype)
    dt_ref[0] = mm(a_hi, c3, c4) + mm(a_lo, c3, c4) + mm(a_hi, c4, c5)


def _even_in(ctx, x, mod, w, n_ctx_tiles, widths):
    b, n_lat, d = x.shape
    t = ctx.shape[1] + n_lat
    tm = TOKEN_TILE
    wz, wx, wu, wg = widths
    cuts = (wz, wz + wx, wz + wx + wu, wz + wx + wu + wg, wz + wx + wu + wg + 128, wz + wx + wu + wg + 256)
    assert w.shape == (d, cuts[-1])
    tok = lambda n: pl.BlockSpec((1, tm, n), lambda i, j: (i, j, 0))
    return pl.pallas_call(
        functools.partial(_even_in_kernel, cuts=cuts, n_ctx_tiles=n_ctx_tiles),
        grid=(b, t // tm),
        in_specs=_stream_specs(tm, d, n_ctx_tiles) + [
            pl.BlockSpec((1, 1, 3, d), lambda i, j: (i, (j >= n_ctx_tiles).astype(jnp.int32), 0, 0)),
            pl.BlockSpec(w.shape, lambda i, j: (0, 0)),
        ],
        out_specs=[tok(wz), tok(wx), tok(wu), tok(wg), tok(128)],
        out_shape=[
            jax.ShapeDtypeStruct((b, t, wz), BF16),
            jax.ShapeDtypeStruct((b, t, wx), F32),
            jax.ShapeDtypeStruct((b, t, wu), F32),
            jax.ShapeDtypeStruct((b, t, wg), BF16),
            jax.ShapeDtypeStruct((b, t, 128), F32),
        ],
        compiler_params=_params(("parallel", "parallel")),
        name="even_in",
    )(ctx, x, mod, w)


def _conv_kernel(prev_ref, main_ref, next_ref, w_ref, b_ref, o_ref, xe_ref, *, n_ctx_tiles, n_tiles):
    tm = TOKEN_TILE
    ch = o_ref.shape[-1]
    t = pl.program_id(1)
    has_prev = jnp.logical_and(t != 0, t != n_ctx_tiles)
    has_next = jnp.logical_and(t != n_ctx_tiles - 1, t != n_tiles - 1)
    xe_ref[0:HALO_ROWS] = jnp.where(has_prev, prev_ref[0], 0.0)
    xe_ref[HALO_ROWS:HALO_ROWS + tm] = main_ref[0]
    xe_ref[HALO_ROWS + tm:2 * HALO_ROWS + tm] = jnp.where(has_next, next_ref[0], 0.0)
    rows = tm + 2 * HALO_ROWS
    half = SSD_CONV // 2
    lanes = 128
    for c0 in range(0, ch, lanes):
        cs = slice(c0, c0 + lanes)
        xe = xe_ref[:, cs]
        acc = b_ref[:, cs] + w_ref[half:half + 1, cs] * xe[HALO_ROWS:HALO_ROWS + tm]
        for k in range(SSD_CONV):
            if k != half:
                shifted = pltpu.roll(xe, (half - k) % rows, 0)
                acc = acc + w_ref[k:k + 1, cs] * shifted[HALO_ROWS:HALO_ROWS + tm]
        o_ref[0, :, cs] = _silu(acc)


def _conv(xbc, conv_w, conv_b, n_ctx_tiles):
    b, t, ch = xbc.shape
    tm = TOKEN_TILE
    n_tiles = t // tm
    per = tm // HALO_ROWS
    last = t // HALO_ROWS - 1
    w = jnp.pad(conv_w.T, ((0, 8 - SSD_CONV), (0, 0)))
    return pl.pallas_call(
        functools.partial(_conv_kernel, n_ctx_tiles=n_ctx_tiles, n_tiles=n_tiles),
        grid=(b, n_tiles),
        in_specs=[
            pl.BlockSpec((1, HALO_ROWS, ch), lambda i, j: (i, jnp.maximum(j * per - 1, 0), 0)),
            pl.BlockSpec((1, tm, ch), lambda i, j: (i, j, 0)),
            pl.BlockSpec((1, HALO_ROWS, ch), lambda i, j: (i, jnp.minimum((j + 1) * per, last), 0)),
            pl.BlockSpec((8, ch), lambda i, j: (0, 0)),
            pl.BlockSpec((1, ch), lambda i, j: (0, 0)),
        ],
        out_specs=pl.BlockSpec((1, tm, ch), lambda i, j: (i, j, 0)),
        out_shape=jax.ShapeDtypeStruct((b, t, ch), F32),
        scratch_shapes=[pltpu.VMEM((tm + 2 * HALO_ROWS, ch), F32)],
        compiler_params=_params(("parallel", "parallel")),
        name="conv",
    )(xbc, xbc, xbc, w, conv_b.reshape(1, ch))


def _ssd_kernel(xf_ref, xb_ref, dtf_ref, dtb_ref, bias_ref, a_ref, e_ref, yf_ref, yb_ref, st_ref):
    L = SSD_CHUNK
    width = SSD_HEADS * HEAD_DIM
    gw = width // SSD_GROUPS
    hpg = SSD_HEADS // SSD_GROUPS

    @pl.when(pl.program_id(1) == 0)
    def _():
        st_ref[...] = jnp.zeros_like(st_ref)

    row = lax.broadcasted_iota(jnp.int32, (L, L), 0)
    col = lax.broadcasted_iota(jnp.int32, (L, L), 1)
    lane_head = lax.broadcasted_iota(jnp.int32, (L, gw), 1) // HEAD_DIM
    sub16 = lax.broadcasted_iota(jnp.int32, (16, 128), 0)

    def split3(x):
        x1 = x.astype(BF16)
        r = x - x1.astype(F32)
        x2 = r.astype(BF16)
        return x1, x2, (r - x2.astype(F32)).astype(BF16)

    def chunk(d, x_ref, dt_ref, y_ref, r0):
        rows = pl.ds(r0, L)
        mask = (row >= col) if d == 0 else (row <= col)
        dtv = _softplus(dt_ref[0, rows, :] + bias_ref[...])
        loga = dtv * a_ref[...]
        tri = jnp.where(mask, 1.0, 0.0).astype(BF16)
        cs = sum(jnp.dot(tri, part, preferred_element_type=F32) for part in split3(loga))
        cs_t = cs.T
        total = jnp.sum(loga, axis=0, keepdims=True)
        t1, t2, t3 = (part.astype(F32) for part in split3(total))
        tot = jnp.where(sub16 == 0, t1, jnp.where(sub16 == 1, t2, jnp.where(sub16 == 2, t3, 0.0)))
        stack = jnp.concatenate(
            [dtv.astype(BF16), jnp.exp(cs).astype(BF16), jnp.exp(total - cs).astype(BF16), tot.astype(BF16)], axis=0)
        ex = jnp.dot(stack, e_ref[d], preferred_element_type=F32)
        dt_x = ex[0:L]
        ecs_x = ex[L:2 * L]
        wend_x = ex[2 * L:3 * L]
        edec_x = jnp.exp(ex[3 * L:3 * L + 1] + ex[3 * L + 1:3 * L + 2] + ex[3 * L + 2:3 * L + 3])

        groups = []
        for g in range(SSD_GROUPS):
            sl = slice(g * gw, (g + 1) * gw)
            xs_g = x_ref[0, rows, sl]
            b_g = x_ref[0, rows, width + g * SSD_STATE:width + (g + 1) * SSD_STATE]
            c_g = x_ref[0, rows, width + (SSD_GROUPS + g) * SSD_STATE:width + (SSD_GROUPS + g + 1) * SSD_STATE]
            cb = c_g.astype(BF16)
            xd = xs_g * dt_x[:, sl]
            gram = lax.dot_general(cb, b_g.astype(BF16), (((1,), (1,)), ((), ())), preferred_element_type=F32)
            scores = []
            for hh in range(hpg):
                li = SSD_HEADS * d + hpg * g + hh
                seg = cs[:, li:li + 1] - cs_t[li:li + 1, :]
                scores.append((gram * jnp.where(mask, jnp.exp(seg), 0.0)).astype(BF16))
            scores = jnp.concatenate(scores, axis=1)
            xd_blocks = jnp.concatenate(
                [jnp.where(lane_head == hh, xd, 0.0).astype(BF16) for hh in range(hpg)], axis=0)
            y = jnp.dot(scores, xd_blocks, preferred_element_type=F32)
            xdw = (xd * wend_x[:, sl]).astype(BF16)
            groups.append((y, cb, b_g.T.astype(BF16), xdw, ecs_x[:, sl], edec_x[:, sl]))

        def carry():
            for g, (y, cb, bt, xdw, ecs_g, edec_g) in enumerate(groups):
                sl = slice(g * gw, (g + 1) * gw)
                st = st_ref[d, :, sl]
                y_ref[0, rows, sl] = (y + jnp.dot(cb, st.astype(BF16), preferred_element_type=F32) * ecs_g
                                      ).astype(y_ref.dtype)
                st_ref[d, :, sl] = st * edec_g + jnp.dot(bt, xdw, preferred_element_type=F32)

        return carry

    carries = [chunk(0, xf_ref, dtf_ref, yf_ref, 0), chunk(1, xb_ref, dtb_ref, yb_ref, L),
               chunk(0, xf_ref, dtf_ref, yf_ref, L), chunk(1, xb_ref, dtb_ref, yb_ref, 0)]
    for carry in carries:
        carry()


def _ssd(xbc, dt, dt_bias, a_log, n_ctx):
    b, t, ch = xbc.shape
    L = 2 * SSD_CHUNK
    assert n_ctx % L == 0 and t % L == 0
    n = t // L
    width = SSD_HEADS * HEAD_DIM
    ncc = n_ctx // L

    def fwd(i, j):
        return (i, j, 0)

    def bwd(i, j):
        return (i, jnp.where(j < ncc, ncc - 1 - j, n + ncc - 1 - j), 0)

    pad = lambda v: jnp.pad(v.reshape(1, -1), ((0, 0), (0, 128 - v.size)))
    bias = pad(dt_bias)
    a_neg = pad(-jnp.exp(a_log))
    head_of_lane = jnp.arange(width) // HEAD_DIM
    expand = jnp.stack([
        (jnp.arange(128)[:, None] == SSD_HEADS * d + head_of_lane[None, :]) for d in range(2)
    ]).astype(BF16)
    return pl.pallas_call(
        _ssd_kernel,
        grid=(b, n),
        in_specs=[
            pl.BlockSpec((1, L, ch), fwd),
            pl.BlockSpec((1, L, ch), bwd),
            pl.BlockSpec((1, L, 128), fwd),
            pl.BlockSpec((1, L, 128), bwd),
            pl.BlockSpec((1, 128), lambda i, j: (0, 0)),
            pl.BlockSpec((1, 128), lambda i, j: (0, 0)),
            pl.BlockSpec((2, 128, width), lambda i, j: (0, 0, 0)),
        ],
        out_specs=[pl.BlockSpec((1, L, width), fwd), pl.BlockSpec((1, L, width), bwd)],
        out_shape=[jax.ShapeDtypeStruct((b, t, width), BF16)] * 2,
        scratch_shapes=[pltpu.VMEM((2, SSD_STATE, width), F32)],
        compiler_params=_params(("parallel", "arbitrary")),
        name="ssd",
    )(xbc, xbc, dt, dt, bias, a_neg, expand)


def _s5_matrices_kernel(z_ref, ak_ref, bb_ref, cc_ref, m_ref, smat_ref, ot_ref):
    nb, width = S5_BLOCK, S5_BLOCK * S5_GROUP
    lanes = z_ref.shape[-1]

    def rows(g, x_ref, kf, kb, conj):
        out = []
        for d, k in ((0, kf), (1, kb)):
            xr, xi = x_ref[g, 2 * d], x_ref[g, 2 * d + 1]
            ar, ai = ak_ref[g, 2 * d, k:k + 1, :], ak_ref[g, 2 * d + 1, k:k + 1, :]
            im = xr * ai + xi * ar
            out.append((xr * ar - xi * ai, -im if conj else im))
        (f_re, f_im), (b_re, b_im) = out
        c0, c1, c2, c3 = (f_re, b_re, f_im, b_im) if g % 2 == 0 else (b_re, f_re, b_im, f_im)
        return jnp.concatenate([c0 + pltpu.roll(c1, S5_STATE, 1), c2 + pltpu.roll(c3, S5_STATE, 1)], axis=1)

    for g in range(z_ref.shape[0]):
        z = z_ref[g]
        for s in range(nb):
            blk = slice(s * S5_GROUP, (s + 1) * S5_GROUP)
            off = (nb - 1 - s) * S5_GROUP
            win = pltpu.roll(z, (lanes - off) % lanes, 1) if off else z
            m_ref[g, blk, :] = win[:, :width].astype(m_ref.dtype)
            smat_ref[g, blk, :] = rows(g, bb_ref, nb - 1 - s, s, False).astype(smat_ref.dtype)
            ot_ref[g, blk, :] = rows(g, cc_ref, s + 1, nb - s, True).astype(ot_ref.dtype)


def _s5_matrices(zall, ak, bb, cc):
    ng, nj, w = zall.shape
    lanes = -(-w // 128) * 128
    per = 8
    width = S5_BLOCK * S5_GROUP
    blk = lambda a: pl.BlockSpec((per,) + a.shape[1:], lambda i: (i,) + (0,) * (a.ndim - 1))
    zall = jnp.pad(zall, ((0, 0), (0, 0), (0, lanes - w)))
    out = jax.ShapeDtypeStruct((ng, S5_BLOCK * nj, width), BF16)
    return pl.pallas_call(
        _s5_matrices_kernel,
        grid=(ng // per,),
        in_specs=[blk(zall), blk(ak), blk(bb), blk(cc)],
        out_specs=[pl.BlockSpec((per, S5_BLOCK * nj, width), lambda i: (i, 0, 0))] * 3,
        out_shape=[out] * 3,
        compiler_params=_params(("parallel",)),
        name="s5_matrices",
    )(zall, ak, bb, cc)


def _s5_operators(lam_re, lam_im, log_step, b_re, b_im, c_re, c_im):
    nb = S5_BLOCK
    ng, ns = lam_re.shape[1:]
    hp = functools.partial(jnp.einsum, precision=HIGHEST)
    step = jnp.exp(log_step)[..., None]
    k = jnp.arange(nb + 1, dtype=F32)[:, None, None, None]
    mag = jnp.exp(k * (lam_re * step))
    ak_re = mag * jnp.cos(k * (lam_im * step))
    ak_im = mag * jnp.sin(k * (lam_im * step))
    ab_re, ab_im = ak_re[1], ak_im[1]
    den = lam_re * lam_re + lam_im * lam_im
    f_re = ((ab_re - 1.0) * lam_re + ab_im * lam_im) / den
    f_im = (ab_im * lam_re - (ab_re - 1.0) * lam_im) / den
    tr = lambda a: jnp.swapaxes(a, -1, -2)
    bb_re = tr(f_re[..., None] * b_re - f_im[..., None] * b_im)
    bb_im = tr(f_re[..., None] * b_im + f_im[..., None] * b_re)
    akr, aki = ak_re[:nb, :, :, None, :], ak_im[:nb, :, :, None, :]
    w_re = akr * bb_re - aki * bb_im
    w_im = akr * bb_im + aki * bb_re
    kern = hp('kdgjp,dgip->kdgji', w_re, c_re) - hp('kdgjp,dgip->kdgji', w_im, c_im)
    kf, kb = kern[:, 0], kern[:, 1]
    zall = jnp.concatenate([kb[:0:-1], (kf[0] + kb[0])[None], kf[1:]], axis=0).transpose(1, 2, 0, 3)
    comps = lambda re, im: jnp.stack([re[0], im[0], re[1], im[1]], axis=1)
    lane_pad = lambda a, r: jnp.pad(a, ((0, 0), (0, 0), (0, r - a.shape[2]), (0, 128 - ns)))
    ak = lane_pad(comps(ak_re.transpose(1, 2, 0, 3), ak_im.transpose(1, 2, 0, 3)), 8 * (-(-(nb + 1) // 8)))
    m, smat, ot = _s5_matrices(zall.reshape(ng, S5_GROUP, (2 * nb - 1) * S5_GROUP), ak,
                               lane_pad(comps(bb_re, bb_im), S5_GROUP), lane_pad(comps(c_re, c_im), S5_GROUP))
    swap = lambda a: a.reshape(ng // 2, 2, ns)[:, ::-1].reshape(ng * ns)
    dec = jnp.stack([ak_re[nb, 0].reshape(-1), ak_im[nb, 0].reshape(-1), swap(ak_re[nb, 1]), swap(ak_im[nb, 1])])
    return smat, m, ot, dec


def _s5_kernel(u_ref, smat_ref, m_ref, ot_ref, dec_ref, y_ref, us, uf, ys, sfr, sbr, sfi, sbi, hfr, hbr, hfi, hbi, *,
               n_ctx_blocks):
    nb = S5_BLOCK
    groups = smat_ref.shape[0]
    n = u_ref.shape[1] // nb
    ncb = n_ctx_blocks
    seg = S5_GROUP
    per_col = 128 // seg
    lane_seg = lax.broadcasted_iota(jnp.int32, (n, 128), 1) // seg
    low = lax.broadcasted_iota(jnp.int32, (n, 128), 1) < S5_STATE

    for s in range(nb):
        us[s] = u_ref[0, pl.ds(s, n, stride=nb), :]

    def fold(g):
        for v in range(nb // per_col):
            col = None
            for k in range(per_col):
                x = us[v * per_col + k]
                shift = (seg * (k - g)) % 128
                if shift:
                    x = pltpu.roll(x, shift, 1)
                col = x if col is None else jnp.where(lane_seg == k, x, col)
            uf[g, :, 128 * v:128 * (v + 1)] = col.astype(BF16)
        return jnp.dot(uf[g], smat_ref[g], preferred_element_type=F32)

    for p in range(groups // 2):
        se, so = fold(2 * p), fold(2 * p + 1)
        sl = slice(128 * p, 128 * (p + 1))
        sfr[:, sl] = jnp.where(low, se[:, :128], so[:, :128])
        sbr[:, sl] = jnp.where(low, so[:, :128], se[:, :128])
        sfi[:, sl] = jnp.where(low, se[:, 128:], so[:, 128:])
        sbi[:, sl] = jnp.where(low, so[:, 128:], se[:, 128:])
    arf, aif, arb, aib = dec_ref[0, 0:1], dec_ref[0, 1:2], dec_ref[0, 2:3], dec_ref[0, 3:4]

    def step(i, carry):
        fr, fi, br, bi = carry
        rf = i
        rb = jnp.where(i < ncb, ncb - 1 - i, n + ncb - 1 - i)
        hfr[pl.ds(rf, 1), :] = fr
        hfi[pl.ds(rf, 1), :] = fi
        hbr[pl.ds(rb, 1), :] = br
        hbi[pl.ds(rb, 1), :] = bi
        nfr = arf * fr - aif * fi + sfr[pl.ds(rf, 1), :]
        nfi = arf * fi + aif * fr + sfi[pl.ds(rf, 1), :]
        nbr = arb * br - aib * bi + sbr[pl.ds(rb, 1), :]
        nbi = arb * bi + aib * br + sbi[pl.ds(rb, 1), :]
        return nfr, nfi, nbr, nbi

    zero = jnp.zeros((1, 128 * (groups // 2)), F32)
    lax.fori_loop(0, n, step, (zero, zero, zero, zero))
    for g in range(groups):
        sl = slice(128 * (g // 2), 128 * (g // 2 + 1))
        first, second = ((hfr, hfi), (hbr, hbi)) if g % 2 == 0 else ((hbr, hbi), (hfr, hfi))
        states = jnp.concatenate(
            [jnp.where(low, a[:, sl], b[:, sl]).astype(BF16) for a, b in zip(first, second)], axis=1)
        ys[g] = (jnp.dot(uf[g], m_ref[g], preferred_element_type=F32)
                 + lax.dot_general(states, ot_ref[g], (((1,), (1,)), ((), ())), preferred_element_type=F32))
    for l in range(nb):
        v, k = divmod(l, per_col)
        out = None
        for g in range(groups):
            x = ys[g, :, 128 * v:128 * (v + 1)]
            shift = (seg * (g - k)) % 128
            if shift:
                x = pltpu.roll(x, shift, 1)
            out = x if out is None else jnp.where(lane_seg == g, x, out)
        y_ref[0, pl.ds(l, n, stride=nb), :] = out


def _s5(u, ops, n_ctx_blocks):
    smat, m, ow, dec = ops
    b, t, w = u.shape
    nb = S5_BLOCK
    n = t // nb
    gw = nb * S5_GROUP
    gpb = 128 // S5_GROUP
    steps = w // 128
    lanes = gpb * S5_STATE
    dec = dec.reshape(4, steps, lanes).transpose(1, 0, 2)
    dec = jnp.concatenate([dec, jnp.zeros_like(dec)], axis=1)
    return pl.pallas_call(
        functools.partial(_s5_kernel, n_ctx_blocks=n_ctx_blocks),
        grid=(b, steps),
        in_specs=[
            pl.BlockSpec((1, t, 128), lambda i, j: (i, 0, j)),
            pl.BlockSpec((gpb,) + smat.shape[1:], lambda i, j: (j, 0, 0)),
            pl.BlockSpec((gpb,) + m.shape[1:], lambda i, j: (j, 0, 0)),
            pl.BlockSpec((gpb,) + ow.shape[1:], lambda i, j: (j, 0, 0)),
            pl.BlockSpec((1, 8, lanes), lambda i, j: (j, 0, 0)),
        ],
        out_specs=pl.BlockSpec((1, t, 128), lambda i, j: (i, 0, j)),
        out_shape=jax.ShapeDtypeStruct((b, t, w), F32),
        scratch_shapes=[pltpu.VMEM((nb, n, 128), F32), pltpu.VMEM((gpb, n, gw), BF16), pltpu.VMEM((gpb, n, gw), F32)]
        + [pltpu.VMEM((n, lanes), F32)] * 8,
        compiler_params=_params(("parallel", "parallel")),
        name="s5",
    )(u, smat, m, ow, dec)


def _even_out_kernel(yf_ref, yb_ref, xs_ref, z_ref, y5_ref, u_ref, g_ref, ctx_ref, x_ref, mod_ref, vs_ref, v5_ref,
                     glu_ref, w_ref, o_ref, *, n_ctx_tiles):
    ws = z_ref.shape[-1]
    y = _gelu_tanh(y5_ref[0] + v5_ref[0:1] * u_ref[0])
    glu = jnp.dot(y.astype(BF16), glu_ref[...], preferred_element_type=F32)
    ys = yf_ref[0].astype(F32) + yb_ref[0].astype(F32) + vs_ref[1:2] * xs_ref[0]
    s = _rms(ys * _silu(z_ref[0].astype(F32))) * vs_ref[0:1]
    o = jnp.dot(s.astype(BF16), w_ref[0:ws], preferred_element_type=F32)
    y = y * _sigmoid(glu + v5_ref[1:2]) * _silu(g_ref[0].astype(F32))
    o = o + jnp.dot(y.astype(BF16), w_ref[ws:], preferred_element_type=F32)
    o_ref[0] = _stream_tile(ctx_ref, x_ref, n_ctx_tiles) + mod_ref[0, 0][2:3] * o


def _even_out(yf, yb, xbc, z, y5, u, g, ctx, x, mod, ssd_norm, d_ssd, d_s5, glu_w, glu_b, w_out, n_ctx_tiles):
    b, t = z.shape[:2]
    d = x.shape[-1]
    tm = TOKEN_TILE
    ws, w5 = z.shape[-1], u.shape[-1]
    vs = jnp.pad(jnp.stack([ssd_norm, jnp.repeat(d_ssd, HEAD_DIM)]), ((0, 6), (0, 0)))
    v5 = jnp.pad(jnp.stack([d_s5, glu_b]), ((0, 6), (0, 0)))
    tok = lambda n: pl.BlockSpec((1, tm, n), lambda i, j: (i, j, 0))
    const = lambda a: pl.BlockSpec(a.shape, lambda i, j: (0,) * a.ndim)
    glu_w = glu_w.astype(BF16)
    w_out = w_out.astype(BF16)
    return pl.pallas_call(
        functools.partial(_even_out_kernel, n_ctx_tiles=n_ctx_tiles),
        grid=(b, t // tm),
        in_specs=[tok(ws), tok(ws), tok(ws), tok(ws), tok(w5), tok(w5), tok(w5)] + _stream_specs(tm, d, n_ctx_tiles) + [
            pl.BlockSpec((1, 1, 3, d), lambda i, j: (i, (j >= n_ctx_tiles).astype(jnp.int32), 0, 0)),
            const(vs), const(v5), const(glu_w), const(w_out),
        ],
        out_specs=tok(d),
        out_shape=jax.ShapeDtypeStruct((b, t, d), F32),
        compiler_params=_params(("parallel", "parallel")),
        name="even_out",
    )(yf, yb, xbc, z, y5, u, g, ctx, x, mod, vs, v5, glu_w, w_out)


def _odd_in_kernel(h_ref, mod_ref, w_ref, qg_ref, kg_ref, cos_ref, sin_ref, ones_ref, q_ref, k_ref, v_ref, g_ref, *,
                   q_w, kv_w):
    tm = TOKEN_TILE
    a = _norm_mod(h_ref[0], mod_ref[0, 0]).astype(BF16)
    cosv = cos_ref[...]
    sinv = sin_ref[...]
    first_half = (lax.broadcasted_iota(jnp.int32, (tm, 128), 1) % (HEAD_DIM // 2)) < (HEAD_DIM // 4)

    def project(lo, width=256):
        return jnp.dot(a, w_ref[:, lo:lo + width], preferred_element_type=F32)

    def head_norm_rope(x, gain, out_ref, c, transposed):
        ms = jnp.dot((x * x).astype(BF16), ones_ref[...], preferred_element_type=F32) * (1.0 / HEAD_DIM)
        xn = x * lax.rsqrt(ms + NORM_EPS) * gain
        for s in range(2):
            xb = xn[:, 128 * s:128 * (s + 1)]
            partner = jnp.where(first_half, pltpu.roll(xb, 128 - HEAD_DIM // 4, 1), pltpu.roll(xb, HEAD_DIM // 4, 1))
            r = xb * cosv + partner * sinv
            head = 4 * c + 2 * s
            if transposed:
                rt = r.T
                out_ref[0, head] = rt[:HEAD_DIM].astype(out_ref.dtype)
                out_ref[0, head + 1] = rt[HEAD_DIM:].astype(out_ref.dtype)
            else:
                out_ref[0, head] = r[:, :HEAD_DIM].astype(out_ref.dtype)
                out_ref[0, head + 1] = pltpu.roll(r, HEAD_DIM, 1)[:, :HEAD_DIM].astype(out_ref.dtype)

    chunks = [(256 * c, qg_ref, q_ref, c) for c in range(q_w // 256)] + \
             [(q_w + 256 * c, kg_ref, k_ref, c) for c in range(kv_w // 256)]
    g_lo = q_w + 2 * kv_w
    g_w = w_ref.shape[1] - g_lo
    x_next = project(chunks[0][0])
    for i, (lo, gain_ref, out_ref, c) in enumerate(chunks):
        x = x_next
        if i + 1 < len(chunks):
            x_next = project(chunks[i + 1][0])
        if 256 * i < g_w:
            g_ref[0, :, 256 * i:256 * (i + 1)] = project(g_lo + 256 * i)
        head_norm_rope(x, gain_ref[:, 256 * c:256 * (c + 1)], out_ref, c, out_ref is q_ref)
    assert 256 * len(chunks) >= g_w
    lane = lax.broadcasted_iota(jnp.int32, (tm, 128), 1)
    vt_rows = v_ref.shape[2]
    for c in range(kv_w // 128):
        x = project(q_w + kv_w + 128 * c, 128)
        for s in range(2):
            xs = pltpu.roll(x, HEAD_DIM, 1) if s else x
            vh = jnp.where(lane < HEAD_DIM, xs, jnp.where(lane == HEAD_DIM, 1.0, 0.0))
            v_ref[0, 2 * c + s] = vh.T[:vt_rows].astype(v_ref.dtype)


def _rope_tables(n_ctx, n_lat):
    pairs = HEAD_DIM // 4
    pos = jnp.arange(n_lat)
    row = (pos // GRID_W).astype(F32)
    colp = (pos % GRID_W).astype(F32)
    inv = ROPE_THETA ** (-jnp.arange(pairs, dtype=F32) / pairs)
    lane = jnp.arange(128) % HEAD_DIM
    axis_is_col = (lane // (HEAD_DIM // 2)) == 1
    ang = jnp.where(axis_is_col[None, :], colp[:, None], row[:, None]) * inv[lane % pairs][None, :]
    sign = jnp.where((lane % (HEAD_DIM // 2)) < pairs, -1.0, 1.0)
    cos = jnp.concatenate([jnp.ones((n_ctx, 128), F32), jnp.cos(ang)], axis=0)
    sin = jnp.concatenate([jnp.zeros((n_ctx, 128), F32), jnp.sin(ang) * sign[None, :]], axis=0)
    return cos, sin


def _odd_in(h, mod, w, q_gain, k_gain, cos, sin, n_ctx_tiles):
    b, t, d = h.shape
    tm = TOKEN_TILE
    q_w = ATTN_Q_HEADS * HEAD_DIM
    kv_w = ATTN_KV_HEADS * HEAD_DIM
    qg = (jnp.tile(q_gain, ATTN_Q_HEADS) * (HEAD_DIM ** -0.5 * math.log2(math.e))).reshape(1, q_w)
    kg = jnp.tile(k_gain, ATTN_KV_HEADS).reshape(1, kv_w)
    blk = jnp.arange(256) // HEAD_DIM
    ones = (blk[:, None] == blk[None, :]).astype(BF16)
    tok = lambda n: pl.BlockSpec((1, tm, n), lambda i, j: (i, j, 0))
    heads = lambda nh, n: pl.BlockSpec((1, nh, tm, n), lambda i, j: (i, 0, j, 0))
    heads_t = lambda nh, n: pl.BlockSpec((1, nh, n, tm), lambda i, j: (i, 0, 0, j))
    const = lambda a: pl.BlockSpec(a.shape, lambda i, j: (0,) * a.ndim)
    return pl.pallas_call(
        functools.partial(_odd_in_kernel, q_w=q_w, kv_w=kv_w),
        grid=(b, t // tm),
        in_specs=[
            tok(d),
            pl.BlockSpec((1, 1, 3, d), lambda i, j: (i, (j >= n_ctx_tiles).astype(jnp.int32), 0, 0)),
            const(w), const(qg), const(kg),
            pl.BlockSpec((tm, 128), lambda i, j: (j, 0)),
            pl.BlockSpec((tm, 128), lambda i, j: (j, 0)),
            const(ones),
        ],
        out_specs=[
            pl.BlockSpec((1, ATTN_Q_HEADS, HEAD_DIM, tm), lambda i, j: (i, 0, 0, jnp.maximum(j - n_ctx_tiles, 0))),
            heads(ATTN_KV_HEADS, HEAD_DIM), heads_t(ATTN_KV_HEADS, VT_ROWS), tok(q_w)],
        out_shape=[
            jax.ShapeDtypeStruct((b, ATTN_Q_HEADS, HEAD_DIM, t - n_ctx_tiles * tm), BF16),
            jax.ShapeDtypeStruct((b, ATTN_KV_HEADS, t, HEAD_DIM), BF16),
            jax.ShapeDtypeStruct((b, ATTN_KV_HEADS, VT_ROWS, t), BF16),
            jax.ShapeDtypeStruct((b, t, q_w), F32),
        ],
        compiler_params=_params(("parallel", "arbitrary")),
        name="odd_in",
    )(h, mod, w, qg, kg, cos, sin, ones)


def _attn_kernel(q_ref, k_ref, v_ref, o_ref, *, tk):
    rep, hd, tq = q_ref.shape[1:]
    rows = v_ref.shape[2]
    nk = k_ref.shape[2] // tk
    m = [jnp.full((1, tq), -1e30, F32)] * rep
    acc = [jnp.zeros((rows, tq), F32)] * rep
    blocks = [(j, r) for j in range(nk) for r in range(rep)]
    scores = {}
    for i in range(len(blocks) + ATTN_LOOKAHEAD):
        if i < len(blocks):
            j, r = blocks[i]
            scores[i] = jnp.dot(k_ref[0, 0, j * tk:(j + 1) * tk, :], q_ref[0, r], preferred_element_type=F32)
        if i >= ATTN_LOOKAHEAD:
            j, r = blocks[i - ATTN_LOOKAHEAD]
            s = scores.pop(i - ATTN_LOOKAHEAD)
            m_new = jnp.maximum(m[r], jnp.max(s, axis=0, keepdims=True))
            p = jnp.exp2(s - m_new).astype(BF16)
            pv = jnp.dot(v_ref[0, 0, :, j * tk:(j + 1) * tk], p, preferred_element_type=F32)
            acc[r] = jnp.exp2(m[r] - m_new) * acc[r] + pv
            m[r] = m_new
    for c in range(rep // 2):
        pair = [acc[r][:hd] * (1.0 / acc[r][hd:hd + 1]) for r in (2 * c, 2 * c + 1)]
        o_ref[0, :, 2 * hd * c:2 * hd * (c + 1)] = jnp.concatenate(pair, axis=0).T.astype(o_ref.dtype)


def _attention(q, k, v):
    b, hq, hd, n_lat = q.shape
    hkv, t = k.shape[1:3]
    rep = hq // hkv
    tq = ATTN_Q_TILE
    tk = next(c for c in (256, 128) if t % c == 0)
    assert n_lat % tq == 0
    return pl.pallas_call(
        functools.partial(_attn_kernel, tk=tk),
        grid=(b, hkv, n_lat // tq),
        in_specs=[
            pl.BlockSpec((1, rep, hd, tq), lambda i, j, n: (i, j, 0, n)),
            pl.BlockSpec((1, 1, t, hd), lambda i, j, n: (i, j, 0, 0)),
            pl.BlockSpec((1, 1, v.shape[2], t), lambda i, j, n: (i, j, 0, 0)),
        ],
        out_specs=pl.BlockSpec((1, tq, rep * hd), lambda i, j, n: (i, n, j)),
        out_shape=jax.ShapeDtypeStruct((b, n_lat, hq * hd), BF16),
        compiler_params=_params(("parallel", "parallel", "parallel")),
        name="attention",
    )(q, k, v)


def _attn_out_kernel(o_ref, g_ref, h_ref, mod_ref, w_ref, fg_ref, out_ref):
    x = o_ref[0].astype(F32) * _silu(g_ref[0])
    y = jnp.dot(x.astype(BF16), w_ref[...], preferred_element_type=F32)
    hn = h_ref[0] + mod_ref[0, 0][2:3] * y
    out_ref[0] = _rms(hn) * fg_ref[...]


def _attn_out(o, g, h, mod, w_out, final_gain, n_ctx_tiles):
    b, n_lat, d = o.shape
    tm = TOKEN_TILE
    lat = lambda n: pl.BlockSpec((1, tm, n), lambda i, j: (i, j + n_ctx_tiles, 0))
    w_out = w_out.astype(BF16)
    return pl.pallas_call(
        _attn_out_kernel,
        grid=(b, n_lat // tm),
        in_specs=[
            pl.BlockSpec((1, tm, d), lambda i, j: (i, j, 0)),
            lat(d), lat(d),
            pl.BlockSpec((1, 1, 3, d), lambda i, j: (i, 1, 0, 0)),
            pl.BlockSpec(w_out.shape, lambda i, j: (0, 0)),
            pl.BlockSpec((1, d), lambda i, j: (0, 0)),
        ],
        out_specs=pl.BlockSpec((1, tm, d), lambda i, j: (i, j, 0)),
        out_shape=jax.ShapeDtypeStruct((b, n_lat, d), F32),
        compiler_params=_params(("parallel", "parallel")),
        name="attn_out",
    )(o, g, h, mod, w_out, final_gain.reshape(1, d))


def kernel(x, c, ctx, c_ctx, ada_w, ada_b, ev_w_in, ev_conv_w, ev_conv_b, ev_dt_bias, ev_a_log, ev_d_ssd, ev_ssd_norm, ev_lam_re, ev_lam_im, ev_log_step, ev_b_re, ev_b_im, ev_c_re, ev_c_im, ev_d_s5, ev_glu_w, ev_glu_b, ev_w_out, od_w_in, od_q_gain, od_k_gain, od_w_out, final_gain):
    b, n_lat, d = x.shape
    n_ctx = ctx.shape[1]
    assert ada_w.shape[0] == 2 and n_ctx % TOKEN_TILE == 0 and n_lat % TOKEN_TILE == 0
    n_ctx_tiles = n_ctx // TOKEN_TILE
    mods = _adaln(c, c_ctx, ada_w, ada_b)

    ws = SSD_HEADS * HEAD_DIM
    wx = ws + 2 * SSD_GROUPS * SSD_STATE
    w5 = ev_d_s5.shape[-1]
    w = ev_w_in[0]
    cuts = (ws, ws + wx, ws + wx + 2 * SSD_HEADS, ws + wx + 2 * SSD_HEADS + w5)
    w_dt = jnp.pad(w[:, cuts[1]:cuts[2]], ((0, 0), (0, 128 - 2 * SSD_HEADS)))
    w_dt_hi, w_dt_lo = _split_bf16(w_dt)
    w_cat = jnp.concatenate(
        [w[:, :cuts[1]].astype(BF16), w[:, cuts[2]:].astype(BF16), w_dt_hi, w_dt_lo], axis=1)
    z, xbc, u, g, dt = _even_in(ctx, x, mods[0], w_cat, n_ctx_tiles, (ws, wx, w5, w5))
    xbc = _conv(xbc, ev_conv_w[0], ev_conv_b[0], n_ctx_tiles)
    yf, yb = _ssd(xbc, dt, ev_dt_bias[0], ev_a_log[0], n_ctx)
    ops = _s5_operators(ev_lam_re[0], ev_lam_im[0], ev_log_step[0], ev_b_re[0], ev_b_im[0], ev_c_re[0], ev_c_im[0])
    y5 = _s5(u, ops, n_ctx // S5_BLOCK)
    h = _even_out(yf, yb, xbc, z, y5, u, g, ctx, x, mods[0], ev_ssd_norm[0], ev_d_ssd[0], ev_d_s5[0], ev_glu_w[0],
                  ev_glu_b[0], ev_w_out[0], n_ctx_tiles)

    cos, sin = _rope_tables(n_ctx, n_lat)
    q, k, v, g = _odd_in(h, mods[1], od_w_in[0].astype(BF16), od_q_gain[0], od_k_gain[0], cos, sin, n_ctx_tiles)
    o = _attention(q, k, v)
    return _attn_out(o, g, h, mods[1], od_w_out[0], final_gain, n_ctx_tiles)
```

```python
import functools
import math

import jax
import jax.numpy as jnp
from jax import lax
from jax.experimental import pallas as pl
from jax.experimental.pallas import tpu as pltpu

F32 = jnp.float32
BF16 = jnp.bfloat16
HIGHEST = lax.Precision.HIGHEST

NORM_EPS = 1e-6
GRID_W = 64
ROPE_THETA = 10000.0

HEAD_DIM = 64
SSD_HEADS = 16
SSD_GROUPS = 4
SSD_STATE = 128
SSD_CHUNK = 128
SSD_CONV = 5
S5_GROUP = 16
S5_STATE = 64
S5_BLOCK = 16
ATTN_Q_HEADS = 16
ATTN_KV_HEADS = 4
VT_ROWS = 80
ATTN_Q_TILE = 256
ATTN_LOOKAHEAD = 8

TOKEN_TILE = 256
HALO_ROWS = 8
VMEM_LIMIT = 56 << 20


def _params(semantics):
    return pltpu.CompilerParams(dimension_semantics=semantics, vmem_limit_bytes=VMEM_LIMIT)


def _sigmoid(x):
    return 1.0 / (1.0 + jnp.exp(-x))


def _silu(x):
    return x * _sigmoid(x)


def _softplus(x):
    return jnp.maximum(x, 0.0) + jnp.log(1.0 + jnp.exp(-jnp.abs(x)))


def _gelu_tanh(x):
    return 0.5 * x * (1.0 + jnp.tanh(math.sqrt(2.0 / math.pi) * (x + 0.044715 * (x * x * x))))


def _rms(x):
    return x * lax.rsqrt(jnp.mean(x * x, axis=-1, keepdims=True) + NORM_EPS)


def _norm_mod(h, mod):
    return _rms(h) * (1.0 + mod[1:2]) + mod[0:1]


def _split_bf16(x):
    hi = x.astype(BF16)
    return hi, (x - hi.astype(F32)).astype(BF16)


def _adaln_kernel(s_ref, w_ref, b_ref, o_ref):
    s = _silu(s_ref[...])
    o_ref[0] = jnp.dot(s, w_ref[0], preferred_element_type=F32, precision=HIGHEST) + b_ref[0]


def _adaln(c, c_ctx, ada_w, ada_b):
    depth, d, d3 = ada_w.shape
    b = c.shape[0]
    assert b < 8
    rows = jnp.concatenate([c, c_ctx[None], jnp.zeros((7 - b, d), F32)], axis=0)
    out = pl.pallas_call(
        _adaln_kernel,
        grid=(depth, d3 // d),
        in_specs=[
            pl.BlockSpec((8, d), lambda i, j: (0, 0)),
            pl.BlockSpec((1, d, d), lambda i, j: (i, 0, j)),
            pl.BlockSpec((1, 1, d), lambda i, j: (i, 0, j)),
        ],
        out_specs=pl.BlockSpec((1, 8, d), lambda i, j: (i, 0, j)),
        out_shape=jax.ShapeDtypeStruct((depth, 8, d3), F32),
        compiler_params=_params(("arbitrary", "arbitrary")),
        name="adaln",
    )(rows, ada_w, ada_b.reshape(depth, 1, d3))
    m = out.reshape(depth, 8, 3, d)
    lat = m[:, :b]
    ctx = jnp.broadcast_to(m[:, b:b + 1], lat.shape)
    return jnp.stack([ctx, lat], axis=2)


def _stream_tile(ctx_ref, x_ref, n_ctx_tiles):
    return jnp.where(pl.program_id(1) < n_ctx_tiles, ctx_ref[0], x_ref[0])


def _stream_specs(tm, d, n_ctx_tiles):
    return [
        pl.BlockSpec((1, tm, d), lambda i, j: (i, jnp.minimum(j, n_ctx_tiles - 1), 0)),
        pl.BlockSpec((1, tm, d), lambda i, j: (i, jnp.maximum(j - n_ctx_tiles, 0), 0)),
    ]


def _even_in_kernel(ctx_ref, x_ref, mod_ref, w_ref, z_ref, xbc_ref, u_ref, g_ref, dt_ref, *, cuts, n_ctx_tiles):
    a = _norm_mod(_stream_tile(ctx_ref, x_ref, n_ctx_tiles), mod_ref[0, 0])
    a = a.astype(BF16)

    def mm(lo, hi):
        return jnp.dot(a, w_ref[:, lo:hi], preferred_element_type=F32)

    c0, c1, c2, c3, c4, c5 = cuts
    z_ref[0] = mm(0, c0).astype(z_ref.dtype)
    xbc_ref[0] = mm(c0, c1)
    u_ref[0] = mm(c1, c2)
    g_ref[0] = mm(c2, c3).astype(g_ref.dtype)
    dt = mm(c3, c5)
    dt_ref[0] = dt[:, :c4 - c3] + dt[:, c4 - c3:]


def _even_in(ctx, x, mod, w, n_ctx_tiles, widths):
    b, n_lat, d = x.shape
    t = ctx.shape[1] + n_lat
    tm = TOKEN_TILE
    wz, wx, wu, wg = widths
    cuts = (wz, wz + wx, wz + wx + wu, wz + wx + wu + wg, wz + wx + wu + wg + 128, wz + wx + wu + wg + 256)
    assert w.shape == (d, cuts[-1])
    tok = lambda n: pl.BlockSpec((1, tm, n), lambda i, j: (i, j, 0))
    return pl.pallas_call(
        functools.partial(_even_in_kernel, cuts=cuts, n_ctx_tiles=n_ctx_tiles),
        grid=(b, t // tm),
        in_specs=_stream_specs(tm, d, n_ctx_tiles) + [
            pl.BlockSpec((1, 1, 3, d), lambda i, j: (i, (j >= n_ctx_tiles).astype(jnp.int32), 0, 0)),
            pl.BlockSpec(w.shape, lambda i, j: (0, 0)),
        ],
        out_specs=[tok(wz), tok(wx), tok(wu), tok(wg), tok(128)],
        out_shape=[
            jax.ShapeDtypeStruct((b, t, wz), BF16),
            jax.ShapeDtypeStruct((b, t, wx), F32),
            jax.ShapeDtypeStruct((b, t, wu), F32),
            jax.ShapeDtypeStruct((b, t, wg), BF16),
            jax.ShapeDtypeStruct((b, t, 128), F32),
        ],
        compiler_params=_params(("parallel", "parallel")),
        name="even_in",
    )(ctx, x, mod, w)


def _conv_kernel(prev_ref, main_ref, next_ref, w_ref, b_ref, o_ref, xe_ref, *, n_ctx_tiles, n_tiles):
    tm = TOKEN_TILE
    ch = o_ref.shape[-1]
    t = pl.program_id(1)
    has_prev = jnp.logical_and(t != 0, t != n_ctx_tiles)
    has_next = jnp.logical_and(t != n_ctx_tiles - 1, t != n_tiles - 1)
    xe_ref[0:HALO_ROWS] = jnp.where(has_prev, prev_ref[0], 0.0)
    xe_ref[HALO_ROWS:HALO_ROWS + tm] = main_ref[0]
    xe_ref[HALO_ROWS + tm:2 * HALO_ROWS + tm] = jnp.where(has_next, next_ref[0], 0.0)
    rows = tm + 2 * HALO_ROWS
    half = SSD_CONV // 2
    lanes = 128
    for c0 in range(0, ch, lanes):
        cs = slice(c0, c0 + lanes)
        xe = xe_ref[:, cs]
        acc = b_ref[:, cs] + w_ref[half:half + 1, cs] * xe[HALO_ROWS:HALO_ROWS + tm]
        for k in range(SSD_CONV):
            if k != half:
                shifted = pltpu.roll(xe, (half - k) % rows, 0)
                acc = acc + w_ref[k:k + 1, cs] * shifted[HALO_ROWS:HALO_ROWS + tm]
        o_ref[0, :, cs] = _silu(acc)


def _conv(xbc, conv_w, conv_b, n_ctx_tiles):
    b, t, ch = xbc.shape
    tm = TOKEN_TILE
    n_tiles = t // tm
    per = tm // HALO_ROWS
    last = t // HALO_ROWS - 1
    w = jnp.pad(conv_w.T, ((0, 8 - SSD_CONV), (0, 0)))
    return pl.pallas_call(
        functools.partial(_conv_kernel, n_ctx_tiles=n_ctx_tiles, n_tiles=n_tiles),
        grid=(b, n_tiles),
        in_specs=[
            pl.BlockSpec((1, HALO_ROWS, ch), lambda i, j: (i, jnp.maximum(j * per - 1, 0), 0)),
            pl.BlockSpec((1, tm, ch), lambda i, j: (i, j, 0)),
            pl.BlockSpec((1, HALO_ROWS, ch), lambda i, j: (i, jnp.minimum((j + 1) * per, last), 0)),
            pl.BlockSpec((8, ch), lambda i, j: (0, 0)),
            pl.BlockSpec((1, ch), lambda i, j: (0, 0)),
        ],
        out_specs=pl.BlockSpec((1, tm, ch), lambda i, j: (i, j, 0)),
        out_shape=jax.ShapeDtypeStruct((b, t, ch), F32),
        scratch_shapes=[pltpu.VMEM((tm + 2 * HALO_ROWS, ch), F32)],
        compiler_params=_params(("parallel", "parallel")),
        name="conv",
    )(xbc, xbc, xbc, w, conv_b.reshape(1, ch))


def _ssd_kernel(xf_ref, xb_ref, dtf_ref, dtb_ref, bias_ref, a_ref, e_ref, yf_ref, yb_ref, st_ref):
    L = SSD_CHUNK
    width = SSD_HEADS * HEAD_DIM
    gw = width // SSD_GROUPS
    hpg = SSD_HEADS // SSD_GROUPS

    @pl.when(pl.program_id(1) == 0)
    def _():
        st_ref[...] = jnp.zeros_like(st_ref)

    row = lax.broadcasted_iota(jnp.int32, (L, L), 0)
    col = lax.broadcasted_iota(jnp.int32, (L, L), 1)
    lane_head = lax.broadcasted_iota(jnp.int32, (L, gw), 1) // HEAD_DIM
    sub16 = lax.broadcasted_iota(jnp.int32, (16, 128), 0)

    def split3(x):
        x1 = x.astype(BF16)
        r = x - x1.astype(F32)
        x2 = r.astype(BF16)
        return x1, x2, (r - x2.astype(F32)).astype(BF16)

    def chunk(d, x_ref, dt_ref, y_ref, r0):
        rows = pl.ds(r0, L)
        mask = (row >= col) if d == 0 else (row <= col)
        dtv = _softplus(dt_ref[0, rows, :] + bias_ref[...])
        loga = dtv * a_ref[...]
        tri = jnp.where(mask, 1.0, 0.0).astype(BF16)
        cs = sum(jnp.dot(tri, part, preferred_element_type=F32) for part in split3(loga))
        cs_t = cs.T
        total = jnp.sum(loga, axis=0, keepdims=True)
        t1, t2, t3 = (part.astype(F32) for part in split3(total))
        tot = jnp.where(sub16 == 0, t1, jnp.where(sub16 == 1, t2, jnp.where(sub16 == 2, t3, 0.0)))
        stack = jnp.concatenate(
            [dtv.astype(BF16), jnp.exp(cs).astype(BF16), jnp.exp(total - cs).astype(BF16), tot.astype(BF16)], axis=0)
        ex = jnp.dot(stack, e_ref[d], preferred_element_type=F32)
        dt_x = ex[0:L]
        ecs_x = ex[L:2 * L]
        wend_x = ex[2 * L:3 * L]
        edec_x = jnp.exp(ex[3 * L:3 * L + 1] + ex[3 * L + 1:3 * L + 2] + ex[3 * L + 2:3 * L + 3])

        groups = []
        for g in range(SSD_GROUPS):
            sl = slice(g * gw, (g + 1) * gw)
            xs_g = x_ref[0, rows, sl]
            b_g = x_ref[0, rows, width + g * SSD_STATE:width + (g + 1) * SSD_STATE]
            c_g = x_ref[0, rows, width + (SSD_GROUPS + g) * SSD_STATE:width + (SSD_GROUPS + g + 1) * SSD_STATE]
            cb = c_g.astype(BF16)
            xd = xs_g * dt_x[:, sl]
            gram = lax.dot_general(cb, b_g.astype(BF16), (((1,), (1,)), ((), ())), preferred_element_type=F32)
            scores = []
            for hh in range(hpg):
                li = SSD_HEADS * d + hpg * g + hh
                seg = cs[:, li:li + 1] - cs_t[li:li + 1, :]
                scores.append((gram * jnp.where(mask, jnp.exp(seg), 0.0)).astype(BF16))
            scores = jnp.concatenate(scores, axis=1)
            xd_blocks = jnp.concatenate(
                [jnp.where(lane_head == hh, xd, 0.0).astype(BF16) for hh in range(hpg)], axis=0)
            y = jnp.dot(scores, xd_blocks, preferred_element_type=F32)
            xdw = (xd * wend_x[:, sl]).astype(BF16)
            groups.append((y, cb, b_g.T.astype(BF16), xdw, ecs_x[:, sl], edec_x[:, sl]))

        def carry():
            for g, (y, cb, bt, xdw, ecs_g, edec_g) in enumerate(groups):
                sl = slice(g * gw, (g + 1) * gw)
                st = st_ref[d, :, sl]
                y_ref[0, rows, sl] = (y + jnp.dot(cb, st.astype(BF16), preferred_element_type=F32) * ecs_g
                                      ).astype(y_ref.dtype)
                st_ref[d, :, sl] = st * edec_g + jnp.dot(bt, xdw, preferred_element_type=F32)

        return carry

    carries = [chunk(0, xf_ref, dtf_ref, yf_ref, 0), chunk(1, xb_ref, dtb_ref, yb_ref, L),
               chunk(0, xf_ref, dtf_ref, yf_ref, L), chunk(1, xb_ref, dtb_ref, yb_ref, 0)]
    for carry in carries:
        carry()


def _ssd(xbc, dt, dt_bias, a_log, n_ctx):
    b, t, ch = xbc.shape
    L = 2 * SSD_CHUNK
    assert n_ctx % L == 0 and t % L == 0
    n = t // L
    width = SSD_HEADS * HEAD_DIM
    ncc = n_ctx // L

    def fwd(i, j):
        return (i, j, 0)

    def bwd(i, j):
        return (i, jnp.where(j < ncc, ncc - 1 - j, n + ncc - 1 - j), 0)

    pad = lambda v: jnp.pad(v.reshape(1, -1), ((0, 0), (0, 128 - v.size)))
    bias = pad(dt_bias)
    a_neg = pad(-jnp.exp(a_log))
    head_of_lane = jnp.arange(width) // HEAD_DIM
    expand = jnp.stack([
        (jnp.arange(128)[:, None] == SSD_HEADS * d + head_of_lane[None, :]) for d in range(2)
    ]).astype(BF16)
    return pl.pallas_call(
        _ssd_kernel,
        grid=(b, n),
        in_specs=[
            pl.BlockSpec((1, L, ch), fwd),
            pl.BlockSpec((1, L, ch), bwd),
            pl.BlockSpec((1, L, 128), fwd),
            pl.BlockSpec((1, L, 128), bwd),
            pl.BlockSpec((1, 128), lambda i, j: (0, 0)),
            pl.BlockSpec((1, 128), lambda i, j: (0, 0)),
            pl.BlockSpec((2, 128, width), lambda i, j: (0, 0, 0)),
        ],
        out_specs=[pl.BlockSpec((1, L, width), fwd), pl.BlockSpec((1, L, width), bwd)],
        out_shape=[jax.ShapeDtypeStruct((b, t, width), BF16)] * 2,
        scratch_shapes=[pltpu.VMEM((2, SSD_STATE, width), F32)],
        compiler_params=_params(("parallel", "arbitrary")),
        name="ssd",
    )(xbc, xbc, dt, dt, bias, a_neg, expand)


def _s5_matrices_kernel(z_ref, ak_ref, bb_ref, cc_ref, m_ref, smat_ref, ot_ref):
    nb, width = S5_BLOCK, S5_BLOCK * S5_GROUP
    lanes = z_ref.shape[-1]

    def rows(g, x_ref, kf, kb, conj):
        out = []
        for d, k in ((0, kf), (1, kb)):
            xr, xi = x_ref[g, 2 * d], x_ref[g, 2 * d + 1]
            ar, ai = ak_ref[g, 2 * d, k:k + 1, :], ak_ref[g, 2 * d + 1, k:k + 1, :]
            im = xr * ai + xi * ar
            out.append((xr * ar - xi * ai, -im if conj else im))
        (f_re, f_im), (b_re, b_im) = out
        c0, c1, c2, c3 = (f_re, b_re, f_im, b_im) if g % 2 == 0 else (b_re, f_re, b_im, f_im)
        return jnp.concatenate([c0 + pltpu.roll(c1, S5_STATE, 1), c2 + pltpu.roll(c3, S5_STATE, 1)], axis=1)

    for g in range(z_ref.shape[0]):
        z = z_ref[g]
        for s in range(nb):
            blk = slice(s * S5_GROUP, (s + 1) * S5_GROUP)
            off = (nb - 1 - s) * S5_GROUP
            win = pltpu.roll(z, (lanes - off) % lanes, 1) if off else z
            m_ref[g, blk, :] = win[:, :width].astype(m_ref.dtype)
            smat_ref[g, blk, :] = rows(g, bb_ref, nb - 1 - s, s, False).astype(smat_ref.dtype)
            ot_ref[g, blk, :] = rows(g, cc_ref, s + 1, nb - s, True).astype(ot_ref.dtype)


def _s5_matrices(zall, ak, bb, cc):
    ng, nj, w = zall.shape
    lanes = -(-w // 128) * 128
    per = 8
    width = S5_BLOCK * S5_GROUP
    blk = lambda a: pl.BlockSpec((per,) + a.shape[1:], lambda i: (i,) + (0,) * (a.ndim - 1))
    zall = jnp.pad(zall, ((0, 0), (0, 0), (0, lanes - w)))
    out = jax.ShapeDtypeStruct((ng, S5_BLOCK * nj, width), BF16)
    return pl.pallas_call(
        _s5_matrices_kernel,
        grid=(ng // per,),
        in_specs=[blk(zall), blk(ak), blk(bb), blk(cc)],
        out_specs=[pl.BlockSpec((per, S5_BLOCK * nj, width), lambda i: (i, 0, 0))] * 3,
        out_shape=[out] * 3,
        compiler_params=_params(("parallel",)),
        name="s5_matrices",
    )(zall, ak, bb, cc)


def _s5_operators(lam_re, lam_im, log_step, b_re, b_im, c_re, c_im):
    nb = S5_BLOCK
    ng, ns = lam_re.shape[1:]
    hp = functools.partial(jnp.einsum, precision=HIGHEST)
    step = jnp.exp(log_step)[..., None]
    k = jnp.arange(nb + 1, dtype=F32)[:, None, None, None]
    mag = jnp.exp(k * (lam_re * step))
    ak_re = mag * jnp.cos(k * (lam_im * step))
    ak_im = mag * jnp.sin(k * (lam_im * step))
    ab_re, ab_im = ak_re[1], ak_im[1]
    den = lam_re * lam_re + lam_im * lam_im
    f_re = ((ab_re - 1.0) * lam_re + ab_im * lam_im) / den
    f_im = (ab_im * lam_re - (ab_re - 1.0) * lam_im) / den
    tr = lambda a: jnp.swapaxes(a, -1, -2)
    bb_re = tr(f_re[..., None] * b_re - f_im[..., None] * b_im)
    bb_im = tr(f_re[..., None] * b_im + f_im[..., None] * b_re)
    akr, aki = ak_re[:nb, :, :, None, :], ak_im[:nb, :, :, None, :]
    w_re = akr * bb_re - aki * bb_im
    w_im = akr * bb_im + aki * bb_re
    kern = hp('kdgjp,dgip->kdgji', w_re, c_re) - hp('kdgjp,dgip->kdgji', w_im, c_im)
    kf, kb = kern[:, 0], kern[:, 1]
    zall = jnp.concatenate([kb[:0:-1], (kf[0] + kb[0])[None], kf[1:]], axis=0).transpose(1, 2, 0, 3)
    comps = lambda re, im: jnp.stack([re[0], im[0], re[1], im[1]], axis=1)
    lane_pad = lambda a, r: jnp.pad(a, ((0, 0), (0, 0), (0, r - a.shape[2]), (0, 128 - ns)))
    ak = lane_pad(comps(ak_re.transpose(1, 2, 0, 3), ak_im.transpose(1, 2, 0, 3)), 8 * (-(-(nb + 1) // 8)))
    m, smat, ot = _s5_matrices(zall.reshape(ng, S5_GROUP, (2 * nb - 1) * S5_GROUP), ak,
                               lane_pad(comps(bb_re, bb_im), S5_GROUP), lane_pad(comps(c_re, c_im), S5_GROUP))
    swap = lambda a: a.reshape(ng // 2, 2, ns)[:, ::-1].reshape(ng * ns)
    dec = jnp.stack([ak_re[nb, 0].reshape(-1), ak_im[nb, 0].reshape(-1), swap(ak_re[nb, 1]), swap(ak_im[nb, 1])])
    return smat, m, ot, dec


def _s5_kernel(u_ref, smat_ref, m_ref, ot_ref, dec_ref, y_ref, us, uf, ys, sfr, sbr, sfi, sbi, hfr, hbr, hfi, hbi, *,
               n_ctx_blocks):
    nb = S5_BLOCK
    groups = smat_ref.shape[0]
    n = u_ref.shape[1] // nb
    ncb = n_ctx_blocks
    seg = S5_GROUP
    per_col = 128 // seg
    lane_seg = lax.broadcasted_iota(jnp.int32, (n, 128), 1) // seg
    low = lax.broadcasted_iota(jnp.int32, (n, 128), 1) < S5_STATE

    for s in range(nb):
        us[s] = u_ref[0, pl.ds(s, n, stride=nb), :]

    def fold(g):
        for v in range(nb // per_col):
            col = None
            for k in range(per_col):
                x = us[v * per_col + k]
                shift = (seg * (k - g)) % 128
                if shift:
                    x = pltpu.roll(x, shift, 1)
                col = x if col is None else jnp.where(lane_seg == k, x, col)
            uf[g, :, 128 * v:128 * (v + 1)] = col.astype(BF16)
        return jnp.dot(uf[g], smat_ref[g], preferred_element_type=F32)

    for p in range(groups // 2):
        se, so = fold(2 * p), fold(2 * p + 1)
        sl = slice(128 * p, 128 * (p + 1))
        sfr[:, sl] = jnp.where(low, se[:, :128], so[:, :128])
        sbr[:, sl] = jnp.where(low, so[:, :128], se[:, :128])
        sfi[:, sl] = jnp.where(low, se[:, 128:], so[:, 128:])
        sbi[:, sl] = jnp.where(low, so[:, 128:], se[:, 128:])
    arf, aif, arb, aib = dec_ref[0, 0:1], dec_ref[0, 1:2], dec_ref[0, 2:3], dec_ref[0, 3:4]

    def step(i, carry):
        fr, fi, br, bi = carry
        rf = i
        rb = jnp.where(i < ncb, ncb - 1 - i, n + ncb - 1 - i)
        hfr[pl.ds(rf, 1), :] = fr
        hfi[pl.ds(rf, 1), :] = fi
        hbr[pl.ds(rb, 1), :] = br
        hbi[pl.ds(rb, 1), :] = bi
        nfr = arf * fr - aif * fi + sfr[pl.ds(rf, 1), :]
        nfi = arf * fi + aif * fr + sfi[pl.ds(rf, 1), :]
        nbr = arb * br - aib * bi + sbr[pl.ds(rb, 1), :]
        nbi = arb * bi + aib * br + sbi[pl.ds(rb, 1), :]
        return nfr, nfi, nbr, nbi

    zero = jnp.zeros((1, 128 * (groups // 2)), F32)
    lax.fori_loop(0, n, step, (zero, zero, zero, zero), unroll=4)
    for g in range(groups):
        sl = slice(128 * (g // 2), 128 * (g // 2 + 1))
        first, second = ((hfr, hfi), (hbr, hbi)) if g % 2 == 0 else ((hbr, hbi), (hfr, hfi))
        states = jnp.concatenate(
            [jnp.where(low, a[:, sl], b[:, sl]).astype(BF16) for a, b in zip(first, second)], axis=1)
        ys[g] = (jnp.dot(uf[g], m_ref[g], preferred_element_type=F32)
                 + lax.dot_general(states, ot_ref[g], (((1,), (1,)), ((), ())), preferred_element_type=F32))
    for l in range(nb):
        v, k = divmod(l, per_col)
        out = None
        for g in range(groups):
            x = ys[g, :, 128 * v:128 * (v + 1)]
            shift = (seg * (g - k)) % 128
            if shift:
                x = pltpu.roll(x, shift, 1)
            out = x if out is None else jnp.where(lane_seg == g, x, out)
        y_ref[0, pl.ds(l, n, stride=nb), :] = out


def _s5(u, ops, n_ctx_blocks):
    smat, m, ow, dec = ops
    b, t, w = u.shape
    nb = S5_BLOCK
    n = t // nb
    gw = nb * S5_GROUP
    gpb = 128 // S5_GROUP
    steps = w // 128
    lanes = gpb * S5_STATE
    dec = dec.reshape(4, steps, lanes).transpose(1, 0, 2)
    dec = jnp.concatenate([dec, jnp.zeros_like(dec)], axis=1)
    return pl.pallas_call(
        functools.partial(_s5_kernel, n_ctx_blocks=n_ctx_blocks),
        grid=(b, steps),
        in_specs=[
            pl.BlockSpec((1, t, 128), lambda i, j: (i, 0, j)),
            pl.BlockSpec((gpb,) + smat.shape[1:], lambda i, j: (j, 0, 0)),
            pl.BlockSpec((gpb,) + m.shape[1:], lambda i, j: (j, 0, 0)),
            pl.BlockSpec((gpb,) + ow.shape[1:], lambda i, j: (j, 0, 0)),
            pl.BlockSpec((1, 8, lanes), lambda i, j: (j, 0, 0)),
        ],
        out_specs=pl.BlockSpec((1, t, 128), lambda i, j: (i, 0, j)),
        out_shape=jax.ShapeDtypeStruct((b, t, w), F32),
        scratch_shapes=[pltpu.VMEM((nb, n, 128), F32), pltpu.VMEM((gpb, n, gw), BF16), pltpu.VMEM((gpb, n, gw), F32)]
        + [pltpu.VMEM((n, lanes), F32)] * 8,
        compiler_params=_params(("parallel", "parallel")),
        name="s5",
    )(u, smat, m, ow, dec)


def _even_out_kernel(yf_ref, yb_ref, xs_ref, z_ref, y5_ref, u_ref, g_ref, ctx_ref, x_ref, mod_ref, vs_ref, v5_ref,
                     glu_ref, w_ref, o_ref, *, n_ctx_tiles):
    ws = z_ref.shape[-1]
    y = _gelu_tanh(y5_ref[0] + v5_ref[0:1] * u_ref[0])
    glu = jnp.dot(y.astype(BF16), glu_ref[...], preferred_element_type=F32)
    ys = yf_ref[0].astype(F32) + yb_ref[0].astype(F32) + vs_ref[1:2] * xs_ref[0]
    s = _rms(ys * _silu(z_ref[0].astype(F32))) * vs_ref[0:1]
    o = jnp.dot(s.astype(BF16), w_ref[0:ws], preferred_element_type=F32)
    y = y * _sigmoid(glu + v5_ref[1:2]) * _silu(g_ref[0].astype(F32))
    o = o + jnp.dot(y.astype(BF16), w_ref[ws:], preferred_element_type=F32)
    o_ref[0] = _stream_tile(ctx_ref, x_ref, n_ctx_tiles) + mod_ref[0, 0][2:3] * o


def _even_out(yf, yb, xbc, z, y5, u, g, ctx, x, mod, ssd_norm, d_ssd, d_s5, glu_w, glu_b, w_out, n_ctx_tiles):
    b, t = z.shape[:2]
    d = x.shape[-1]
    tm = TOKEN_TILE
    ws, w5 = z.shape[-1], u.shape[-1]
    vs = jnp.pad(jnp.stack([ssd_norm, jnp.repeat(d_ssd, HEAD_DIM)]), ((0, 6), (0, 0)))
    v5 = jnp.pad(jnp.stack([d_s5, glu_b]), ((0, 6), (0, 0)))
    tok = lambda n: pl.BlockSpec((1, tm, n), lambda i, j: (i, j, 0))
    const = lambda a: pl.BlockSpec(a.shape, lambda i, j: (0,) * a.ndim)
    glu_w = glu_w.astype(BF16)
    w_out = w_out.astype(BF16)
    return pl.pallas_call(
        functools.partial(_even_out_kernel, n_ctx_tiles=n_ctx_tiles),
        grid=(b, t // tm),
        in_specs=[tok(ws), tok(ws), tok(ws), tok(ws), tok(w5), tok(w5), tok(w5)] + _stream_specs(tm, d, n_ctx_tiles) + [
            pl.BlockSpec((1, 1, 3, d), lambda i, j: (i, (j >= n_ctx_tiles).astype(jnp.int32), 0, 0)),
            const(vs), const(v5), const(glu_w), const(w_out),
        ],
        out_specs=tok(d),
        out_shape=jax.ShapeDtypeStruct((b, t, d), F32),
        compiler_params=_params(("parallel", "parallel")),
        name="even_out",
    )(yf, yb, xbc, z, y5, u, g, ctx, x, mod, vs, v5, glu_w, w_out)


def _odd_in_kernel(h_ref, mod_ref, w_ref, qg_ref, kg_ref, cos_ref, sin_ref, ones_ref, q_ref, k_ref, v_ref, g_ref, *,
                   q_w, kv_w):
    tm = TOKEN_TILE
    a = _norm_mod(h_ref[0], mod_ref[0, 0]).astype(BF16)
    cosv = cos_ref[...]
    sinv = sin_ref[...]
    first_half = (lax.broadcasted_iota(jnp.int32, (tm, 128), 1) % (HEAD_DIM // 2)) < (HEAD_DIM // 4)

    def project(lo, width=256):
        return jnp.dot(a, w_ref[:, lo:lo + width], preferred_element_type=F32)

    def head_norm_rope(x, gain, out_ref, c, transposed):
        ms = jnp.dot((x * x).astype(BF16), ones_ref[...], preferred_element_type=F32) * (1.0 / HEAD_DIM)
        xn = x * lax.rsqrt(ms + NORM_EPS) * gain
        for s in range(2):
            xb = xn[:, 128 * s:128 * (s + 1)]
            partner = jnp.where(first_half, pltpu.roll(xb, 128 - HEAD_DIM // 4, 1), pltpu.roll(xb, HEAD_DIM // 4, 1))
            r = xb * cosv + partner * sinv
            head = 4 * c + 2 * s
            if transposed:
                rt = r.T
                out_ref[0, head] = rt[:HEAD_DIM].astype(out_ref.dtype)
                out_ref[0, head + 1] = rt[HEAD_DIM:].astype(out_ref.dtype)
            else:
                out_ref[0, head] = r[:, :HEAD_DIM].astype(out_ref.dtype)
                out_ref[0, head + 1] = pltpu.roll(r, HEAD_DIM, 1)[:, :HEAD_DIM].astype(out_ref.dtype)

    chunks = [(256 * c, qg_ref, q_ref, c) for c in range(q_w // 256)] + \
             [(q_w + 256 * c, kg_ref, k_ref, c) for c in range(kv_w // 256)]
    g_lo = q_w + 2 * kv_w
    g_w = w_ref.shape[1] - g_lo
    x_next = project(chunks[0][0])
    for i, (lo, gain_ref, out_ref, c) in enumerate(chunks):
        x = x_next
        if i + 1 < len(chunks):
            x_next = project(chunks[i + 1][0])
        if 256 * i < g_w:
            g_ref[0, :, 256 * i:256 * (i + 1)] = project(g_lo + 256 * i).astype(g_ref.dtype)
        head_norm_rope(x, gain_ref[:, 256 * c:256 * (c + 1)], out_ref, c, out_ref is q_ref)
    assert 256 * len(chunks) >= g_w
    lane = lax.broadcasted_iota(jnp.int32, (tm, 128), 1)
    vt_rows = v_ref.shape[2]
    for c in range(kv_w // 128):
        x = project(q_w + kv_w + 128 * c, 128)
        for s in range(2):
            xs = pltpu.roll(x, HEAD_DIM, 1) if s else x
            vh = jnp.where(lane < HEAD_DIM, xs, jnp.where(lane == HEAD_DIM, 1.0, 0.0))
            v_ref[0, 2 * c + s] = vh.T[:vt_rows].astype(v_ref.dtype)


def _rope_tables(n_ctx, n_lat):
    pairs = HEAD_DIM // 4
    pos = jnp.arange(n_lat)
    row = (pos // GRID_W).astype(F32)
    colp = (pos % GRID_W).astype(F32)
    inv = ROPE_THETA ** (-jnp.arange(pairs, dtype=F32) / pairs)
    lane = jnp.arange(128) % HEAD_DIM
    axis_is_col = (lane // (HEAD_DIM // 2)) == 1
    ang = jnp.where(axis_is_col[None, :], colp[:, None], row[:, None]) * inv[lane % pairs][None, :]
    sign = jnp.where((lane % (HEAD_DIM // 2)) < pairs, -1.0, 1.0)
    cos = jnp.concatenate([jnp.ones((n_ctx, 128), F32), jnp.cos(ang)], axis=0)
    sin = jnp.concatenate([jnp.zeros((n_ctx, 128), F32), jnp.sin(ang) * sign[None, :]], axis=0)
    return cos, sin


def _odd_in(h, mod, w, q_gain, k_gain, cos, sin, n_ctx_tiles):
    b, t, d = h.shape
    tm = TOKEN_TILE
    q_w = ATTN_Q_HEADS * HEAD_DIM
    kv_w = ATTN_KV_HEADS * HEAD_DIM
    qg = (jnp.tile(q_gain, ATTN_Q_HEADS) * (HEAD_DIM ** -0.5 * math.log2(math.e))).reshape(1, q_w)
    kg = jnp.tile(k_gain, ATTN_KV_HEADS).reshape(1, kv_w)
    blk = jnp.arange(256) // HEAD_DIM
    ones = (blk[:, None] == blk[None, :]).astype(BF16)
    tok = lambda n: pl.BlockSpec((1, tm, n), lambda i, j: (i, j, 0))
    heads = lambda nh, n: pl.BlockSpec((1, nh, tm, n), lambda i, j: (i, 0, j, 0))
    heads_t = lambda nh, n: pl.BlockSpec((1, nh, n, tm), lambda i, j: (i, 0, 0, j))
    const = lambda a: pl.BlockSpec(a.shape, lambda i, j: (0,) * a.ndim)
    return pl.pallas_call(
        functools.partial(_odd_in_kernel, q_w=q_w, kv_w=kv_w),
        grid=(b, t // tm),
        in_specs=[
            tok(d),
            pl.BlockSpec((1, 1, 3, d), lambda i, j: (i, (j >= n_ctx_tiles).astype(jnp.int32), 0, 0)),
            const(w), const(qg), const(kg),
            pl.BlockSpec((tm, 128), lambda i, j: (j, 0)),
            pl.BlockSpec((tm, 128), lambda i, j: (j, 0)),
            const(ones),
        ],
        out_specs=[
            pl.BlockSpec((1, ATTN_Q_HEADS, HEAD_DIM, tm), lambda i, j: (i, 0, 0, jnp.maximum(j - n_ctx_tiles, 0))),
            heads(ATTN_KV_HEADS, HEAD_DIM), heads_t(ATTN_KV_HEADS, VT_ROWS), tok(q_w)],
        out_shape=[
            jax.ShapeDtypeStruct((b, ATTN_Q_HEADS, HEAD_DIM, t - n_ctx_tiles * tm), BF16),
            jax.ShapeDtypeStruct((b, ATTN_KV_HEADS, t, HEAD_DIM), BF16),
            jax.ShapeDtypeStruct((b, ATTN_KV_HEADS, VT_ROWS, t), BF16),
            jax.ShapeDtypeStruct((b, t, q_w), BF16),
        ],
        compiler_params=_params(("parallel", "arbitrary")),
        name="odd_in",
    )(h, mod, w, qg, kg, cos, sin, ones)


def _attn_kernel(q_ref, k_ref, v_ref, o_ref, *, tk):
    rep, hd, tq = q_ref.shape[1:]
    rows = v_ref.shape[2]
    nk = k_ref.shape[2] // tk
    m = [jnp.full((1, tq), -1e30, F32)] * rep
    acc = [jnp.zeros((rows, tq), F32)] * rep
    blocks = [(j, r) for j in range(nk) for r in range(rep)]
    scores = {}
    for i in range(len(blocks) + ATTN_LOOKAHEAD):
        if i < len(blocks):
            j, r = blocks[i]
            scores[i] = jnp.dot(k_ref[0, 0, j * tk:(j + 1) * tk, :], q_ref[0, r], preferred_element_type=F32)
        if i >= ATTN_LOOKAHEAD:
            j, r = blocks[i - ATTN_LOOKAHEAD]
            s = scores.pop(i - ATTN_LOOKAHEAD)
            m_new = jnp.maximum(m[r], jnp.max(s, axis=0, keepdims=True))
            p = jnp.exp2(s - m_new).astype(BF16)
            pv = jnp.dot(v_ref[0, 0, :, j * tk:(j + 1) * tk], p, preferred_element_type=F32)
            acc[r] = jnp.exp2(m[r] - m_new) * acc[r] + pv
            m[r] = m_new
    for c in range(rep // 2):
        pair = [acc[r][:hd] * (1.0 / acc[r][hd:hd + 1]) for r in (2 * c, 2 * c + 1)]
        o_ref[0, :, 2 * hd * c:2 * hd * (c + 1)] = jnp.concatenate(pair, axis=0).T.astype(o_ref.dtype)


def _attention(q, k, v):
    b, hq, hd, n_lat = q.shape
    hkv, t = k.shape[1:3]
    rep = hq // hkv
    tq = ATTN_Q_TILE
    tk = next(c for c in (256, 128) if t % c == 0)
    assert n_lat % tq == 0
    return pl.pallas_call(
        functools.partial(_attn_kernel, tk=tk),
        grid=(b, hkv, n_lat // tq),
        in_specs=[
            pl.BlockSpec((1, rep, hd, tq), lambda i, j, n: (i, j, 0, n)),
            pl.BlockSpec((1, 1, t, hd), lambda i, j, n: (i, j, 0, 0)),
            pl.BlockSpec((1, 1, v.shape[2], t), lambda i, j, n: (i, j, 0, 0)),
        ],
        out_specs=pl.BlockSpec((1, tq, rep * hd), lambda i, j, n: (i, n, j)),
        out_shape=jax.ShapeDtypeStruct((b, n_lat, hq * hd), BF16),
        compiler_params=_params(("parallel", "parallel", "parallel")),
        name="attention",
    )(q, k, v)


def _attn_out_kernel(o_ref, g_ref, h_ref, mod_ref, w_ref, fg_ref, out_ref):
    x = o_ref[0].astype(F32) * _silu(g_ref[0].astype(F32))
    y = jnp.dot(x.astype(BF16), w_ref[...], preferred_element_type=F32)
    hn = h_ref[0] + mod_ref[0, 0][2:3] * y
    out_ref[0] = _rms(hn) * fg_ref[...]


def _attn_out(o, g, h, mod, w_out, final_gain, n_ctx_tiles):
    b, n_lat, d = o.shape
    tm = TOKEN_TILE
    lat = lambda n: pl.BlockSpec((1, tm, n), lambda i, j: (i, j + n_ctx_tiles, 0))
    w_out = w_out.astype(BF16)
    return pl.pallas_call(
        _attn_out_kernel,
        grid=(b, n_lat // tm),
        in_specs=[
            pl.BlockSpec((1, tm, d), lambda i, j: (i, j, 0)),
            lat(d), lat(d),
            pl.BlockSpec((1, 1, 3, d), lambda i, j: (i, 1, 0, 0)),
            pl.BlockSpec(w_out.shape, lambda i, j: (0, 0)),
            pl.BlockSpec((1, d), lambda i, j: (0, 0)),
        ],
        out_specs=pl.BlockSpec((1, tm, d), lambda i, j: (i, j, 0)),
        out_shape=jax.ShapeDtypeStruct((b, n_lat, d), F32),
        compiler_params=_params(("parallel", "parallel")),
        name="attn_out",
    )(o, g, h, mod, w_out, final_gain.reshape(1, d))


def kernel(x, c, ctx, c_ctx, ada_w, ada_b, ev_w_in, ev_conv_w, ev_conv_b, ev_dt_bias, ev_a_log, ev_d_ssd, ev_ssd_norm, ev_lam_re, ev_lam_im, ev_log_step, ev_b_re, ev_b_im, ev_c_re, ev_c_im, ev_d_s5, ev_glu_w, ev_glu_b, ev_w_out, od_w_in, od_q_gain, od_k_gain, od_w_out, final_gain):
    b, n_lat, d = x.shape
    n_ctx = ctx.shape[1]
    assert ada_w.shape[0] == 2 and n_ctx % TOKEN_TILE == 0 and n_lat % TOKEN_TILE == 0
    n_ctx_tiles = n_ctx // TOKEN_TILE
    mods = _adaln(c, c_ctx, ada_w, ada_b)

    ws = SSD_HEADS * HEAD_DIM
    wx = ws + 2 * SSD_GROUPS * SSD_STATE
    w5 = ev_d_s5.shape[-1]
    w = ev_w_in[0]
    cuts = (ws, ws + wx, ws + wx + 2 * SSD_HEADS, ws + wx + 2 * SSD_HEADS + w5)
    w_dt = jnp.pad(w[:, cuts[1]:cuts[2]], ((0, 0), (0, 128 - 2 * SSD_HEADS)))
    w_dt_hi, w_dt_lo = _split_bf16(w_dt)
    w_cat = jnp.concatenate(
        [w[:, :cuts[1]].astype(BF16), w[:, cuts[2]:].astype(BF16), w_dt_hi, w_dt_lo], axis=1)
    z, xbc, u, g, dt = _even_in(ctx, x, mods[0], w_cat, n_ctx_tiles, (ws, wx, w5, w5))
    xbc = _conv(xbc, ev_conv_w[0], ev_conv_b[0], n_ctx_tiles)
    yf, yb = _ssd(xbc, dt, ev_dt_bias[0], ev_a_log[0], n_ctx)
    ops = _s5_operators(ev_lam_re[0], ev_lam_im[0], ev_log_step[0], ev_b_re[0], ev_b_im[0], ev_c_re[0], ev_c_im[0])
    y5 = _s5(u, ops, n_ctx // S5_BLOCK)
    h = _even_out(yf, yb, xbc, z, y5, u, g, ctx, x, mods[0], ev_ssd_norm[0], ev_d_ssd[0], ev_d_s5[0], ev_glu_w[0],
                  ev_glu_b[0], ev_w_out[0], n_ctx_tiles)

    cos, sin = _rope_tables(n_ctx, n_lat)
    q, k, v, g = _odd_in(h, mods[1], od_w_in[0].astype(BF16), od_q_gain[0], od_k_gain[0], cos, sin, n_ctx_tiles)
    o = _attention(q, k, v)
    return _attn_out(o, g, h, mods[1], od_w_out[0], final_gain, n_ctx_tiles)
```

```python
import functools
import math

import jax
import jax.numpy as jnp
from jax import lax
from jax.experimental import pallas as pl
from jax.experimental.pallas import tpu as pltpu

F32 = jnp.float32
BF16 = jnp.bfloat16
HIGHEST = lax.Precision.HIGHEST

NORM_EPS = 1e-6
GRID_W = 64
ROPE_THETA = 10000.0

HEAD_DIM = 64
SSD_HEADS = 16
SSD_GROUPS = 4
SSD_STATE = 128
SSD_CHUNK = 128
SSD_CONV = 5
S5_GROUP = 16
S5_STATE = 64
S5_BLOCK = 16
ATTN_Q_HEADS = 16
ATTN_KV_HEADS = 4
VT_ROWS = 80
ATTN_Q_TILE = 256
ATTN_LOOKAHEAD = 8

TOKEN_TILE = 256
HALO_ROWS = 8
VMEM_LIMIT = 56 << 20


def _params(semantics):
    return pltpu.CompilerParams(dimension_semantics=semantics, vmem_limit_bytes=VMEM_LIMIT)


def _sigmoid(x):
    return 1.0 / (1.0 + jnp.exp2(x * (-math.log2(math.e))))


def _silu(x):
    return x * _sigmoid(x)


def _softplus(x):
    return jnp.maximum(x, 0.0) + jnp.log(1.0 + jnp.exp(-jnp.abs(x)))


def _gelu_tanh(x):
    c = math.sqrt(2.0 / math.pi)
    return x * (0.5 + 0.5 * jnp.tanh(x * (c + (0.044715 * c) * (x * x))))


def _rms(x):
    return x * lax.rsqrt(jnp.mean(x * x, axis=-1, keepdims=True) + NORM_EPS)


def _norm_mod(h, mod):
    return _rms(h) * (1.0 + mod[1:2]) + mod[0:1]


def _split_bf16(x):
    hi = x.astype(BF16)
    return hi, (x - hi.astype(F32)).astype(BF16)


def _adaln_kernel(s_ref, w_ref, b_ref, o_ref):
    s = _silu(s_ref[...])
    o_ref[0] = jnp.dot(s, w_ref[0], preferred_element_type=F32, precision=HIGHEST) + b_ref[0]


def _adaln(c, c_ctx, ada_w, ada_b):
    depth, d, d3 = ada_w.shape
    b = c.shape[0]
    assert b < 8
    rows = jnp.concatenate([c, c_ctx[None], jnp.zeros((7 - b, d), F32)], axis=0)
    out = pl.pallas_call(
        _adaln_kernel,
        grid=(depth, d3 // d),
        in_specs=[
            pl.BlockSpec((8, d), lambda i, j: (0, 0)),
            pl.BlockSpec((1, d, d), lambda i, j: (i, 0, j)),
            pl.BlockSpec((1, 1, d), lambda i, j: (i, 0, j)),
        ],
        out_specs=pl.BlockSpec((1, 8, d), lambda i, j: (i, 0, j)),
        out_shape=jax.ShapeDtypeStruct((depth, 8, d3), F32),
        compiler_params=_params(("arbitrary", "arbitrary")),
        name="adaln",
    )(rows, ada_w, ada_b.reshape(depth, 1, d3))
    m = out.reshape(depth, 8, 3, d)
    lat = m[:, :b]
    ctx = jnp.broadcast_to(m[:, b:b + 1], lat.shape)
    return jnp.stack([ctx, lat], axis=2)


def _stream_tile(ctx_ref, x_ref, n_ctx_tiles):
    return jnp.where(pl.program_id(1) < n_ctx_tiles, ctx_ref[0], x_ref[0])


def _stream_specs(tm, d, n_ctx_tiles):
    return [
        pl.BlockSpec((1, tm, d), lambda i, j: (i, jnp.minimum(j, n_ctx_tiles - 1), 0)),
        pl.BlockSpec((1, tm, d), lambda i, j: (i, jnp.maximum(j - n_ctx_tiles, 0), 0)),
    ]


def _even_in_kernel(ctx_ref, x_ref, mod_ref, w_ref, z_ref, xbc_ref, u_ref, g_ref, dt_ref, *, cuts, n_ctx_tiles):
    a = _norm_mod(_stream_tile(ctx_ref, x_ref, n_ctx_tiles), mod_ref[0, 0])
    a = a.astype(BF16)

    def mm(lo, hi):
        return jnp.dot(a, w_ref[:, lo:hi], preferred_element_type=F32)

    c0, c1, c2, c3, c4, c5 = cuts
    z_ref[0] = mm(0, c0).astype(z_ref.dtype)
    xbc_ref[0] = mm(c0, c1)
    u_ref[0] = mm(c1, c2)
    g_ref[0] = mm(c2, c3).astype(g_ref.dtype)
    dt = mm(c3, c5)
    dt_ref[0] = dt[:, :c4 - c3] + dt[:, c4 - c3:]


def _even_in(ctx, x, mod, w, n_ctx_tiles, widths):
    b, n_lat, d = x.shape
    t = ctx.shape[1] + n_lat
    tm = TOKEN_TILE
    wz, wx, wu, wg = widths
    cuts = (wz, wz + wx, wz + wx + wu, wz + wx + wu + wg, wz + wx + wu + wg + 128, wz + wx + wu + wg + 256)
    assert w.shape == (d, cuts[-1])
    tok = lambda n: pl.BlockSpec((1, tm, n), lambda i, j: (i, j, 0))
    return pl.pallas_call(
        functools.partial(_even_in_kernel, cuts=cuts, n_ctx_tiles=n_ctx_tiles),
        grid=(b, t // tm),
        in_specs=_stream_specs(tm, d, n_ctx_tiles) + [
            pl.BlockSpec((1, 1, 3, d), lambda i, j: (i, (j >= n_ctx_tiles).astype(jnp.int32), 0, 0)),
            pl.BlockSpec(w.shape, lambda i, j: (0, 0)),
        ],
        out_specs=[tok(wz), tok(wx), tok(wu), tok(wg), tok(128)],
        out_shape=[
            jax.ShapeDtypeStruct((b, t, wz), BF16),
            jax.ShapeDtypeStruct((b, t, wx), F32),
            jax.ShapeDtypeStruct((b, t, wu), F32),
            jax.ShapeDtypeStruct((b, t, wg), BF16),
            jax.ShapeDtypeStruct((b, t, 128), F32),
        ],
        compiler_params=_params(("parallel", "parallel")),
        name="even_in",
    )(ctx, x, mod, w)


def _conv_kernel(prev_ref, main_ref, next_ref, w_ref, b_ref, o_ref, xe_ref, *, n_ctx_tiles, n_tiles):
    tm = TOKEN_TILE
    ch = o_ref.shape[-1]
    t = pl.program_id(1)
    has_prev = jnp.logical_and(t != 0, t != n_ctx_tiles)
    has_next = jnp.logical_and(t != n_ctx_tiles - 1, t != n_tiles - 1)
    xe_ref[0:HALO_ROWS] = jnp.where(has_prev, prev_ref[0], 0.0)
    xe_ref[HALO_ROWS:HALO_ROWS + tm] = main_ref[0]
    xe_ref[HALO_ROWS + tm:2 * HALO_ROWS + tm] = jnp.where(has_next, next_ref[0], 0.0)
    rows = tm + 2 * HALO_ROWS
    half = SSD_CONV // 2
    lanes = 128
    for c0 in range(0, ch, lanes):
        cs = slice(c0, c0 + lanes)
        xe = xe_ref[:, cs]
        acc = b_ref[:, cs] + w_ref[half:half + 1, cs] * xe[HALO_ROWS:HALO_ROWS + tm]
        for k in range(SSD_CONV):
            if k != half:
                shifted = pltpu.roll(xe, (half - k) % rows, 0)
                acc = acc + w_ref[k:k + 1, cs] * shifted[HALO_ROWS:HALO_ROWS + tm]
        o_ref[0, :, cs] = _silu(acc)


def _conv(xbc, conv_w, conv_b, n_ctx_tiles):
    b, t, ch = xbc.shape
    tm = TOKEN_TILE
    n_tiles = t // tm
    per = tm // HALO_ROWS
    last = t // HALO_ROWS - 1
    w = jnp.pad(conv_w.T, ((0, 8 - SSD_CONV), (0, 0)))
    return pl.pallas_call(
        functools.partial(_conv_kernel, n_ctx_tiles=n_ctx_tiles, n_tiles=n_tiles),
        grid=(b, n_tiles),
        in_specs=[
            pl.BlockSpec((1, HALO_ROWS, ch), lambda i, j: (i, jnp.maximum(j * per - 1, 0), 0)),
            pl.BlockSpec((1, tm, ch), lambda i, j: (i, j, 0)),
            pl.BlockSpec((1, HALO_ROWS, ch), lambda i, j: (i, jnp.minimum((j + 1) * per, last), 0)),
            pl.BlockSpec((8, ch), lambda i, j: (0, 0)),
            pl.BlockSpec((1, ch), lambda i, j: (0, 0)),
        ],
        out_specs=pl.BlockSpec((1, tm, ch), lambda i, j: (i, j, 0)),
        out_shape=jax.ShapeDtypeStruct((b, t, ch), F32),
        scratch_shapes=[pltpu.VMEM((tm + 2 * HALO_ROWS, ch), F32)],
        compiler_params=_params(("parallel", "parallel")),
        name="conv",
    )(xbc, xbc, xbc, w, conv_b.reshape(1, ch))


def _ssd_kernel(xf_ref, xb_ref, dtf_ref, dtb_ref, bias_ref, a_ref, e_ref, yf_ref, yb_ref, st_ref):
    L = SSD_CHUNK
    width = SSD_HEADS * HEAD_DIM
    gw = width // SSD_GROUPS
    hpg = SSD_HEADS // SSD_GROUPS

    @pl.when(pl.program_id(1) == 0)
    def _():
        st_ref[...] = jnp.zeros_like(st_ref)

    row = lax.broadcasted_iota(jnp.int32, (L, L), 0)
    col = lax.broadcasted_iota(jnp.int32, (L, L), 1)
    lane_head = lax.broadcasted_iota(jnp.int32, (L, gw), 1) // HEAD_DIM
    sub16 = lax.broadcasted_iota(jnp.int32, (16, 128), 0)

    def split3(x):
        x1 = x.astype(BF16)
        r = x - x1.astype(F32)
        x2 = r.astype(BF16)
        return x1, x2, (r - x2.astype(F32)).astype(BF16)

    def chunk(d, x_ref, dt_ref, y_ref, r0):
        rows = pl.ds(r0, L)
        mask = (row >= col) if d == 0 else (row <= col)
        dtv = _softplus(dt_ref[0, rows, :] + bias_ref[...])
        loga = dtv * a_ref[...]
        tri = jnp.where(mask, 1.0, 0.0).astype(BF16)
        cs = sum(jnp.dot(tri, part, preferred_element_type=F32) for part in split3(loga))
        cs_t = cs.T
        total = jnp.sum(loga, axis=0, keepdims=True)
        t1, t2, t3 = (part.astype(F32) for part in split3(total))
        tot = jnp.where(sub16 == 0, t1, jnp.where(sub16 == 1, t2, jnp.where(sub16 == 2, t3, 0.0)))
        stack = jnp.concatenate(
            [dtv.astype(BF16), jnp.exp(cs).astype(BF16), jnp.exp(total - cs).astype(BF16), tot.astype(BF16)], axis=0)
        ex = jnp.dot(stack, e_ref[d], preferred_element_type=F32)
        dt_x = ex[0:L]
        ecs_x = ex[L:2 * L]
        wend_x = ex[2 * L:3 * L]
        edec_x = jnp.exp(ex[3 * L:3 * L + 1] + ex[3 * L + 1:3 * L + 2] + ex[3 * L + 2:3 * L + 3])

        groups = []
        for g in range(SSD_GROUPS):
            sl = slice(g * gw, (g + 1) * gw)
            xs_g = x_ref[0, rows, sl]
            b_g = x_ref[0, rows, width + g * SSD_STATE:width + (g + 1) * SSD_STATE]
            c_g = x_ref[0, rows, width + (SSD_GROUPS + g) * SSD_STATE:width + (SSD_GROUPS + g + 1) * SSD_STATE]
            cb = c_g.astype(BF16)
            xd = xs_g * dt_x[:, sl]
            gram = lax.dot_general(cb, b_g.astype(BF16), (((1,), (1,)), ((), ())), preferred_element_type=F32)
            scores = []
            for hh in range(hpg):
                li = SSD_HEADS * d + hpg * g + hh
                seg = cs[:, li:li + 1] - cs_t[li:li + 1, :]
                scores.append((gram * jnp.where(mask, jnp.exp(seg), 0.0)).astype(BF16))
            scores = jnp.concatenate(scores, axis=1)
            xd_blocks = jnp.concatenate(
                [jnp.where(lane_head == hh, xd, 0.0).astype(BF16) for hh in range(hpg)], axis=0)
            y = jnp.dot(scores, xd_blocks, preferred_element_type=F32)
            xdw = (xd * wend_x[:, sl]).astype(BF16)
            groups.append((y, cb, b_g.T.astype(BF16), xdw, ecs_x[:, sl], edec_x[:, sl]))

        def carry():
            for g, (y, cb, bt, xdw, ecs_g, edec_g) in enumerate(groups):
                sl = slice(g * gw, (g + 1) * gw)
                st = st_ref[d, :, sl]
                y_ref[0, rows, sl] = (y + jnp.dot(cb, st.astype(BF16), preferred_element_type=F32) * ecs_g
                                      ).astype(y_ref.dtype)
                st_ref[d, :, sl] = st * edec_g + jnp.dot(bt, xdw, preferred_element_type=F32)

        return carry

    carries = [chunk(0, xf_ref, dtf_ref, yf_ref, 0), chunk(1, xb_ref, dtb_ref, yb_ref, L),
               chunk(0, xf_ref, dtf_ref, yf_ref, L), chunk(1, xb_ref, dtb_ref, yb_ref, 0)]
    for carry in carries:
        carry()


def _ssd(xbc, dt, dt_bias, a_log, n_ctx):
    b, t, ch = xbc.shape
    L = 2 * SSD_CHUNK
    assert n_ctx % L == 0 and t % L == 0
    n = t // L
    width = SSD_HEADS * HEAD_DIM
    ncc = n_ctx // L

    def fwd(i, j):
        return (i, j, 0)

    def bwd(i, j):
        return (i, jnp.where(j < ncc, ncc - 1 - j, n + ncc - 1 - j), 0)

    pad = lambda v: jnp.pad(v.reshape(1, -1), ((0, 0), (0, 128 - v.size)))
    bias = pad(dt_bias)
    a_neg = pad(-jnp.exp(a_log))
    head_of_lane = jnp.arange(width) // HEAD_DIM
    expand = jnp.stack([
        (jnp.arange(128)[:, None] == SSD_HEADS * d + head_of_lane[None, :]) for d in range(2)
    ]).astype(BF16)
    return pl.pallas_call(
        _ssd_kernel,
        grid=(b, n),
        in_specs=[
            pl.BlockSpec((1, L, ch), fwd),
            pl.BlockSpec((1, L, ch), bwd),
            pl.BlockSpec((1, L, 128), fwd),
            pl.BlockSpec((1, L, 128), bwd),
            pl.BlockSpec((1, 128), lambda i, j: (0, 0)),
            pl.BlockSpec((1, 128), lambda i, j: (0, 0)),
            pl.BlockSpec((2, 128, width), lambda i, j: (0, 0, 0)),
        ],
        out_specs=[pl.BlockSpec((1, L, width), fwd), pl.BlockSpec((1, L, width), bwd)],
        out_shape=[jax.ShapeDtypeStruct((b, t, width), BF16)] * 2,
        scratch_shapes=[pltpu.VMEM((2, SSD_STATE, width), F32)],
        compiler_params=_params(("parallel", "arbitrary")),
        name="ssd",
    )(xbc, xbc, dt, dt, bias, a_neg, expand)


def _s5_matrices_kernel(z_ref, ak_ref, bb_ref, cc_ref, m_ref, smat_ref, ot_ref):
    nb, width = S5_BLOCK, S5_BLOCK * S5_GROUP
    lanes = z_ref.shape[-1]

    def rows(g, x_ref, kf, kb, conj):
        out = []
        for d, k in ((0, kf), (1, kb)):
            xr, xi = x_ref[g, 2 * d], x_ref[g, 2 * d + 1]
            ar, ai = ak_ref[g, 2 * d, k:k + 1, :], ak_ref[g, 2 * d + 1, k:k + 1, :]
            im = xr * ai + xi * ar
            out.append((xr * ar - xi * ai, -im if conj else im))
        (f_re, f_im), (b_re, b_im) = out
        c0, c1, c2, c3 = (f_re, b_re, f_im, b_im) if g % 2 == 0 else (b_re, f_re, b_im, f_im)
        return jnp.concatenate([c0 + pltpu.roll(c1, S5_STATE, 1), c2 + pltpu.roll(c3, S5_STATE, 1)], axis=1)

    for g in range(z_ref.shape[0]):
        z = z_ref[g]
        for s in range(nb):
            blk = slice(s * S5_GROUP, (s + 1) * S5_GROUP)
            off = (nb - 1 - s) * S5_GROUP
            win = pltpu.roll(z, (lanes - off) % lanes, 1) if off else z
            m_ref[g, blk, :] = win[:, :width].astype(m_ref.dtype)
            smat_ref[g, blk, :] = rows(g, bb_ref, nb - 1 - s, s, False).astype(smat_ref.dtype)
            ot_ref[g, blk, :] = rows(g, cc_ref, s + 1, nb - s, True).astype(ot_ref.dtype)


def _s5_matrices(zall, ak, bb, cc):
    ng, nj, w = zall.shape
    lanes = -(-w // 128) * 128
    per = 8
    width = S5_BLOCK * S5_GROUP
    blk = lambda a: pl.BlockSpec((per,) + a.shape[1:], lambda i: (i,) + (0,) * (a.ndim - 1))
    zall = jnp.pad(zall, ((0, 0), (0, 0), (0, lanes - w)))
    out = jax.ShapeDtypeStruct((ng, S5_BLOCK * nj, width), BF16)
    return pl.pallas_call(
        _s5_matrices_kernel,
        grid=(ng // per,),
        in_specs=[blk(zall), blk(ak), blk(bb), blk(cc)],
        out_specs=[pl.BlockSpec((per, S5_BLOCK * nj, width), lambda i: (i, 0, 0))] * 3,
        out_shape=[out] * 3,
        compiler_params=_params(("parallel",)),
        name="s5_matrices",
    )(zall, ak, bb, cc)


def _s5_operators(lam_re, lam_im, log_step, b_re, b_im, c_re, c_im):
    nb = S5_BLOCK
    ng, ns = lam_re.shape[1:]
    hp = functools.partial(jnp.einsum, precision=HIGHEST)
    step = jnp.exp(log_step)[..., None]
    k = jnp.arange(nb + 1, dtype=F32)[:, None, None, None]
    mag = jnp.exp(k * (lam_re * step))
    ak_re = mag * jnp.cos(k * (lam_im * step))
    ak_im = mag * jnp.sin(k * (lam_im * step))
    ab_re, ab_im = ak_re[1], ak_im[1]
    den = lam_re * lam_re + lam_im * lam_im
    f_re = ((ab_re - 1.0) * lam_re + ab_im * lam_im) / den
    f_im = (ab_im * lam_re - (ab_re - 1.0) * lam_im) / den
    tr = lambda a: jnp.swapaxes(a, -1, -2)
    bb_re = tr(f_re[..., None] * b_re - f_im[..., None] * b_im)
    bb_im = tr(f_re[..., None] * b_im + f_im[..., None] * b_re)
    akr, aki = ak_re[:nb, :, :, None, :], ak_im[:nb, :, :, None, :]
    w_re = akr * bb_re - aki * bb_im
    w_im = akr * bb_im + aki * bb_re
    kern = hp('kdgjp,dgip->kdgji', w_re, c_re) - hp('kdgjp,dgip->kdgji', w_im, c_im)
    kf, kb = kern[:, 0], kern[:, 1]
    zall = jnp.concatenate([kb[:0:-1], (kf[0] + kb[0])[None], kf[1:]], axis=0).transpose(1, 2, 0, 3)
    comps = lambda re, im: jnp.stack([re[0], im[0], re[1], im[1]], axis=1)
    lane_pad = lambda a, r: jnp.pad(a, ((0, 0), (0, 0), (0, r - a.shape[2]), (0, 128 - ns)))
    ak = lane_pad(comps(ak_re.transpose(1, 2, 0, 3), ak_im.transpose(1, 2, 0, 3)), 8 * (-(-(nb + 1) // 8)))
    m, smat, ot = _s5_matrices(zall.reshape(ng, S5_GROUP, (2 * nb - 1) * S5_GROUP), ak,
                               lane_pad(comps(bb_re, bb_im), S5_GROUP), lane_pad(comps(c_re, c_im), S5_GROUP))
    swap = lambda a: a.reshape(ng // 2, 2, ns)[:, ::-1].reshape(ng * ns)
    dec = jnp.stack([ak_re[nb, 0].reshape(-1), ak_im[nb, 0].reshape(-1), swap(ak_re[nb, 1]), swap(ak_im[nb, 1])])
    return smat, m, ot, dec


def _s5_kernel(u_ref, smat_ref, m_ref, ot_ref, dec_ref, y_ref, us, uf, ys, sfr, sbr, sfi, sbi, hfr, hbr, hfi, hbi, *,
               n_ctx_blocks):
    nb = S5_BLOCK
    groups = smat_ref.shape[0]
    n = u_ref.shape[1] // nb
    ncb = n_ctx_blocks
    seg = S5_GROUP
    per_col = 128 // seg
    lane_seg = lax.broadcasted_iota(jnp.int32, (n, 128), 1) // seg
    low = lax.broadcasted_iota(jnp.int32, (n, 128), 1) < S5_STATE

    for s in range(nb):
        us[s] = u_ref[0, pl.ds(s, n, stride=nb), :]

    def fold(g):
        for v in range(nb // per_col):
            col = None
            for k in range(per_col):
                x = us[v * per_col + k]
                shift = (seg * (k - g)) % 128
                if shift:
                    x = pltpu.roll(x, shift, 1)
                col = x if col is None else jnp.where(lane_seg == k, x, col)
            uf[g, :, 128 * v:128 * (v + 1)] = col.astype(BF16)
        return jnp.dot(uf[g], smat_ref[g], preferred_element_type=F32)

    for p in range(groups // 2):
        se, so = fold(2 * p), fold(2 * p + 1)
        sl = slice(128 * p, 128 * (p + 1))
        sfr[:, sl] = jnp.where(low, se[:, :128], so[:, :128])
        sbr[:, sl] = jnp.where(low, so[:, :128], se[:, :128])
        sfi[:, sl] = jnp.where(low, se[:, 128:], so[:, 128:])
        sbi[:, sl] = jnp.where(low, so[:, 128:], se[:, 128:])
    arf, aif, arb, aib = dec_ref[0, 0:1], dec_ref[0, 1:2], dec_ref[0, 2:3], dec_ref[0, 3:4]

    def step(i, carry):
        fr, fi, br, bi = carry
        rf = i
        rb = jnp.where(i < ncb, ncb - 1 - i, n + ncb - 1 - i)
        hfr[pl.ds(rf, 1), :] = fr
        hfi[pl.ds(rf, 1), :] = fi
        hbr[pl.ds(rb, 1), :] = br
        hbi[pl.ds(rb, 1), :] = bi
        nfr = arf * fr - aif * fi + sfr[pl.ds(rf, 1), :]
        nfi = arf * fi + aif * fr + sfi[pl.ds(rf, 1), :]
        nbr = arb * br - aib * bi + sbr[pl.ds(rb, 1), :]
        nbi = arb * bi + aib * br + sbi[pl.ds(rb, 1), :]
        return nfr, nfi, nbr, nbi

    zero = jnp.zeros((1, 128 * (groups // 2)), F32)
    lax.fori_loop(0, n, step, (zero, zero, zero, zero), unroll=4)
    for g in range(groups):
        sl = slice(128 * (g // 2), 128 * (g // 2 + 1))
        first, second = ((hfr, hfi), (hbr, hbi)) if g % 2 == 0 else ((hbr, hbi), (hfr, hfi))
        states = jnp.concatenate(
            [jnp.where(low, a[:, sl], b[:, sl]).astype(BF16) for a, b in zip(first, second)], axis=1)
        ys[g] = (jnp.dot(uf[g], m_ref[g], preferred_element_type=F32)
                 + lax.dot_general(states, ot_ref[g], (((1,), (1,)), ((), ())), preferred_element_type=F32))
    for l in range(nb):
        v, k = divmod(l, per_col)
        out = None
        for g in range(groups):
            x = ys[g, :, 128 * v:128 * (v + 1)]
            shift = (seg * (g - k)) % 128
            if shift:
                x = pltpu.roll(x, shift, 1)
            out = x if out is None else jnp.where(lane_seg == g, x, out)
        y_ref[0, pl.ds(l, n, stride=nb), :] = out


def _s5(u, ops, n_ctx_blocks):
    smat, m, ow, dec = ops
    b, t, w = u.shape
    nb = S5_BLOCK
    n = t // nb
    gw = nb * S5_GROUP
    gpb = 128 // S5_GROUP
    steps = w // 128
    lanes = gpb * S5_STATE
    dec = dec.reshape(4, steps, lanes).transpose(1, 0, 2)
    dec = jnp.concatenate([dec, jnp.zeros_like(dec)], axis=1)
    return pl.pallas_call(
        functools.partial(_s5_kernel, n_ctx_blocks=n_ctx_blocks),
        grid=(b, steps),
        in_specs=[
            pl.BlockSpec((1, t, 128), lambda i, j: (i, 0, j)),
            pl.BlockSpec((gpb,) + smat.shape[1:], lambda i, j: (j, 0, 0)),
            pl.BlockSpec((gpb,) + m.shape[1:], lambda i, j: (j, 0, 0)),
            pl.BlockSpec((gpb,) + ow.shape[1:], lambda i, j: (j, 0, 0)),
            pl.BlockSpec((1, 8, lanes), lambda i, j: (j, 0, 0)),
        ],
        out_specs=pl.BlockSpec((1, t, 128), lambda i, j: (i, 0, j)),
        out_shape=jax.ShapeDtypeStruct((b, t, w), F32),
        scratch_shapes=[pltpu.VMEM((nb, n, 128), F32), pltpu.VMEM((gpb, n, gw), BF16), pltpu.VMEM((gpb, n, gw), F32)]
        + [pltpu.VMEM((n, lanes), F32)] * 8,
        compiler_params=_params(("parallel", "parallel")),
        name="s5",
    )(u, smat, m, ow, dec)


def _even_out_kernel(yf_ref, yb_ref, xs_ref, z_ref, y5_ref, u_ref, g_ref, ctx_ref, x_ref, mod_ref, vs_ref, v5_ref,
                     glu_ref, w_ref, o_ref, *, n_ctx_tiles):
    ws = z_ref.shape[-1]
    y = _gelu_tanh(y5_ref[0] + v5_ref[0:1] * u_ref[0])
    glu = jnp.dot(y.astype(BF16), glu_ref[...], preferred_element_type=F32)
    ys = yf_ref[0].astype(F32) + yb_ref[0].astype(F32) + vs_ref[1:2] * xs_ref[0]
    s = _rms(ys * _silu(z_ref[0].astype(F32))) * vs_ref[0:1]
    o = jnp.dot(s.astype(BF16), w_ref[0:ws], preferred_element_type=F32)
    y = y * _sigmoid(glu + v5_ref[1:2]) * _silu(g_ref[0].astype(F32))
    o = o + jnp.dot(y.astype(BF16), w_ref[ws:], preferred_element_type=F32)
    o_ref[0] = _stream_tile(ctx_ref, x_ref, n_ctx_tiles) + mod_ref[0, 0][2:3] * o


def _even_out(yf, yb, xbc, z, y5, u, g, ctx, x, mod, ssd_norm, d_ssd, d_s5, glu_w, glu_b, w_out, n_ctx_tiles):
    b, t = z.shape[:2]
    d = x.shape[-1]
    tm = TOKEN_TILE
    ws, w5 = z.shape[-1], u.shape[-1]
    vs = jnp.pad(jnp.stack([ssd_norm, jnp.repeat(d_ssd, HEAD_DIM)]), ((0, 6), (0, 0)))
    v5 = jnp.pad(jnp.stack([d_s5, glu_b]), ((0, 6), (0, 0)))
    tok = lambda n: pl.BlockSpec((1, tm, n), lambda i, j: (i, j, 0))
    const = lambda a: pl.BlockSpec(a.shape, lambda i, j: (0,) * a.ndim)
    glu_w = glu_w.astype(BF16)
    w_out = w_out.astype(BF16)
    return pl.pallas_call(
        functools.partial(_even_out_kernel, n_ctx_tiles=n_ctx_tiles),
        grid=(b, t // tm),
        in_specs=[tok(ws), tok(ws), tok(ws), tok(ws), tok(w5), tok(w5), tok(w5)] + _stream_specs(tm, d, n_ctx_tiles) + [
            pl.BlockSpec((1, 1, 3, d), lambda i, j: (i, (j >= n_ctx_tiles).astype(jnp.int32), 0, 0)),
            const(vs), const(v5), const(glu_w), const(w_out),
        ],
        out_specs=tok(d),
        out_shape=jax.ShapeDtypeStruct((b, t, d), F32),
        compiler_params=_params(("parallel", "parallel")),
        name="even_out",
    )(yf, yb, xbc, z, y5, u, g, ctx, x, mod, vs, v5, glu_w, w_out)


def _odd_in_kernel(h_ref, mod_ref, w_ref, qg_ref, kg_ref, cos_ref, sin_ref, ones_ref, q_ref, k_ref, v_ref, g_ref, *,
                   q_w, kv_w):
    tm = TOKEN_TILE
    a = _norm_mod(h_ref[0], mod_ref[0, 0]).astype(BF16)
    cosv = cos_ref[...]
    sinv = sin_ref[...]
    first_half = (lax.broadcasted_iota(jnp.int32, (tm, 128), 1) % (HEAD_DIM // 2)) < (HEAD_DIM // 4)

    def project(lo, width=256):
        return jnp.dot(a, w_ref[:, lo:lo + width], preferred_element_type=F32)

    def head_norm_rope(x, gain, out_ref, c, transposed):
        ms = jnp.dot((x * x).astype(BF16), ones_ref[...], preferred_element_type=F32) * (1.0 / HEAD_DIM)
        xn = x * lax.rsqrt(ms + NORM_EPS) * gain
        for s in range(2):
            xb = xn[:, 128 * s:128 * (s + 1)]
            partner = jnp.where(first_half, pltpu.roll(xb, 128 - HEAD_DIM // 4, 1), pltpu.roll(xb, HEAD_DIM // 4, 1))
            r = xb * cosv + partner * sinv
            head = 4 * c + 2 * s
            if transposed:
                rt = r.T
                out_ref[0, head] = rt[:HEAD_DIM].astype(out_ref.dtype)
                out_ref[0, head + 1] = rt[HEAD_DIM:].astype(out_ref.dtype)
            else:
                out_ref[0, head] = r[:, :HEAD_DIM].astype(out_ref.dtype)
                out_ref[0, head + 1] = pltpu.roll(r, HEAD_DIM, 1)[:, :HEAD_DIM].astype(out_ref.dtype)

    chunks = [(256 * c, qg_ref, q_ref, c) for c in range(q_w // 256)] + \
             [(q_w + 256 * c, kg_ref, k_ref, c) for c in range(kv_w // 256)]
    g_lo = q_w + 2 * kv_w
    g_w = w_ref.shape[1] - g_lo
    x_next = project(chunks[0][0])
    for i, (lo, gain_ref, out_ref, c) in enumerate(chunks):
        x = x_next
        if i + 1 < len(chunks):
            x_next = project(chunks[i + 1][0])
        if 256 * i < g_w:
            g_ref[0, :, 256 * i:256 * (i + 1)] = project(g_lo + 256 * i).astype(g_ref.dtype)
        head_norm_rope(x, gain_ref[:, 256 * c:256 * (c + 1)], out_ref, c, out_ref is q_ref)
    assert 256 * len(chunks) >= g_w
    lane = lax.broadcasted_iota(jnp.int32, (tm, 128), 1)
    vt_rows = v_ref.shape[2]
    for c in range(kv_w // 128):
        x = project(q_w + kv_w + 128 * c, 128)
        for s in range(2):
            xs = pltpu.roll(x, HEAD_DIM, 1) if s else x
            vh = jnp.where(lane < HEAD_DIM, xs, jnp.where(lane == HEAD_DIM, 1.0, 0.0))
            v_ref[0, 2 * c + s] = vh.T[:vt_rows].astype(v_ref.dtype)


def _rope_tables(n_ctx, n_lat):
    pairs = HEAD_DIM // 4
    pos = jnp.arange(n_lat)
    row = (pos // GRID_W).astype(F32)
    colp = (pos % GRID_W).astype(F32)
    inv = ROPE_THETA ** (-jnp.arange(pairs, dtype=F32) / pairs)
    lane = jnp.arange(128) % HEAD_DIM
    axis_is_col = (lane // (HEAD_DIM // 2)) == 1
    ang = jnp.where(axis_is_col[None, :], colp[:, None], row[:, None]) * inv[lane % pairs][None, :]
    sign = jnp.where((lane % (HEAD_DIM // 2)) < pairs, -1.0, 1.0)
    cos = jnp.concatenate([jnp.ones((n_ctx, 128), F32), jnp.cos(ang)], axis=0)
    sin = jnp.concatenate([jnp.zeros((n_ctx, 128), F32), jnp.sin(ang) * sign[None, :]], axis=0)
    return cos, sin


def _odd_in(h, mod, w, q_gain, k_gain, cos, sin, n_ctx_tiles):
    b, t, d = h.shape
    tm = TOKEN_TILE
    q_w = ATTN_Q_HEADS * HEAD_DIM
    kv_w = ATTN_KV_HEADS * HEAD_DIM
    qg = (jnp.tile(q_gain, ATTN_Q_HEADS) * (HEAD_DIM ** -0.5 * math.log2(math.e))).reshape(1, q_w)
    kg = jnp.tile(k_gain, ATTN_KV_HEADS).reshape(1, kv_w)
    blk = jnp.arange(256) // HEAD_DIM
    ones = (blk[:, None] == blk[None, :]).astype(BF16)
    tok = lambda n: pl.BlockSpec((1, tm, n), lambda i, j: (i, j, 0))
    heads = lambda nh, n: pl.BlockSpec((1, nh, tm, n), lambda i, j: (i, 0, j, 0))
    heads_t = lambda nh, n: pl.BlockSpec((1, nh, n, tm), lambda i, j: (i, 0, 0, j))
    const = lambda a: pl.BlockSpec(a.shape, lambda i, j: (0,) * a.ndim)
    return pl.pallas_call(
        functools.partial(_odd_in_kernel, q_w=q_w, kv_w=kv_w),
        grid=(b, t // tm),
        in_specs=[
            tok(d),
            pl.BlockSpec((1, 1, 3, d), lambda i, j: (i, (j >= n_ctx_tiles).astype(jnp.int32), 0, 0)),
            const(w), const(qg), const(kg),
            pl.BlockSpec((tm, 128), lambda i, j: (j, 0)),
            pl.BlockSpec((tm, 128), lambda i, j: (j, 0)),
            const(ones),
        ],
        out_specs=[
            pl.BlockSpec((1, ATTN_Q_HEADS, HEAD_DIM, tm), lambda i, j: (i, 0, 0, jnp.maximum(j - n_ctx_tiles, 0))),
            heads(ATTN_KV_HEADS, HEAD_DIM), heads_t(ATTN_KV_HEADS, VT_ROWS), tok(q_w)],
        out_shape=[
            jax.ShapeDtypeStruct((b, ATTN_Q_HEADS, HEAD_DIM, t - n_ctx_tiles * tm), BF16),
            jax.ShapeDtypeStruct((b, ATTN_KV_HEADS, t, HEAD_DIM), BF16),
            jax.ShapeDtypeStruct((b, ATTN_KV_HEADS, VT_ROWS, t), BF16),
            jax.ShapeDtypeStruct((b, t, q_w), BF16),
        ],
        compiler_params=_params(("parallel", "arbitrary")),
        name="odd_in",
    )(h, mod, w, qg, kg, cos, sin, ones)


def _attn_kernel(q_ref, k_ref, v_ref, o_ref, *, tk):
    rep, hd, tq = q_ref.shape[1:]
    rows = v_ref.shape[2]
    nk = k_ref.shape[2] // tk
    m = [jnp.full((1, tq), -1e30, F32)] * rep
    acc = [jnp.zeros((rows, tq), F32)] * rep
    blocks = [(j, r) for j in range(nk) for r in range(rep)]
    scores = {}
    for i in range(len(blocks) + ATTN_LOOKAHEAD):
        if i < len(blocks):
            j, r = blocks[i]
            scores[i] = jnp.dot(k_ref[0, 0, j * tk:(j + 1) * tk, :], q_ref[0, r], preferred_element_type=F32)
        if i >= ATTN_LOOKAHEAD:
            j, r = blocks[i - ATTN_LOOKAHEAD]
            s = scores.pop(i - ATTN_LOOKAHEAD)
            m_new = jnp.maximum(m[r], jnp.max(s, axis=0, keepdims=True))
            p = jnp.exp2(s - m_new).astype(BF16)
            pv = jnp.dot(v_ref[0, 0, :, j * tk:(j + 1) * tk], p, preferred_element_type=F32)
            acc[r] = jnp.exp2(m[r] - m_new) * acc[r] + pv
            m[r] = m_new
    for c in range(rep // 2):
        pair = [acc[r][:hd] * (1.0 / acc[r][hd:hd + 1]) for r in (2 * c, 2 * c + 1)]
        o_ref[0, :, 2 * hd * c:2 * hd * (c + 1)] = jnp.concatenate(pair, axis=0).T.astype(o_ref.dtype)


def _attention(q, k, v):
    b, hq, hd, n_lat = q.shape
    hkv, t = k.shape[1:3]
    rep = hq // hkv
    tq = ATTN_Q_TILE
    tk = next(c for c in (256, 128) if t % c == 0)
    assert n_lat % tq == 0
    return pl.pallas_call(
        functools.partial(_attn_kernel, tk=tk),
        grid=(b, hkv, n_lat // tq),
        in_specs=[
            pl.BlockSpec((1, rep, hd, tq), lambda i, j, n: (i, j, 0, n)),
            pl.BlockSpec((1, 1, t, hd), lambda i, j, n: (i, j, 0, 0)),
            pl.BlockSpec((1, 1, v.shape[2], t), lambda i, j, n: (i, j, 0, 0)),
        ],
        out_specs=pl.BlockSpec((1, tq, rep * hd), lambda i, j, n: (i, n, j)),
        out_shape=jax.ShapeDtypeStruct((b, n_lat, hq * hd), BF16),
        compiler_params=_params(("parallel", "parallel", "parallel")),
        name="attention",
    )(q, k, v)


def _attn_out_kernel(o_ref, g_ref, h_ref, mod_ref, w_ref, fg_ref, out_ref):
    x = o_ref[0].astype(F32) * _silu(g_ref[0].astype(F32))
    y = jnp.dot(x.astype(BF16), w_ref[...], preferred_element_type=F32)
    hn = h_ref[0] + mod_ref[0, 0][2:3] * y
    out_ref[0] = _rms(hn) * fg_ref[...]


def _attn_out(o, g, h, mod, w_out, final_gain, n_ctx_tiles):
    b, n_lat, d = o.shape
    tm = TOKEN_TILE
    lat = lambda n: pl.BlockSpec((1, tm, n), lambda i, j: (i, j + n_ctx_tiles, 0))
    w_out = w_out.astype(BF16)
    return pl.pallas_call(
        _attn_out_kernel,
        grid=(b, n_lat // tm),
        in_specs=[
            pl.BlockSpec((1, tm, d), lambda i, j: (i, j, 0)),
            lat(d), lat(d),
            pl.BlockSpec((1, 1, 3, d), lambda i, j: (i, 1, 0, 0)),
            pl.BlockSpec(w_out.shape, lambda i, j: (0, 0)),
            pl.BlockSpec((1, d), lambda i, j: (0, 0)),
        ],
        out_specs=pl.BlockSpec((1, tm, d), lambda i, j: (i, j, 0)),
        out_shape=jax.ShapeDtypeStruct((b, n_lat, d), F32),
        compiler_params=_params(("parallel", "parallel")),
        name="attn_out",
    )(o, g, h, mod, w_out, final_gain.reshape(1, d))


def kernel(x, c, ctx, c_ctx, ada_w, ada_b, ev_w_in, ev_conv_w, ev_conv_b, ev_dt_bias, ev_a_log, ev_d_ssd, ev_ssd_norm, ev_lam_re, ev_lam_im, ev_log_step, ev_b_re, ev_b_im, ev_c_re, ev_c_im, ev_d_s5, ev_glu_w, ev_glu_b, ev_w_out, od_w_in, od_q_gain, od_k_gain, od_w_out, final_gain):
    b, n_lat, d = x.shape
    n_ctx = ctx.shape[1]
    assert ada_w.shape[0] == 2 and n_ctx % TOKEN_TILE == 0 and n_lat % TOKEN_TILE == 0
    n_ctx_tiles = n_ctx // TOKEN_TILE
    mods = _adaln(c, c_ctx, ada_w, ada_b)

    ws = SSD_HEADS * HEAD_DIM
    wx = ws + 2 * SSD_GROUPS * SSD_STATE
    w5 = ev_d_s5.shape[-1]
    w = ev_w_in[0]
    cuts = (ws, ws + wx, ws + wx + 2 * SSD_HEADS, ws + wx + 2 * SSD_HEADS + w5)
    w_dt = jnp.pad(w[:, cuts[1]:cuts[2]], ((0, 0), (0, 128 - 2 * SSD_HEADS)))
    w_dt_hi, w_dt_lo = _split_bf16(w_dt)
    w_cat = jnp.concatenate(
        [w[:, :cuts[1]].astype(BF16), w[:, cuts[2]:].astype(BF16), w_dt_hi, w_dt_lo], axis=1)
    z, xbc, u, g, dt = _even_in(ctx, x, mods[0], w_cat, n_ctx_tiles, (ws, wx, w5, w5))
    xbc = _conv(xbc, ev_conv_w[0], ev_conv_b[0], n_ctx_tiles)
    yf, yb = _ssd(xbc, dt, ev_dt_bias[0], ev_a_log[0], n_ctx)
    ops = _s5_operators(ev_lam_re[0], ev_lam_im[0], ev_log_step[0], ev_b_re[0], ev_b_im[0], ev_c_re[0], ev_c_im[0])
    y5 = _s5(u, ops, n_ctx // S5_BLOCK)
    h = _even_out(yf, yb, xbc, z, y5, u, g, ctx, x, mods[0], ev_ssd_norm[0], ev_d_ssd[0], ev_d_s5[0], ev_glu_w[0],
                  ev_glu_b[0], ev_w_out[0], n_ctx_tiles)

    cos, sin = _rope_tables(n_ctx, n_lat)
    q, k, v, g = _odd_in(h, mods[1], od_w_in[0].astype(BF16), od_q_gain[0], od_k_gain[0], cos, sin, n_ctx_tiles)
    o = _attention(q, k, v)
    return _attn_out(o, g, h, mods[1], od_w_out[0], final_gain, n_ctx_tiles)
```
